```python
import math
import jax
import jax.numpy as jnp
from jax import lax
import numpy as np

D_MODEL = 1024
BATCH = 1
SEQ = 16384
DEPTH = 2
DEC_BATCH = 4
DEC_SEQ = 4096
PAST_LEN = 128

HY_WIDTH = D_MODEL // 2
HY_ORDER = 2
SHORT_CONV = 3
FILTER_EMB = 33
FILTER_HIDDEN = 64
DECAY_TARGET = 0.01
FAST_DECAY_PCT = 0.3
SLOW_DECAY_PCT = 1.5
HEAD_DIM = 64
N_HEADS = 8
N_KV_HEADS = 2
ATT_WIDTH = N_HEADS * HEAD_DIM
KV_WIDTH = N_KV_HEADS * HEAD_DIM
WINDOW = 128
BLOCK = 128
ROPE_THETA = 10000.0
N_BRANCH = 2
IN_WIDTH = 3 * HY_WIDTH + ATT_WIDTH + 2 * KV_WIDTH + N_BRANCH * D_MODEL
N_EXPERTS = 16
CAPACITY_FACTOR = 2
D_FF = 2816
EPS = 1e-6

kernel_name = 'hyena_swa_ec_moe_adaln_encoder'


def rms_norm(x, g):
    xf = x.astype(jnp.float32)
    y = xf * lax.rsqrt(jnp.mean(xf * xf, axis=-1, keepdims=True) + EPS)
    return (y * g.astype(jnp.float32)).astype(x.dtype)


def short_conv(u, w, b):
    up = jnp.pad(u, ((0, 0), (1, 1), (0, 0)))
    return up[:, :-2] * w[0] + up[:, 1:-1] * w[1] + up[:, 2:] * w[2] + b


def hyena_filters(L, p):
    t = jnp.linspace(0.0, 1.0, L, dtype=jnp.float32)[:, None]
    bands = (FILTER_EMB - 1) // 2
    f = jnp.linspace(1e-4, bands - 1, bands, dtype=jnp.float32)[None, :]
    w = (2.0 * math.pi / L) * jnp.arange(L, dtype=jnp.float32)[:, None]
    z = jnp.concatenate([t, jnp.cos(f * w), -jnp.sin(f * w)], axis=-1)
    fr = p['flt_freq'].astype(jnp.float32)
    h = jnp.sin(fr * (z @ p['flt_w1'].astype(jnp.float32) + p['flt_b1'].astype(jnp.float32)))
    h = jnp.sin(fr * (h @ p['flt_w2'].astype(jnp.float32) + p['flt_b2'].astype(jnp.float32)))
    h = jnp.sin(fr * (h @ p['flt_w3'].astype(jnp.float32) + p['flt_b3'].astype(jnp.float32)))
    h = (h @ p['flt_w_out'].astype(jnp.float32)).reshape(L, 2, HY_ORDER, HY_WIDTH)
    max_decay = math.log(DECAY_TARGET) / FAST_DECAY_PCT
    min_decay = math.log(DECAY_TARGET) / SLOW_DECAY_PCT
    deltas = jnp.abs(jnp.linspace(min_decay, max_decay, HY_WIDTH, dtype=jnp.float32))
    h = h * jnp.exp(-t * deltas)[:, None, None, :]
    fwd, bwd = h[:, 0], h[:, 1]
    k = jnp.concatenate([fwd, jnp.zeros((1, HY_ORDER, HY_WIDTH), jnp.float32), bwd[:0:-1]], axis=0)
    k = k * lax.rsqrt(jnp.sum(k * k, axis=0, keepdims=True) + EPS)
    return jnp.fft.rfft(k, n=2 * L, axis=0)


def long_conv(u, kf):
    L = u.shape[1]
    uf = jnp.fft.rfft(u.astype(jnp.float32), n=2 * L, axis=1)
    y = jnp.fft.irfft(uf * kf[None], n=2 * L, axis=1)[:, :L]
    return y.astype(u.dtype)


def rope_tables(L):
    inv = 1.0 / (ROPE_THETA ** (jnp.arange(0, HEAD_DIM, 2, dtype=jnp.float32) / HEAD_DIM))
    ang = jnp.arange(L, dtype=jnp.float32)[:, None] * inv[None, :]
    return jnp.cos(ang), jnp.sin(ang)


def apply_rope(x, cos, sin):
    half = HEAD_DIM // 2
    xf = x.astype(jnp.float32)
    x1, x2 = xf[..., :half], xf[..., half:]
    c = cos[None, :, None, :]
    s = sin[None, :, None, :]
    return jnp.concatenate([x1 * c - x2 * s, x2 * c + x1 * s], axis=-1).astype(x.dtype)


def band_attention(q, k, v, sink):
    B, L = q.shape[0], q.shape[1]
    nb = L // BLOCK
    G = N_HEADS // N_KV_HEADS

    def kv_blocks(t):
        tp = jnp.pad(t, ((0, 0), (BLOCK, BLOCK), (0, 0), (0, 0)))
        return jnp.concatenate(
            [tp[:, i * BLOCK:i * BLOCK + L].reshape(B, nb, BLOCK, N_KV_HEADS, HEAD_DIM) for i in range(3)],
            axis=2)

    kb, vb = kv_blocks(k), kv_blocks(v)
    qb = q.reshape(B, nb, BLOCK, N_KV_HEADS, G, HEAD_DIM)
    s = jnp.einsum('bnqkgd,bnskd->bnkgqs', qb, kb, preferred_element_type=jnp.float32) * (HEAD_DIM ** -0.5)
    qpos = jnp.arange(nb)[:, None, None] * BLOCK + jnp.arange(BLOCK)[None, :, None]
    kpos = (jnp.arange(nb)[:, None, None] - 1) * BLOCK + jnp.arange(3 * BLOCK)[None, None, :]
    mask = (jnp.abs(kpos - qpos) <= WINDOW) & (kpos >= 0) & (kpos < L)
    s = jnp.where(mask[None, :, None, None], s, -1e30)
    sink_l = sink.astype(jnp.float32).reshape(N_KV_HEADS, G)[None, None, :, :, None, None]
    m = jnp.maximum(jnp.max(s, axis=-1, keepdims=True), sink_l)
    pr = jnp.exp(s - m)
    denom = jnp.sum(pr, axis=-1, keepdims=True) + jnp.exp(sink_l - m)
    o = jnp.einsum('bnkgqs,bnskd->bnqkgd', (pr / denom).astype(v.dtype), vb)
    return o.reshape(B, L, ATT_WIDTH)


def mixer(h, p):
    B, L, _ = h.shape
    proj = h @ p['w_in']
    s0 = 3 * HY_WIDTH
    s1 = s0 + ATT_WIDTH
    s2 = s1 + KV_WIDTH
    s3 = s2 + KV_WIDTH
    hy, q, k, v, gates = proj[..., :s0], proj[..., s0:s1], proj[..., s1:s2], proj[..., s2:s3], proj[..., s3:]
    hy = short_conv(hy, p['hy_conv_w'], p['hy_conv_b'])
    z = hy[..., :HY_WIDTH]
    gate_list = (hy[..., HY_WIDTH:2 * HY_WIDTH], hy[..., 2 * HY_WIDTH:])
    kf = hyena_filters(L, p)
    for o in range(HY_ORDER):
        z = gate_list[o] * (long_conv(z, kf[:, o]) + p['hy_bias'][o] * z)
    y_hy = z @ p['w_hy_proj']
    cos, sin = rope_tables(L)
    q = apply_rope(rms_norm(q.reshape(B, L, N_HEADS, HEAD_DIM), p['q_norm_g']), cos, sin)
    k = apply_rope(rms_norm(k.reshape(B, L, N_KV_HEADS, HEAD_DIM), p['k_norm_g']), cos, sin)
    v = v.reshape(B, L, N_KV_HEADS, HEAD_DIM)
    y_at = band_attention(q, k, v, p['sink_logit']) @ p['w_at_proj']
    g = jax.nn.sigmoid(gates)
    merged = g[..., :D_MODEL] * y_hy + g[..., D_MODEL:] * y_at
    return merged @ p['w_o']


def expert_choice_moe(h, p):
    B, L, D = h.shape
    T = B * L
    cap = CAPACITY_FACTOR * T // N_EXPERTS
    hf = h.reshape(T, D)
    aff = jax.nn.softmax((hf @ p['router_w']).astype(jnp.float32), axis=-1)
    gate, idx = lax.top_k(aff.T, cap)
    xe = hf[idx]
    hid = jax.nn.silu(jnp.einsum('ecd,edf->ecf', xe, p['exp_w_gate'])) * jnp.einsum('ecd,edf->ecf', xe, p['exp_w_up'])
    ye = jnp.einsum('ecf,efd->ecd', hid, p['exp_w_down'])
    contrib = (gate[..., None].astype(ye.dtype) * ye).reshape(-1, D)
    y = jnp.zeros_like(hf).at[idx.reshape(-1)].add(contrib)
    return y.reshape(B, L, D)


def encoder_layer(x, c, p):
    mod = (jax.nn.silu(c) @ p['ada_w'] + p['ada_b'])[:, None, :]
    sh1, sc1, g1, sh2, sc2, g2 = jnp.split(mod, 6, axis=-1)
    h = rms_norm(x, p['norm1_g']) * (1.0 + sc1) + sh1
    x = x + g1 * mixer(h, p)
    h = rms_norm(x, p['norm2_g']) * (1.0 + sc2) + sh2
    x = x + g2 * expert_choice_moe(h, p)
    return x


def setup_inputs(seed: int = 0) -> dict:
    key = jax.random.key(seed)
    ks = jax.random.split(key, 32)
    f32 = jnp.float32

    def nrm(k, shape, scale):
        return jax.random.normal(k, shape, f32) * scale

    return {
        'x_prompt': nrm(ks[0], (BATCH, SEQ, D_MODEL), 1.0),
        'x_sample': nrm(ks[1], (DEC_BATCH, DEC_SEQ, D_MODEL), 1.0),
        'c_prompt': nrm(ks[2], (BATCH, D_MODEL), 1.0),
        'c_sample': nrm(ks[3], (DEC_BATCH, D_MODEL), 1.0),
        'ada_w': nrm(ks[4], (DEPTH, D_MODEL, 6 * D_MODEL), 0.5 * D_MODEL ** -0.5),
        'ada_b': nrm(ks[5], (DEPTH, 6 * D_MODEL), 0.02),
        'norm1_g': 1.0 + nrm(ks[6], (DEPTH, D_MODEL), 0.02),
        'norm2_g': 1.0 + nrm(ks[7], (DEPTH, D_MODEL), 0.02),
        'w_in': nrm(ks[8], (DEPTH, D_MODEL, IN_WIDTH), D_MODEL ** -0.5),
        'hy_conv_w': nrm(ks[9], (DEPTH, SHORT_CONV, 3 * HY_WIDTH), SHORT_CONV ** -0.5),
        'hy_conv_b': nrm(ks[10], (DEPTH, 3 * HY_WIDTH), 0.02),
        'flt_w1': nrm(ks[11], (DEPTH, FILTER_EMB, FILTER_HIDDEN), FILTER_EMB ** -0.5),
        'flt_b1': nrm(ks[12], (DEPTH, FILTER_HIDDEN), 0.1),
        'flt_w2': nrm(ks[13], (DEPTH, FILTER_HIDDEN, FILTER_HIDDEN), FILTER_HIDDEN ** -0.5),
        'flt_b2': nrm(ks[14], (DEPTH, FILTER_HIDDEN), 0.1),
        'flt_w3': nrm(ks[15], (DEPTH, FILTER_HIDDEN, FILTER_HIDDEN), FILTER_HIDDEN ** -0.5),
        'flt_b3': nrm(ks[16], (DEPTH, FILTER_HIDDEN), 0.1),
        'flt_freq': 1.0 + nrm(ks[17], (DEPTH, FILTER_HIDDEN), 0.02),
        'flt_w_out': nrm(ks[18], (DEPTH, FILTER_HIDDEN, 2 * HY_ORDER * HY_WIDTH), FILTER_HIDDEN ** -0.5),
        'hy_bias': nrm(ks[19], (DEPTH, HY_ORDER, HY_WIDTH), 1.0),
        'q_norm_g': 1.0 + nrm(ks[20], (DEPTH, HEAD_DIM), 0.02),
        'k_norm_g': 1.0 + nrm(ks[21], (DEPTH, HEAD_DIM), 0.02),
        'sink_logit': nrm(ks[22], (DEPTH, N_HEADS), 0.5),
        'w_hy_proj': nrm(ks[23], (DEPTH, HY_WIDTH, D_MODEL), HY_WIDTH ** -0.5),
        'w_at_proj': nrm(ks[24], (DEPTH, ATT_WIDTH, D_MODEL), ATT_WIDTH ** -0.5),
        'w_o': nrm(ks[25], (DEPTH, D_MODEL, D_MODEL), D_MODEL ** -0.5),
        'router_w': nrm(ks[26], (DEPTH, D_MODEL, N_EXPERTS), D_MODEL ** -0.5),
        'exp_w_gate': nrm(ks[27], (DEPTH, N_EXPERTS, D_MODEL, D_FF), D_MODEL ** -0.5),
        'exp_w_up': nrm(ks[28], (DEPTH, N_EXPERTS, D_MODEL, D_FF), D_MODEL ** -0.5),
        'exp_w_down': nrm(ks[29], (DEPTH, N_EXPERTS, D_FF, D_MODEL), D_FF ** -0.5),
    }


def reference(x_prompt, x_sample, c_prompt, c_sample, ada_w, ada_b, norm1_g, norm2_g, w_in, hy_conv_w, hy_conv_b,
              flt_w1, flt_b1, flt_w2, flt_b2, flt_w3, flt_b3, flt_freq, flt_w_out, hy_bias, q_norm_g, k_norm_g,
              sink_logit, w_hy_proj, w_at_proj, w_o, router_w, exp_w_gate, exp_w_up, exp_w_down):
    y_prompt = x_prompt
    y_sample = x_sample
    for l in range(DEPTH):
        p = dict(ada_w=ada_w[l], ada_b=ada_b[l], norm1_g=norm1_g[l], norm2_g=norm2_g[l], w_in=w_in[l],
                 hy_conv_w=hy_conv_w[l], hy_conv_b=hy_conv_b[l], flt_w1=flt_w1[l], flt_b1=flt_b1[l],
                 flt_w2=flt_w2[l], flt_b2=flt_b2[l], flt_w3=flt_w3[l], flt_b3=flt_b3[l], flt_freq=flt_freq[l],
                 flt_w_out=flt_w_out[l], hy_bias=hy_bias[l], q_norm_g=q_norm_g[l], k_norm_g=k_norm_g[l],
                 sink_logit=sink_logit[l], w_hy_proj=w_hy_proj[l], w_at_proj=w_at_proj[l], w_o=w_o[l],
                 router_w=router_w[l], exp_w_gate=exp_w_gate[l], exp_w_up=exp_w_up[l], exp_w_down=exp_w_down[l])
        y_prompt = encoder_layer(y_prompt, c_prompt, p)
        y_sample = encoder_layer(y_sample, c_sample, p)
    return (y_prompt, y_sample)
```

```python
import functools
import math

import numpy as np
import jax
import jax.numpy as jnp
from jax import lax
from jax.experimental import pallas as pl
from jax.experimental.pallas import tpu as pltpu

F32 = jnp.float32
BF16 = jnp.bfloat16
I32 = jnp.int32

EPS = 1e-6
HEAD_DIM = 64
N_HEADS = 8
N_KV_HEADS = 2
WINDOW = 128
ROPE_THETA = 10000.0
HY_ORDER = 2
FILTER_EMB = 33
DECAY_TARGET = 0.01
FAST_DECAY_PCT = 0.3
SLOW_DECAY_PCT = 1.5
N_EXPERTS = 16
CAPACITY_FACTOR = 2
LANES = 128
FFT_N2 = LANES
VMEM_LIMIT = 52 << 20


def _params(*sem):
    return pltpu.CompilerParams(dimension_semantics=sem, vmem_limit_bytes=VMEM_LIMIT)


def _dot(a, b):
    return jnp.dot(a, b, preferred_element_type=F32)


def _dot_nt(a, b):
    return lax.dot_general(a, b, (((1,), (1,)), ((), ())), preferred_element_type=F32)


def _split(x):
    hi = x.astype(BF16)
    lo = (x - hi.astype(F32)).astype(BF16)
    return hi, lo


def _dot3(a_hi, a_lo, b_hi, b_lo):
    return _dot(a_hi, b_hi) + _dot(a_lo, b_hi) + _dot(a_hi, b_lo)


def _np_split(m):
    m = np.asarray(m, np.float64)
    hi = m.astype(np.float32).astype(BF16)
    lo = (m - hi.astype(np.float64)).astype(np.float32).astype(BF16)
    return jnp.asarray(hi), jnp.asarray(lo)


def _ada_kernel(c_ref, w_ref, b_ref, o_ref):
    c = c_ref[...]
    s = c * jax.nn.sigmoid(c)
    o_ref[0] = _dot(s.astype(BF16), w_ref[0].astype(BF16)) + b_ref[0]


def _adaln(c_all, ada_w, ada_b):
    depth, d, n = ada_w.shape
    tn = n // 4
    return pl.pallas_call(
        _ada_kernel,
        out_shape=jax.ShapeDtypeStruct((depth, c_all.shape[0], n), F32),
        grid=(depth, n // tn),
        in_specs=[pl.BlockSpec(c_all.shape, lambda l, j: (0, 0)),
                  pl.BlockSpec((1, d, tn), lambda l, j: (l, 0, j)),
                  pl.BlockSpec((1, 1, tn), lambda l, j: (l, 0, j))],
        out_specs=pl.BlockSpec((1, c_all.shape[0], tn), lambda l, j: (l, 0, j)),
        compiler_params=_params("arbitrary", "arbitrary"),
        name="adaln",
    )(c_all, ada_w, ada_b.reshape(depth, 1, n))


def _head_norm_rope(x, gain, cos, sin, blk_hi):
    sq = x * x
    hi, lo = _split(sq)
    ssum = _dot(hi, blk_hi) + _dot(lo, blk_hi)
    xn = x * lax.rsqrt(ssum * (1.0 / HEAD_DIM) + EPS) * gain
    lane = lax.broadcasted_iota(I32, x.shape, 1)
    first_half = (lane % HEAD_DIM) < (HEAD_DIM // 2)
    rot = jnp.where(first_half, pltpu.roll(xn, LANES - HEAD_DIM // 2, 1), pltpu.roll(xn, HEAD_DIM // 2, 1))
    return xn * cos + rot * sin


def _inproj_kernel(x_ref, mod_ref, g_ref, w_ref, cos_ref, sin_ref, qg_ref, kg_ref, blk_ref,
                   hy_ref, q_ref, k_ref, v_ref, gs_ref, *, hy_w, att_w, kv_w):
    x = x_ref[...]
    mod = mod_ref[0]
    ms = jnp.mean(x * x, axis=-1, keepdims=True)
    h = (x * lax.rsqrt(ms + EPS) * g_ref[...]) * (1.0 + mod[1:2]) + mod[0:1]
    hb = h.astype(BF16)
    s0 = 3 * hy_w
    s1 = s0 + att_w
    s2 = s1 + kv_w
    s3 = s2 + kv_w
    hy_ref[...] = _dot(hb, w_ref[:, 0:s0])
    q = _dot(hb, w_ref[:, s0:s1])
    k = _dot(hb, w_ref[:, s1:s2])
    v_ref[...] = _dot(hb, w_ref[:, s2:s3]).astype(BF16)
    gs_ref[...] = jax.nn.sigmoid(_dot(hb, w_ref[:, s3:]))
    cos = cos_ref[...]
    sin = sin_ref[...]
    blk = blk_ref[...]
    k_ref[...] = _head_norm_rope(k, kg_ref[...], cos, sin, blk).astype(BF16)
    lane = lax.broadcasted_iota(I32, cos.shape, 1)
    low = lane < HEAD_DIM
    heads_per_kv = N_HEADS // N_KV_HEADS
    for ci in range(att_w // LANES):
        qr = _head_norm_rope(q[:, ci * LANES:(ci + 1) * LANES], qg_ref[...], cos, sin, blk)
        qs = pltpu.roll(qr, HEAD_DIM, 1)
        kvh = (2 * ci) // heads_per_kv
        if kvh == 0:
            h0 = jnp.where(low, qr, 0.0)
            h1 = jnp.where(low, qs, 0.0)
        else:
            h0 = jnp.where(low, 0.0, qs)
            h1 = jnp.where(low, 0.0, qr)
        q_ref[:, (2 * ci) * LANES:(2 * ci + 1) * LANES] = h0.astype(BF16)
        q_ref[:, (2 * ci + 1) * LANES:(2 * ci + 2) * LANES] = h1.astype(BF16)


def _inproj(x, mod_l, b0, L, norm_g, w_in_bf, cos_t, sin_t, qg, kg, blk, hy_w, att_w, kv_w):
    T, D = x.shape
    tm = 256
    nper = L // tm
    in_w = w_in_bf.shape[1]
    kern = functools.partial(_inproj_kernel, hy_w=hy_w, att_w=att_w, kv_w=kv_w)
    return pl.pallas_call(
        kern,
        out_shape=(jax.ShapeDtypeStruct((T, 3 * hy_w), F32),
                   jax.ShapeDtypeStruct((T, N_HEADS * LANES), BF16),
                   jax.ShapeDtypeStruct((T, kv_w), BF16),
                   jax.ShapeDtypeStruct((T, kv_w), BF16),
                   jax.ShapeDtypeStruct((T, in_w - 3 * hy_w - att_w - 2 * kv_w), F32)),
        grid=(T // tm,),
        in_specs=[pl.BlockSpec((tm, D), lambda i: (i, 0)),
                  pl.BlockSpec((1, 6, D), lambda i: (b0 + i // nper, 0, 0)),
                  pl.BlockSpec((1, D), lambda i: (0, 0)),
                  pl.BlockSpec((D, in_w), lambda i: (0, 0)),
                  pl.BlockSpec((tm, LANES), lambda i: (i % nper, 0)),
                  pl.BlockSpec((tm, LANES), lambda i: (i % nper, 0)),
                  pl.BlockSpec((1, LANES), lambda i: (0, 0)),
                  pl.BlockSpec((1, LANES), lambda i: (0, 0)),
                  pl.BlockSpec((LANES, LANES), lambda i: (0, 0))],
        out_specs=(pl.BlockSpec((tm, 3 * hy_w), lambda i: (i, 0)),
                   pl.BlockSpec((tm, N_HEADS * LANES), lambda i: (i, 0)),
                   pl.BlockSpec((tm, kv_w), lambda i: (i, 0)),
                   pl.BlockSpec((tm, kv_w), lambda i: (i, 0)),
                   pl.BlockSpec((tm, in_w - 3 * hy_w - att_w - 2 * kv_w), lambda i: (i, 0))),
        compiler_params=_params("arbitrary"),
        name="inproj",
    )(x, mod_l, norm_g, w_in_bf, cos_t, sin_t, qg, kg, blk)


def _shortconv_kernel(u_ref, p_ref, n_ref, w_ref, b_ref, z_ref, ga_ref, gb_ref, *, nper, hy_w):
    i = pl.program_id(0)
    u = u_ref[...]
    tm = u.shape[0]
    first = (i % nper) == 0
    last = (i % nper) == nper - 1
    prev_row = jnp.where(first, 0.0, p_ref[7:8, :])
    next_row = jnp.where(last, 0.0, n_ref[0:1, :])
    row = lax.broadcasted_iota(I32, u.shape, 0)
    up = jnp.where(row == 0, prev_row, pltpu.roll(u, 1, 0))
    dn = jnp.where(row == tm - 1, next_row, pltpu.roll(u, tm - 1, 0))
    y = up * w_ref[0:1, :] + u * w_ref[1:2, :] + dn * w_ref[2:3, :] + b_ref[...]
    z_ref[...] = y[:, 0:hy_w]
    ga_ref[...] = y[:, hy_w:2 * hy_w]
    gb_ref[...] = y[:, 2 * hy_w:3 * hy_w]


def _shortconv(hy, L, conv_w, conv_b, hy_w):
    T, W = hy.shape
    tm = 512
    nper = L // tm
    r8 = tm // 8
    nb8 = T // 8
    kern = functools.partial(_shortconv_kernel, nper=nper, hy_w=hy_w)
    o = jax.ShapeDtypeStruct((T, hy_w), F32)
    ospec = pl.BlockSpec((tm, hy_w), lambda i: (i, 0))
    return pl.pallas_call(
        kern,
        out_shape=(o, o, o),
        grid=(T // tm,),
        in_specs=[pl.BlockSpec((tm, W), lambda i: (i, 0)),
                  pl.BlockSpec((8, W), lambda i: (jnp.maximum(i * r8 - 1, 0), 0)),
                  pl.BlockSpec((8, W), lambda i: (jnp.minimum((i + 1) * r8, nb8 - 1), 0)),
                  pl.BlockSpec((3, W), lambda i: (0, 0)),
                  pl.BlockSpec((1, W), lambda i: (0, 0))],
        out_specs=(ospec, ospec, ospec),
        compiler_params=_params("arbitrary"),
        name="shortconv",
    )(hy, hy, hy, conv_w, conv_b.reshape(1, W))


def _filter_kernel(z_ref, w1_ref, b1_ref, w2_ref, b2_ref, w3_ref, b3_ref, fr_ref, wo_ref, dl_ref,
                   k_ref, ssq_ref, *, L):
    i = pl.program_id(0)
    z = z_ref[...]
    fr = fr_ref[...]

    def lin(a, w_ref_, b_ref_):
        ah, al = _split(a)
        wh, wl = _split(w_ref_[...])
        return _dot3(ah, al, wh, wl) + b_ref_[...]

    h = jnp.sin(fr * lin(z, w1_ref, b1_ref))
    h = jnp.sin(fr * lin(h, w2_ref, b2_ref))
    h = jnp.sin(fr * lin(h, w3_ref, b3_ref))
    hh, hl = _split(h)
    wh, wl = _split(wo_ref[0])
    k = _dot3(hh, hl, wh, wl)
    k = k * jnp.exp(-z[:, 0:1] * dl_ref[...])
    row = lax.broadcasted_iota(I32, k.shape, 0) + i * k.shape[0]
    k = jnp.where(row == L, 0.0, k)
    k_ref[...] = k

    @pl.when(i == 0)
    def _():
        ssq_ref[...] = jnp.zeros_like(ssq_ref)

    ssq_ref[...] += jnp.sum(k * k, axis=0, keepdims=True)


def _filter_rows(L, p):
    hid = p['flt_w2'].shape[0]
    hy_w = p['hy_bias'].shape[1]
    cw = HY_ORDER * hy_w
    tr = 512
    N = 2 * L
    r = jnp.arange(N, dtype=I32)
    j = jnp.minimum(jnp.where(r < L, r, N - r), L - 1)
    t = jnp.linspace(0.0, 1.0, L, dtype=F32)[j][:, None]
    bands = (FILTER_EMB - 1) // 2
    f = jnp.linspace(1e-4, bands - 1, bands, dtype=F32)[None, :]
    w = ((2.0 * math.pi / L) * jnp.arange(L, dtype=F32))[j][:, None]
    z = jnp.concatenate([t, jnp.cos(f * w), -jnp.sin(f * w)], axis=-1)
    z = jnp.pad(z, ((0, 0), (0, LANES - FILTER_EMB)))
    w1 = jnp.pad(p['flt_w1'], ((0, LANES - FILTER_EMB), (0, 0)))
    wo = p['flt_w_out'].reshape(hid, 2, cw).transpose(1, 0, 2)
    max_decay = math.log(DECAY_TARGET) / FAST_DECAY_PCT
    min_decay = math.log(DECAY_TARGET) / SLOW_DECAY_PCT
    deltas = jnp.abs(jnp.linspace(min_decay, max_decay, hy_w, dtype=F32))
    dl = jnp.tile(deltas, HY_ORDER)[None, :]
    nblk = N // tr
    half = nblk // 2
    full = lambda i: (0, 0)
    return pl.pallas_call(
        functools.partial(_filter_kernel, L=L),
        out_shape=(jax.ShapeDtypeStruct((N, cw), F32), jax.ShapeDtypeStruct((1, cw), F32)),
        grid=(nblk,),
        in_specs=[pl.BlockSpec((tr, LANES), lambda i: (i, 0)),
                  pl.BlockSpec((LANES, hid), full), pl.BlockSpec((1, hid), full),
                  pl.BlockSpec((hid, hid), full), pl.BlockSpec((1, hid), full),
                  pl.BlockSpec((hid, hid), full), pl.BlockSpec((1, hid), full),
                  pl.BlockSpec((1, hid), full),
                  pl.BlockSpec((1, hid, cw), lambda i: (i // half, 0, 0)),
                  pl.BlockSpec((1, cw), full)],
        out_specs=(pl.BlockSpec((tr, cw), lambda i: (i, 0)), pl.BlockSpec((1, cw), full)),
        compiler_params=_params("arbitrary"),
        name="hyena_filter",
    )(z, w1, p['flt_b1'][None], p['flt_w2'], p['flt_b2'][None], p['flt_w3'], p['flt_b3'][None],
      p['flt_freq'][None], wo, dl)


def _fft_consts(L, planes):
    N = 2 * L
    n1 = N // FFT_N2
    n1h = n1 // 2
    k1 = np.arange(n1)[:, None]
    th = 2.0 * np.pi * k1 * np.arange(n1)[None, :] / n1
    c, s = np.cos(th), np.sin(th)
    ch, sh = c[:, :n1h], s[:, :n1h]
    if planes == 1:
        m1 = np.concatenate([ch, -sh], axis=0)
        m3 = np.concatenate([ch.T, -sh.T], axis=1) / N
    else:
        m1 = np.block([[ch, sh], [-sh, ch]])
        m3 = np.block([[ch.T, -sh.T], [sh.T, ch.T]]) / N
    m1_full = np.concatenate([c, -s], axis=0)
    ph = 2.0 * np.pi * k1 * np.arange(FFT_N2)[None, :] / N
    twr, twi = np.cos(ph), -np.sin(ph)
    a = 2.0 * np.pi * np.arange(FFT_N2)[:, None] * np.arange(FFT_N2)[None, :] / FFT_N2
    ca, sa = np.cos(a), np.sin(a)
    rt = np.block([[ca, -sa], [sa, ca]])
    rit = np.block([[ca, sa], [-sa, ca]])
    return dict(n1=n1, n1h=n1h, m1=_np_split(m1), m3=_np_split(m3), m1_full=_np_split(m1_full),
                twr=jnp.asarray(twr, F32), twi=jnp.asarray(twi, F32), rt=_np_split(rt), rit=_np_split(rit))


def _fwd_lane_dft(ar, ai, twr, twi, rt_hi, rt_lo, ct):
    pieces = []
    for c in range(ct):
        r = ar[:, c * LANES:(c + 1) * LANES]
        i = ai[:, c * LANES:(c + 1) * LANES]
        pieces.append(jnp.concatenate([r * twr - i * twi, r * twi + i * twr], axis=1))
    s = jnp.concatenate(pieces, axis=0)
    sh, sl = _split(s)
    return _dot3(sh, sl, rt_hi, rt_lo)


def _spectrum_kernel(k_ref, m1h_ref, m1l_ref, twr_ref, twi_ref, rth_ref, rtl_ref, sc_ref, o_ref, *, ct):
    n1 = twr_ref.shape[0]
    uh, ul = _split(k_ref[...])
    a = _dot3(m1h_ref[...], m1l_ref[...], uh, ul)
    x = _fwd_lane_dft(a[:n1], a[n1:], twr_ref[...], twi_ref[...], rth_ref[...], rtl_ref[...], ct)
    for c in range(ct):
        o_ref[0, :, c * 2 * LANES:(c + 1) * 2 * LANES] = x[c * n1:(c + 1) * n1] * sc_ref[0, 0:1, c:c + 1]


def _filter_spectrum(k_cm, scale, fc, hy_w):
    n1 = fc['n1']
    ct = 8
    ntile = hy_w // ct
    sc = scale.reshape(HY_ORDER * ntile, 1, ct)
    sc = jnp.broadcast_to(sc, (HY_ORDER * ntile, 8, ct))
    full = lambda o, j: (0, 0)
    return pl.pallas_call(
        functools.partial(_spectrum_kernel, ct=ct),
        out_shape=jax.ShapeDtypeStruct((HY_ORDER, n1, hy_w * 2 * LANES), F32),
        grid=(HY_ORDER, ntile),
        in_specs=[pl.BlockSpec((n1, ct * LANES), lambda o, j: (0, o * ntile + j)),
                  pl.BlockSpec((2 * n1, n1), full), pl.BlockSpec((2 * n1, n1), full),
                  pl.BlockSpec((n1, LANES), full), pl.BlockSpec((n1, LANES), full),
                  pl.BlockSpec((2 * LANES, 2 * LANES), full), pl.BlockSpec((2 * LANES, 2 * LANES), full),
                  pl.BlockSpec((1, 8, ct), lambda o, j: (o * ntile + j, 0, 0))],
        out_specs=pl.BlockSpec((1, n1, ct * 2 * LANES), lambda o, j: (o, 0, j)),
        compiler_params=_params("arbitrary", "arbitrary"),
        name="filter_spectrum",
    )(k_cm, *fc['m1_full'], fc['twr'], fc['twi'], *fc['rt'], sc)


def _longconv_kernel(z_ref, g_ref, kf_ref, b_ref, m1h_ref, m1l_ref, twr_ref, twi_ref, rth_ref, rtl_ref,
                     rih_ref, ril_ref, m3h_ref, m3l_ref, o_ref, *, ct):
    n1 = twr_ref.shape[0]
    planes, n1h, w = z_ref.shape[1], z_ref.shape[2], z_ref.shape[3]
    z = z_ref[0].reshape(planes * n1h, w)
    zh, zl = _split(z)
    a = _dot3(m1h_ref[...], m1l_ref[...], zh, zl)
    twr = twr_ref[...]
    twi = twi_ref[...]
    x = _fwd_lane_dft(a[:n1], a[n1:], twr, twi, rth_ref[...], rtl_ref[...], ct)
    ys = []
    for c in range(ct):
        xc = x[c * n1:(c + 1) * n1]
        kc = kf_ref[0, :, c * 2 * LANES:(c + 1) * 2 * LANES]
        xr, xi = xc[:, :LANES], xc[:, LANES:]
        kr, ki = kc[:, :LANES], kc[:, LANES:]
        ys.append(jnp.concatenate([xr * kr - xi * ki, xr * ki + xi * kr], axis=1))
    y = jnp.concatenate(ys, axis=0)
    yh, yl = _split(y)
    bp = _dot3(yh, yl, rih_ref[...], ril_ref[...])
    brs, bis = [], []
    for c in range(ct):
        bc = bp[c * n1:(c + 1) * n1]
        r, i = bc[:, :LANES], bc[:, LANES:]
        brs.append(r * twr + i * twi)
        bis.append(i * twr - r * twi)
    bst = jnp.concatenate([jnp.concatenate(brs, axis=1), jnp.concatenate(bis, axis=1)], axis=0)
    bh, bl = _split(bst)
    yt = _dot3(m3h_ref[...], m3l_ref[...], bh, bl)
    out = g_ref[0].reshape(planes * n1h, w) * (yt + b_ref[...] * z)
    o_ref[0] = out.reshape(planes, n1h, w)


def _longconv(z_cm, g_cm, kf, order, bias_cm, fc):
    npair, planes, n1h, wtot = z_cm.shape
    n1 = fc['n1']
    ct = 8
    w = ct * LANES
    full = lambda p, j: (0, 0)
    m1r, m1c = 2 * n1, planes * n1h
    blk = pl.BlockSpec((1, planes, n1h, w), lambda p, j: (p, 0, 0, j))
    return pl.pallas_call(
        functools.partial(_longconv_kernel, ct=ct),
        out_shape=jax.ShapeDtypeStruct(z_cm.shape, F32),
        grid=(npair, wtot // w),
        in_specs=[blk, blk,
                  pl.BlockSpec((1, n1, 2 * w), lambda p, j: (order, 0, j)),
                  pl.BlockSpec((1, w), lambda p, j: (0, j)),
                  pl.BlockSpec((m1r, m1c), full), pl.BlockSpec((m1r, m1c), full),
                  pl.BlockSpec((n1, LANES), full), pl.BlockSpec((n1, LANES), full),
                  pl.BlockSpec((2 * LANES, 2 * LANES), full), pl.BlockSpec((2 * LANES, 2 * LANES), full),
                  pl.BlockSpec((2 * LANES, 2 * LANES), full), pl.BlockSpec((2 * LANES, 2 * LANES), full),
                  pl.BlockSpec((m1c, m1r), full), pl.BlockSpec((m1c, m1r), full)],
        out_specs=blk,
        compiler_params=_params("arbitrary", "arbitrary"),
        name="hyena_longconv",
    )(z_cm, g_cm, kf, bias_cm[order:order + 1], *fc['m1'], fc['twr'], fc['twi'], *fc['rt'], *fc['rit'], *fc['m3'])


def _to_cm(x, B, L, planes):
    C = x.shape[1]
    n1h = L // FFT_N2
    y = x.reshape(B, n1h, FFT_N2, C).swapaxes(2, 3)
    return y.reshape(B // planes, planes, n1h, C * FFT_N2)


def _from_cm(y, B, L):
    n1h = L // FFT_N2
    C = y.shape[-1] // FFT_N2
    return y.reshape(B, n1h, C, FFT_N2).swapaxes(2, 3).reshape(B * L, C)


def _attn_kernel(sink_ref, q_ref, kp_ref, km_ref, kn_ref, vp_ref, vm_ref, vn_ref, o_ref, *, L, tq):
    i = pl.program_id(1)
    t0 = i * tq
    tk = tq + 2 * WINDOW
    kk = jnp.concatenate([kp_ref[...], km_ref[...], kn_ref[...]], axis=0)
    vv = jnp.concatenate([vp_ref[...], vm_ref[...], vn_ref[...]], axis=0)
    qpos = t0 + lax.broadcasted_iota(I32, (tq, tk), 0)
    kpos = t0 - WINDOW + lax.broadcasted_iota(I32, (tq, tk), 1)
    valid = (jnp.abs(kpos - qpos) <= WINDOW) & (kpos >= 0) & (kpos < L)
    scale = HEAD_DIM ** -0.5
    for h in range(N_HEADS):
        s = _dot_nt(q_ref[:, h * LANES:(h + 1) * LANES], kk) * scale
        s = jnp.where(valid, s, -1e30)
        sink = sink_ref[h]
        m = jnp.maximum(jnp.max(s, axis=-1, keepdims=True), sink)
        p = jnp.exp(s - m)
        denom = jnp.sum(p, axis=-1, keepdims=True) + jnp.exp(sink - m)
        o_ref[:, h * LANES:(h + 1) * LANES] = _dot((p / denom).astype(BF16), vv).astype(BF16)


def _attention(q, k, v, sink, B, L):
    T = q.shape[0]
    tq = 256
    nq = L // tq
    r = tq // WINDOW
    nb = L // WINDOW
    kv_w = k.shape[1]
    kp = pl.BlockSpec((WINDOW, kv_w), lambda b, i: (b * nb + jnp.maximum(i * r - 1, 0), 0))
    km = pl.BlockSpec((tq, kv_w), lambda b, i: (b * nq + i, 0))
    kn = pl.BlockSpec((WINDOW, kv_w), lambda b, i: (b * nb + jnp.minimum((i + 1) * r, nb - 1), 0))
    qs = pl.BlockSpec((tq, N_HEADS * LANES), lambda b, i: (b * nq + i, 0))
    return pl.pallas_call(
        functools.partial(_attn_kernel, L=L, tq=tq),
        out_shape=jax.ShapeDtypeStruct((T, N_HEADS * LANES), BF16),
        grid=(B, nq),
        in_specs=[pl.BlockSpec(memory_space=pltpu.SMEM), qs, kp, km, kn, kp, km, kn],
        out_specs=qs,
        compiler_params=_params("arbitrary", "arbitrary"),
        name="band_attention",
    )(sink, q, k, k, k, v, v, v)


def _merge_kernel(x_ref, z_ref, a_ref, gs_ref, mod_ref, why_ref, wat_ref, wo_ref, g2_ref, rwh_ref, rwl_ref,
                  xo_ref, h2_ref, aff_ref):
    d = x_ref.shape[1]
    mod = mod_ref[0]
    y_hy = _dot(z_ref[...].astype(BF16), why_ref[...])
    y_at = _dot(a_ref[...], wat_ref[...])
    gs = gs_ref[...]
    merged = gs[:, :d] * y_hy + gs[:, d:] * y_at
    x = x_ref[...] + mod[2:3] * _dot(merged.astype(BF16), wo_ref[...])
    xo_ref[...] = x
    ms = jnp.mean(x * x, axis=-1, keepdims=True)
    h = (x * lax.rsqrt(ms + EPS) * g2_ref[...]) * (1.0 + mod[4:5]) + mod[3:4]
    h2_ref[...] = h
    hh, hl = _split(h)
    rh, rl = rwh_ref[...], rwl_ref[...]
    logits = _dot_nt(rh, hh) + _dot_nt(rl, hh) + _dot_nt(rh, hl)
    m = jnp.max(logits, axis=0, keepdims=True)
    e = jnp.exp(logits - m)
    aff_ref[...] = e / jnp.sum(e, axis=0, keepdims=True)


def _merge(x, z, att, gs, mod_l, b0, L, w_hy, w_at_x, w_o, norm2_g, rw_hi, rw_lo):
    T, D = x.shape
    tm = 256
    nper = L // tm
    E = rw_hi.shape[0]
    full = lambda i: (0, 0)
    row = lambda w: pl.BlockSpec((tm, w), lambda i: (i, 0))
    return pl.pallas_call(
        _merge_kernel,
        out_shape=(jax.ShapeDtypeStruct((T, D), F32), jax.ShapeDtypeStruct((T, D), F32),
                   jax.ShapeDtypeStruct((E, T), F32)),
        grid=(T // tm,),
        in_specs=[row(D), row(z.shape[1]), row(att.shape[1]), row(gs.shape[1]),
                  pl.BlockSpec((1, 6, D), lambda i: (b0 + i // nper, 0, 0)),
                  pl.BlockSpec(w_hy.shape, full), pl.BlockSpec(w_at_x.shape, full), pl.BlockSpec(w_o.shape, full),
                  pl.BlockSpec((1, D), full), pl.BlockSpec((E, D), full), pl.BlockSpec((E, D), full)],
        out_specs=(row(D), row(D), pl.BlockSpec((E, tm), lambda i: (0, i))),
        compiler_params=_params("arbitrary"),
        name="merge_router",
    )(x, z, att, gs, mod_l, w_hy, w_at_x, w_o, norm2_g, rw_hi, rw_lo)


def _cumsum_rows(mb, u, ones, ls):
    rc = _dot(mb, u)
    totb = _dot(mb, ones)
    offs = _dot(ls, totb.astype(BF16))
    return rc, totb, offs


def _topk_kernel(aff_ref, u_ref, ones_ref, ls_ref, idx_ref, gate_ref, src_ref, cnt_ref, thr_ref, *, cap):
    E, R, _ = aff_ref.shape
    bits = pltpu.bitcast(aff_ref[...], I32)

    def search(i, cur):
        cand = cur | jnp.left_shift(jnp.int32(1), 30 - i)
        ge = jnp.where(bits >= cand, 1.0, 0.0)
        c = jnp.sum(jnp.sum(ge, axis=2, keepdims=True), axis=1, keepdims=True)
        return jnp.where(c >= cap, cand, cur)

    thr = lax.fori_loop(0, 31, search, jnp.zeros((E, 1, 1), I32))
    thr_ref[...] = jnp.broadcast_to(thr, thr_ref.shape)
    src_ref[...] = jnp.zeros_like(src_ref)
    u = u_ref[...]
    ones = ones_ref[...]
    ls = ls_ref[...]
    s_row = lax.broadcasted_iota(I32, (1, cap), 1).astype(F32)
    r_col = lax.broadcasted_iota(I32, (R, 1), 0).astype(F32)
    c_col = lax.broadcasted_iota(I32, (LANES, 1), 0).astype(F32)

    def per_expert(e, cnt_run):
        a = aff_ref[e]
        b = pltpu.bitcast(a, I32)
        t = thr_ref[e][0:1, :]
        gt = b > t
        eq = b == t
        ngt = jnp.sum(jnp.sum(jnp.where(gt, 1.0, 0.0), axis=1, keepdims=True), axis=0, keepdims=True)
        need = cap - ngt
        erc, _, eoffs = _cumsum_rows(jnp.where(eq, 1.0, 0.0).astype(BF16), u, ones, ls)
        sel = gt | (eq & ((erc + eoffs) <= need))
        rc, totb, offs = _cumsum_rows(jnp.where(sel, 1.0, 0.0).astype(BF16), u, ones, ls)
        pos = (rc + offs - 1.0).astype(I32)
        val = e * cap + pos
        for k in range(E):
            src_ref[k] = jnp.where(sel & (cnt_run == k), val, src_ref[k])
        offs_col = offs[:, 0:1]
        end_col = offs_col + totb[:, 0:1]
        oht = (offs_col <= s_row) & (s_row < end_col)
        row_s = jnp.sum(jnp.where(oht, r_col, 0.0), axis=0, keepdims=True)
        offs_s = jnp.sum(jnp.where(oht, offs_col, 0.0), axis=0, keepdims=True)
        ohb = jnp.where(oht, 1.0, 0.0).astype(BF16)
        rcs = _dot(rc.T.astype(BF16), ohb)
        col_s = jnp.sum(jnp.where(rcs <= (s_row - offs_s), 1.0, 0.0), axis=0, keepdims=True)
        idx_ref[pl.ds(e, 1), :] = (row_s * LANES + col_s).astype(I32)
        at = a.T
        p0 = at.astype(BF16)
        r1 = at - p0.astype(F32)
        p1 = r1.astype(BF16)
        p2 = (r1 - p1.astype(F32)).astype(BF16)
        affs = _dot(p0, ohb) + _dot(p1, ohb) + _dot(p2, ohb)
        gate_ref[pl.ds(e, 1), :] = jnp.sum(jnp.where(c_col == col_s, affs, 0.0), axis=0, keepdims=True)
        return cnt_run + jnp.where(sel, 1, 0)

    cnt_ref[...] = lax.fori_loop(0, E, per_expert, jnp.zeros((R, LANES), I32))


def _topk(aff_t, cap):
    E, T = aff_t.shape
    R = T // LANES
    u = jnp.asarray(np.triu(np.ones((LANES, LANES))), BF16)
    ones = jnp.ones((LANES, LANES), BF16)
    ls = jnp.asarray(np.tril(np.ones((R, R)), -1), BF16)
    z2 = lambda: (0, 0)
    z3 = lambda: (0, 0, 0)
    idx, gate, src, cnt = pl.pallas_call(
        functools.partial(_topk_kernel, cap=cap),
        out_shape=(jax.ShapeDtypeStruct((E, cap), I32), jax.ShapeDtypeStruct((E, cap), F32),
                   jax.ShapeDtypeStruct((E, R, LANES), I32), jax.ShapeDtypeStruct((R, LANES), I32)),
        in_specs=[pl.BlockSpec((E, R, LANES), z3), pl.BlockSpec((LANES, LANES), z2),
                  pl.BlockSpec((LANES, LANES), z2), pl.BlockSpec((R, R), z2)],
        out_specs=(pl.BlockSpec((E, cap), z2), pl.BlockSpec((E, cap), z2),
                   pl.BlockSpec((E, R, LANES), z3), pl.BlockSpec((R, LANES), z2)),
        scratch_shapes=[pltpu.VMEM((E, 8, LANES), I32)],
        compiler_params=pltpu.CompilerParams(vmem_limit_bytes=VMEM_LIMIT),
        name="expert_topk",
    )(aff_t.reshape(E, R, LANES), u, ones, ls)
    return idx, gate, src.reshape(E, T), cnt.reshape(1, T)


def _gather_kernel(idx_ref, h_ref, o_ref, buf, sem, *, ts):
    base = pl.program_id(0) * ts

    def issue(s, c):
        pltpu.make_async_copy(h_ref.at[pl.ds(idx_ref[base + s], 1), :], buf.at[pl.ds(s, 1), :], sem).start()
        return c

    lax.fori_loop(0, ts, issue, 0)

    def wait(s, c):
        pltpu.make_async_copy(h_ref.at[pl.ds(0, 1), :], buf.at[pl.ds(s, 1), :], sem).wait()
        return c

    lax.fori_loop(0, ts, wait, 0)
    o_ref[...] = buf[...].astype(o_ref.dtype)


def _gather_rows(h, idx_flat):
    n = idx_flat.shape[0]
    d = h.shape[1]
    ts = min(512, n)
    return pl.pallas_call(
        functools.partial(_gather_kernel, ts=ts),
        out_shape=jax.ShapeDtypeStruct((n, d), BF16),
        grid_spec=pltpu.PrefetchScalarGridSpec(
            num_scalar_prefetch=1,
            grid=(n // ts,),
            in_specs=[pl.BlockSpec(memory_space=pl.ANY)],
            out_specs=pl.BlockSpec((ts, d), lambda i, idx: (i, 0)),
            scratch_shapes=[pltpu.VMEM((ts, d), F32), pltpu.SemaphoreType.DMA]),
        compiler_params=_params("arbitrary"),
        name="expert_gather",
    )(idx_flat, h)


def _expert_kernel(x_ref, wg_ref, wu_ref, wd_ref, gate_ref, o_ref):
    f = pl.program_id(1)
    x = x_ref[...]
    g = _dot(x, wg_ref[0].astype(BF16))
    u = _dot(x, wu_ref[0].astype(BF16))
    hid = (g * jax.nn.sigmoid(g) * u).astype(BF16)
    y = _dot(hid, wd_ref[0].astype(BF16))

    @pl.when(f == 0)
    def _():
        o_ref[...] = y

    @pl.when(f > 0)
    def _():
        o_ref[...] += y

    @pl.when(f == pl.num_programs(1) - 1)
    def _():
        o_ref[...] = gate_ref[0] * o_ref[...]


def _experts(xe, w_gate, w_up, w_down, gate_col, cap):
    E, D, F = w_gate.shape
    ft = 256
    return pl.pallas_call(
        _expert_kernel,
        out_shape=jax.ShapeDtypeStruct((E * cap, D), F32),
        grid=(E, F // ft),
        in_specs=[pl.BlockSpec((cap, D), lambda e, f: (e, 0)),
                  pl.BlockSpec((1, D, ft), lambda e, f: (e, 0, f)),
                  pl.BlockSpec((1, D, ft), lambda e, f: (e, 0, f)),
                  pl.BlockSpec((1, ft, D), lambda e, f: (e, f, 0)),
                  pl.BlockSpec((1, cap, 1), lambda e, f: (e, 0, 0))],
        out_specs=pl.BlockSpec((cap, D), lambda e, f: (e, 0)),
        compiler_params=_params("arbitrary", "arbitrary"),
        name="expert_ffn",
    )(xe, w_gate, w_up, w_down, gate_col)


def _combine_kernel(x_ref, mod_ref, tab_ref, eye_ref, ye_ref, o_ref, tab_s, buf, acc, sem, csem):
    E = buf.shape[0]
    tt = x_ref.shape[0]
    cp = pltpu.make_async_copy(tab_ref, tab_s, csem)
    cp.start()
    cp.wait()

    @pl.when(pl.program_id(0) == 0)
    def _():
        buf[...] = jnp.zeros_like(buf)

    def issue_t(t, kmax):
        c = tab_s[E, t]

        def issue_k(k, z):
            pltpu.make_async_copy(ye_ref.at[pl.ds(tab_s[k, t], 1), :], buf.at[k, pl.ds(t, 1), :], sem).start()
            return z

        lax.fori_loop(0, c, issue_k, 0)
        return jnp.maximum(kmax, c)

    kmax = lax.fori_loop(0, tt, issue_t, 0)

    def wait_t(t, z):
        def wait_k(k, z2):
            pltpu.make_async_copy(ye_ref.at[pl.ds(0, 1), :], buf.at[k, pl.ds(t, 1), :], sem).wait()
            return z2

        lax.fori_loop(0, tab_s[E, t], wait_k, 0)
        return z

    lax.fori_loop(0, tt, wait_t, 0)
    cnt_col = _dot_nt(eye_ref[...], tab_ref[...].astype(F32).astype(BF16))[:, E:E + 1]
    acc[...] = jnp.zeros_like(acc)

    def add_k(k, z):
        acc[...] += jnp.where(cnt_col > k.astype(F32), buf[k], 0.0)
        return z

    lax.fori_loop(0, kmax, add_k, 0)
    o_ref[...] = x_ref[...] + mod_ref[0][5:6] * acc[...]


def _combine(x, mod_l, b0, L, tab, ye):
    T, D = x.shape
    tt = LANES
    nper = L // tt
    E = tab.shape[0] - 8
    eye = jnp.eye(tt, dtype=BF16)
    return pl.pallas_call(
        _combine_kernel,
        out_shape=jax.ShapeDtypeStruct((T, D), F32),
        grid=(T // tt,),
        in_specs=[pl.BlockSpec((tt, D), lambda i: (i, 0)),
                  pl.BlockSpec((1, 6, D), lambda i: (b0 + i // nper, 0, 0)),
                  pl.BlockSpec((E + 8, tt), lambda i: (0, i)),
                  pl.BlockSpec((tt, tt), lambda i: (0, 0)),
                  pl.BlockSpec(memory_space=pl.ANY)],
        out_specs=pl.BlockSpec((tt, D), lambda i: (i, 0)),
        scratch_shapes=[pltpu.SMEM((E + 8, tt), I32), pltpu.VMEM((E, tt, D), F32), pltpu.VMEM((tt, D), F32),
                        pltpu.SemaphoreType.DMA, pltpu.SemaphoreType.DMA],
        compiler_params=_params("arbitrary"),
        name="moe_combine",
    )(x, mod_l, tab, eye, ye)


def _layer(x, B, L, b0, mod_l, p, kf, fc, tabs):
    hy_w = p['hy_bias'].shape[1]
    att_w = N_HEADS * HEAD_DIM
    kv_w = N_KV_HEADS * HEAD_DIM
    planes = 2 if B % 2 == 0 else 1
    hy, q, k, v, gs = _inproj(x, mod_l, b0, L, p['norm1_g'][None], p['w_in_bf'], tabs['cos'], tabs['sin'],
                              tabs['qg'], tabs['kg'], tabs['blk'], hy_w, att_w, kv_w)
    z, ga, gb = _shortconv(hy, L, p['hy_conv_w'], p['hy_conv_b'], hy_w)
    z = _to_cm(z, B, L, planes)
    gates = (_to_cm(ga, B, L, planes), _to_cm(gb, B, L, planes))
    for o in range(HY_ORDER):
        z = _longconv(z, gates[o], kf, o, p['bias_cm'], fc)
    z = _from_cm(z, B, L)
    att = _attention(q, k, v, p['sink_logit'], B, L)
    x, h2, aff_t = _merge(x, z, att, gs, mod_l, b0, L, p['w_hy_bf'], p['w_at_x'], p['w_o_bf'], p['norm2_g'][None],
                          p['rw_hi'], p['rw_lo'])
    T = x.shape[0]
    cap = CAPACITY_FACTOR * T // N_EXPERTS
    idx, gate, src, cnt = _topk(aff_t, cap)
    xe = _gather_rows(h2, idx.reshape(-1))
    ye = _experts(xe, p['exp_w_gate'], p['exp_w_up'], p['exp_w_down'], gate.reshape(N_EXPERTS, cap, 1), cap)
    tab = jnp.concatenate([src, cnt, jnp.zeros((7, T), I32)], axis=0)
    return _combine(x, mod_l, b0, L, tab, ye)


def _rope_tables(L):
    inv = 1.0 / (ROPE_THETA ** (jnp.arange(0, HEAD_DIM, 2, dtype=F32) / HEAD_DIM))
    ang = jnp.arange(L, dtype=F32)[:, None] * inv[None, :]
    cos, sin = jnp.cos(ang), jnp.sin(ang)
    cos_t = jnp.tile(cos, (1, LANES // (HEAD_DIM // 2)))
    sin_t = jnp.tile(jnp.concatenate([-sin, sin], axis=1), (1, LANES // HEAD_DIM))
    return cos_t, sin_t


def _filter_spectrum_for(L, p, fc):
    hy_w = p['hy_bias'].shape[1]
    k_rows, ssq = _filter_rows(L, p)
    n1 = fc['n1']
    k_cm = k_rows.reshape(n1, FFT_N2, HY_ORDER * hy_w).swapaxes(1, 2).reshape(n1, HY_ORDER * hy_w * FFT_N2)
    scale = lax.rsqrt(ssq[0] + EPS)
    return _filter_spectrum(k_cm, scale, fc, hy_w)


def kernel(x_prompt, x_sample, c_prompt, c_sample, ada_w, ada_b, norm1_g, norm2_g, w_in, hy_conv_w, hy_conv_b,
           flt_w1, flt_b1, flt_w2, flt_b2, flt_w3, flt_b3, flt_freq, flt_w_out, hy_bias, q_norm_g, k_norm_g,
           sink_logit, w_hy_proj, w_at_proj, w_o, router_w, exp_w_gate, exp_w_up, exp_w_down):
    depth = ada_w.shape[0]
    Bp, Lp, D = x_prompt.shape
    Bs, Ls, _ = x_sample.shape
    nb = Bp + Bs
    c_all = jnp.concatenate([c_prompt, c_sample, jnp.zeros(((-nb) % 8, D), F32)], axis=0)
    mod = _adaln(c_all, ada_w, ada_b).reshape(depth, c_all.shape[0], 6, D)
    hy_w = hy_bias.shape[2]
    att_w = N_HEADS * HEAD_DIM
    groups = [dict(B=Bp, L=Lp, b0=0, x=x_prompt.reshape(Bp * Lp, D)),
              dict(B=Bs, L=Ls, b0=Bp, x=x_sample.reshape(Bs * Ls, D))]
    blk = jnp.asarray(np.kron(np.eye(LANES // HEAD_DIM), np.ones((HEAD_DIM, HEAD_DIM))), BF16)
    for g in groups:
        g['fc'] = _fft_consts(g['L'], 2 if g['B'] % 2 == 0 else 1)
        cos_t, sin_t = _rope_tables(g['L'])
        g['tabs'] = dict(cos=cos_t, sin=sin_t, blk=blk)
    heads_per_kv = N_HEADS // N_KV_HEADS
    for l in range(depth):
        w_at_x = jnp.zeros((N_HEADS, LANES, D), BF16)
        w_at_h = w_at_proj[l].astype(BF16).reshape(N_HEADS, HEAD_DIM, D)
        for h in range(N_HEADS):
            off = (h // heads_per_kv) * HEAD_DIM
            w_at_x = w_at_x.at[h, off:off + HEAD_DIM].set(w_at_h[h])
        rw_hi, rw_lo = _split(router_w[l].T)
        p = dict(norm1_g=norm1_g[l], norm2_g=norm2_g[l], w_in_bf=w_in[l].astype(BF16),
                 hy_conv_w=hy_conv_w[l], hy_conv_b=hy_conv_b[l],
                 flt_w1=flt_w1[l], flt_b1=flt_b1[l], flt_w2=flt_w2[l], flt_b2=flt_b2[l], flt_w3=flt_w3[l],
                 flt_b3=flt_b3[l], flt_freq=flt_freq[l], flt_w_out=flt_w_out[l], hy_bias=hy_bias[l],
                 bias_cm=jnp.repeat(hy_bias[l], FFT_N2, axis=1),
                 sink_logit=sink_logit[l], w_hy_bf=w_hy_proj[l].astype(BF16),
                 w_at_x=w_at_x.reshape(N_HEADS * LANES, D), w_o_bf=w_o[l].astype(BF16),
                 rw_hi=rw_hi, rw_lo=rw_lo,
                 exp_w_gate=exp_w_gate[l], exp_w_up=exp_w_up[l], exp_w_down=exp_w_down[l])
        qg = jnp.tile(q_norm_g[l], LANES // HEAD_DIM)[None]
        kg = jnp.tile(k_norm_g[l], LANES // HEAD_DIM)[None]
        for g in groups:
            kf = _filter_spectrum_for(g['L'], p, g['fc'])
            tabs = dict(g['tabs'], qg=qg, kg=kg)
            g['x'] = _layer(g['x'], g['B'], g['L'], g['b0'], mod[l], p, kf, g['fc'], tabs)
    return (groups[0]['x'].reshape(Bp, Lp, D), groups[1]['x'].reshape(Bs, Ls, D))
```

```python
import functools
import math

import numpy as np
import jax
import jax.numpy as jnp
from jax import lax
from jax.experimental import pallas as pl
from jax.experimental.pallas import tpu as pltpu

F32 = jnp.float32
BF16 = jnp.bfloat16
I32 = jnp.int32

EPS = 1e-6
HEAD_DIM = 64
N_HEADS = 8
N_KV_HEADS = 2
WINDOW = 128
ROPE_THETA = 10000.0
HY_ORDER = 2
FILTER_EMB = 33
DECAY_TARGET = 0.01
FAST_DECAY_PCT = 0.3
SLOW_DECAY_PCT = 1.5
N_EXPERTS = 16
CAPACITY_FACTOR = 2
LANES = 128
FFT_N2 = LANES
VMEM_LIMIT = 52 << 20


def _params(*sem):
    return pltpu.CompilerParams(dimension_semantics=sem, vmem_limit_bytes=VMEM_LIMIT)


def _dot(a, b):
    return jnp.dot(a, b, preferred_element_type=F32)


def _dot_nt(a, b):
    return lax.dot_general(a, b, (((1,), (1,)), ((), ())), preferred_element_type=F32)


def _split(x):
    hi = x.astype(BF16)
    lo = (x - hi.astype(F32)).astype(BF16)
    return hi, lo


def _dot3(a_hi, a_lo, b_hi, b_lo):
    return _dot(a_hi, b_hi) + _dot(a_lo, b_hi) + _dot(a_hi, b_lo)


def _ada_kernel(c_ref, w_ref, b_ref, o_ref):
    c = c_ref[...]
    s = c * jax.nn.sigmoid(c)
    o_ref[0] = _dot(s.astype(BF16), w_ref[0].astype(BF16)) + b_ref[0]


def _adaln(c_all, ada_w, ada_b):
    depth, d, n = ada_w.shape
    tn = n // 4
    return pl.pallas_call(
        _ada_kernel,
        out_shape=jax.ShapeDtypeStruct((depth, c_all.shape[0], n), F32),
        grid=(depth, n // tn),
        in_specs=[pl.BlockSpec(c_all.shape, lambda l, j: (0, 0)),
                  pl.BlockSpec((1, d, tn), lambda l, j: (l, 0, j)),
                  pl.BlockSpec((1, 1, tn), lambda l, j: (l, 0, j))],
        out_specs=pl.BlockSpec((1, c_all.shape[0], tn), lambda l, j: (l, 0, j)),
        compiler_params=_params("arbitrary", "arbitrary"),
        name="adaln",
    )(c_all, ada_w, ada_b.reshape(depth, 1, n))


def _head_norm_rope(x, gain, cos, sin, blk_hi):
    sq = x * x
    hi, lo = _split(sq)
    ssum = _dot(hi, blk_hi) + _dot(lo, blk_hi)
    xn = x * lax.rsqrt(ssum * (1.0 / HEAD_DIM) + EPS) * gain
    lane = lax.broadcasted_iota(I32, x.shape, 1)
    first_half = (lane % HEAD_DIM) < (HEAD_DIM // 2)
    rot = jnp.where(first_half, pltpu.roll(xn, LANES - HEAD_DIM // 2, 1), pltpu.roll(xn, HEAD_DIM // 2, 1))
    return xn * cos + rot * sin


def _inproj_kernel(x_ref, mod_ref, g_ref, w_ref, cos_ref, sin_ref, qg_ref, kg_ref, blk_ref,
                   hy_ref, q_ref, k_ref, v_ref, gs_ref, *, hy_w, att_w, kv_w):
    x = x_ref[...]
    mod = mod_ref[0]
    ms = jnp.mean(x * x, axis=-1, keepdims=True)
    h = (x * lax.rsqrt(ms + EPS) * g_ref[...]) * (1.0 + mod[1:2]) + mod[0:1]
    hb = h.astype(BF16)
    s0 = 3 * hy_w
    s1 = s0 + att_w
    s2 = s1 + kv_w
    s3 = s2 + kv_w
    hy_ref[...] = _dot(hb, w_ref[:, 0:s0])
    q = _dot(hb, w_ref[:, s0:s1])
    k = _dot(hb, w_ref[:, s1:s2])
    v_ref[...] = _dot(hb, w_ref[:, s2:s3]).astype(BF16)
    gs_ref[...] = jax.nn.sigmoid(_dot(hb, w_ref[:, s3:]))
    cos = cos_ref[...]
    sin = sin_ref[...]
    blk = blk_ref[...]
    k_ref[...] = _head_norm_rope(k, kg_ref[...], cos, sin, blk).astype(BF16)
    lane = lax.broadcasted_iota(I32, cos.shape, 1)
    low = lane < HEAD_DIM
    heads_per_kv = N_HEADS // N_KV_HEADS
    for ci in range(att_w // LANES):
        qr = _head_norm_rope(q[:, ci * LANES:(ci + 1) * LANES], qg_ref[...], cos, sin, blk) * (HEAD_DIM ** -0.5)
        qs = pltpu.roll(qr, HEAD_DIM, 1)
        kvh = (2 * ci) // heads_per_kv
        if kvh == 0:
            h0 = jnp.where(low, qr, 0.0)
            h1 = jnp.where(low, qs, 0.0)
        else:
            h0 = jnp.where(low, 0.0, qs)
            h1 = jnp.where(low, 0.0, qr)
        q_ref[:, (2 * ci) * LANES:(2 * ci + 1) * LANES] = h0.astype(BF16)
        q_ref[:, (2 * ci + 1) * LANES:(2 * ci + 2) * LANES] = h1.astype(BF16)


def _inproj(x, mod_l, b0, L, norm_g, w_in_bf, cos_t, sin_t, qg, kg, blk, hy_w, att_w, kv_w):
    T, D = x.shape
    tm = 256
    nper = L // tm
    in_w = w_in_bf.shape[1]
    kern = functools.partial(_inproj_kernel, hy_w=hy_w, att_w=att_w, kv_w=kv_w)
    return pl.pallas_call(
        kern,
        out_shape=(jax.ShapeDtypeStruct((T, 3 * hy_w), F32),
                   jax.ShapeDtypeStruct((T, N_HEADS * LANES), BF16),
                   jax.ShapeDtypeStruct((T, kv_w), BF16),
                   jax.ShapeDtypeStruct((T, kv_w), BF16),
                   jax.ShapeDtypeStruct((T, in_w - 3 * hy_w - att_w - 2 * kv_w), F32)),
        grid=(T // tm,),
        in_specs=[pl.BlockSpec((tm, D), lambda i: (i, 0)),
                  pl.BlockSpec((1, 6, D), lambda i: (b0 + i // nper, 0, 0)),
                  pl.BlockSpec((1, D), lambda i: (0, 0)),
                  pl.BlockSpec((D, in_w), lambda i: (0, 0)),
                  pl.BlockSpec((tm, LANES), lambda i: (i % nper, 0)),
                  pl.BlockSpec((tm, LANES), lambda i: (i % nper, 0)),
                  pl.BlockSpec((1, LANES), lambda i: (0, 0)),
                  pl.BlockSpec((1, LANES), lambda i: (0, 0)),
                  pl.BlockSpec((LANES, LANES), lambda i: (0, 0))],
        out_specs=(pl.BlockSpec((tm, 3 * hy_w), lambda i: (i, 0)),
                   pl.BlockSpec((tm, N_HEADS * LANES), lambda i: (i, 0)),
                   pl.BlockSpec((tm, kv_w), lambda i: (i, 0)),
                   pl.BlockSpec((tm, kv_w), lambda i: (i, 0)),
                   pl.BlockSpec((tm, in_w - 3 * hy_w - att_w - 2 * kv_w), lambda i: (i, 0))),
        compiler_params=_params("arbitrary"),
        name="inproj",
    )(x, mod_l, norm_g, w_in_bf, cos_t, sin_t, qg, kg, blk)


def _shortconv_kernel(u_ref, p_ref, n_ref, w_ref, b_ref, z_ref, ga_ref, gb_ref, *, nper, hy_w):
    i = pl.program_id(0)
    u = u_ref[...]
    tm = u.shape[0]
    first = (i % nper) == 0
    last = (i % nper) == nper - 1
    prev_row = jnp.where(first, 0.0, p_ref[7:8, :])
    next_row = jnp.where(last, 0.0, n_ref[0:1, :])
    row = lax.broadcasted_iota(I32, u.shape, 0)
    up = jnp.where(row == 0, prev_row, pltpu.roll(u, 1, 0))
    dn = jnp.where(row == tm - 1, next_row, pltpu.roll(u, tm - 1, 0))
    y = up * w_ref[0:1, :] + u * w_ref[1:2, :] + dn * w_ref[2:3, :] + b_ref[...]
    for r in range(tm // LANES):
        blk = y[r * LANES:(r + 1) * LANES, :]
        z_ref[r] = blk[:, 0:hy_w].T
        ga_ref[r] = blk[:, hy_w:2 * hy_w].T
        gb_ref[r] = blk[:, 2 * hy_w:3 * hy_w].T


def _shortconv(hy, L, conv_w, conv_b, hy_w):
    T, W = hy.shape
    tm = 512
    nper = L // tm
    r8 = tm // 8
    nb8 = T // 8
    kern = functools.partial(_shortconv_kernel, nper=nper, hy_w=hy_w)
    o = jax.ShapeDtypeStruct((T // LANES, hy_w, LANES), F32)
    ospec = pl.BlockSpec((tm // LANES, hy_w, LANES), lambda i: (i, 0, 0))
    return pl.pallas_call(
        kern,
        out_shape=(o, o, o),
        grid=(T // tm,),
        in_specs=[pl.BlockSpec((tm, W), lambda i: (i, 0)),
                  pl.BlockSpec((8, W), lambda i: (jnp.maximum(i * r8 - 1, 0), 0)),
                  pl.BlockSpec((8, W), lambda i: (jnp.minimum((i + 1) * r8, nb8 - 1), 0)),
                  pl.BlockSpec((3, W), lambda i: (0, 0)),
                  pl.BlockSpec((1, W), lambda i: (0, 0))],
        out_specs=(ospec, ospec, ospec),
        compiler_params=_params("arbitrary"),
        name="shortconv",
    )(hy, hy, hy, conv_w, conv_b.reshape(1, W))


def _filter_kernel(z_ref, tl_ref, w1_ref, b1_ref, w2_ref, b2_ref, w3_ref, b3_ref, fr_ref, woa_ref, wob_ref, dl_ref,
                   k_ref, ssq_ref, *, L):
    i = pl.program_id(0)
    z = z_ref[...]
    tr = z.shape[0]
    half = tr // 2
    fr = fr_ref[...]

    def lin(a, w_ref_, b_ref_):
        ah, al = _split(a)
        wh, wl = _split(w_ref_[...])
        return _dot3(ah, al, wh, wl) + b_ref_[...]

    zz = jnp.concatenate([z[:half], z[half:]], axis=1)
    h = jnp.sin(fr * lin(zz, w1_ref, b1_ref))
    h = jnp.sin(fr * lin(h, w2_ref, b2_ref))
    h = jnp.sin(fr * lin(h, w3_ref, b3_ref))
    hh, hl = _split(h)
    wts = (_split(woa_ref[0]), _split(wob_ref[0]))

    @pl.when(i == 0)
    def _():
        ssq_ref[...] = jnp.zeros_like(ssq_ref)

    lane = lax.broadcasted_iota(I32, (1, LANES), 1)
    per_half = half // LANES
    for r in range(tr // LANES):
        wh, wl = wts[r // per_half]
        rows = slice((r % per_half) * LANES, (r % per_half + 1) * LANES)
        kt = _dot_nt(wh, hh[rows]) + _dot_nt(wl, hh[rows]) + _dot_nt(wh, hl[rows])
        kt = kt * jnp.exp(-dl_ref[...] * tl_ref[r])
        kt = jnp.where((i * tr + r * LANES + lane) == L, 0.0, kt)
        k_ref[r] = kt
        ssq_ref[...] += kt * kt


def _filter_rows(L, p):
    hid = p['flt_w2'].shape[0]
    hy_w = p['hy_bias'].shape[1]
    cw = HY_ORDER * hy_w
    tr = 512
    N = 2 * L
    n1 = N // LANES
    t = jnp.linspace(0.0, 1.0, L, dtype=F32)[:, None]
    bands = (FILTER_EMB - 1) // 2
    f = jnp.linspace(1e-4, bands - 1, bands, dtype=F32)[None, :]
    w = (2.0 * math.pi / L) * jnp.arange(L, dtype=F32)[:, None]
    zf = jnp.concatenate([t, jnp.cos(f * w), -jnp.sin(f * w)], axis=-1)
    z = jnp.concatenate([zf, zf[L - 1:L], zf[:0:-1]], axis=0)
    tl = z[:, 0].reshape(n1, 1, LANES)
    z = jnp.pad(z, ((0, 0), (0, LANES - FILTER_EMB)))
    eye2 = jnp.eye(2, dtype=F32)
    w1 = jnp.concatenate([jnp.pad(p['flt_w1'], ((0, LANES - FILTER_EMB), (0, hid))),
                          jnp.pad(p['flt_w1'], ((0, LANES - FILTER_EMB), (hid, 0)))], axis=0)
    w2 = jnp.kron(eye2, p['flt_w2'])
    w3 = jnp.kron(eye2, p['flt_w3'])
    two = lambda v: jnp.tile(v, 2)[None]
    wo_t = p['flt_w_out'].reshape(hid, 2, cw).transpose(1, 2, 0)
    woa = jnp.pad(wo_t, ((0, 0), (0, 0), (0, hid)))
    wob = jnp.pad(wo_t, ((0, 0), (0, 0), (hid, 0)))
    max_decay = math.log(DECAY_TARGET) / FAST_DECAY_PCT
    min_decay = math.log(DECAY_TARGET) / SLOW_DECAY_PCT
    deltas = jnp.abs(jnp.linspace(min_decay, max_decay, hy_w, dtype=F32))
    dl = jnp.tile(deltas, HY_ORDER)[:, None]
    nblk = N // tr
    halfb = nblk // 2
    full = lambda i: (0, 0)
    h2 = 2 * hid
    return pl.pallas_call(
        functools.partial(_filter_kernel, L=L),
        out_shape=(jax.ShapeDtypeStruct((n1, cw, LANES), F32), jax.ShapeDtypeStruct((cw, LANES), F32)),
        grid=(nblk,),
        in_specs=[pl.BlockSpec((tr, LANES), lambda i: (i, 0)),
                  pl.BlockSpec((tr // LANES, 1, LANES), lambda i: (i, 0, 0)),
                  pl.BlockSpec((2 * LANES, h2), full), pl.BlockSpec((1, h2), full),
                  pl.BlockSpec((h2, h2), full), pl.BlockSpec((1, h2), full),
                  pl.BlockSpec((h2, h2), full), pl.BlockSpec((1, h2), full),
                  pl.BlockSpec((1, h2), full),
                  pl.BlockSpec((1, cw, h2), lambda i: (i // halfb, 0, 0)),
                  pl.BlockSpec((1, cw, h2), lambda i: (i // halfb, 0, 0)),
                  pl.BlockSpec((cw, 1), full)],
        out_specs=(pl.BlockSpec((tr // LANES, cw, LANES), lambda i: (i, 0, 0)), pl.BlockSpec((cw, LANES), full)),
        compiler_params=_params("arbitrary"),
        name="hyena_filter",
    )(z, tl, w1, two(p['flt_b1']), w2, two(p['flt_b2']), w3, two(p['flt_b3']), two(p['flt_freq']), woa, wob, dl)


def _fft_consts(L, planes):
    N = 2 * L
    n1 = N // FFT_N2
    n1h = n1 // 2
    k1 = np.arange(n1)[:, None]
    th = 2.0 * np.pi * k1 * np.arange(n1)[None, :] / n1
    c, s = np.cos(th), np.sin(th)
    ch, sh = c[:, :n1h], s[:, :n1h]
    if planes == 1:
        m1 = np.concatenate([ch, -sh], axis=0)
        m3 = np.concatenate([ch.T, -sh.T], axis=1) / N
    else:
        m1 = np.block([[ch, sh], [-sh, ch]])
        m3 = np.block([[ch.T, -sh.T], [sh.T, ch.T]]) / N
    m1_full = np.concatenate([c, -s], axis=0)
    ph = 2.0 * np.pi * k1 * np.arange(FFT_N2)[None, :] / N
    twr, twi = np.cos(ph), -np.sin(ph)
    a = 2.0 * np.pi * np.arange(FFT_N2)[:, None] * np.arange(FFT_N2)[None, :] / FFT_N2
    ca, sa = np.cos(a), np.sin(a)
    rt = np.block([[ca, -sa], [sa, ca]])
    rit = np.block([[ca, sa], [-sa, ca]])
    bf = lambda m: jnp.asarray(np.asarray(m, np.float32).astype(BF16))
    return dict(n1=n1, n1h=n1h, m1=bf(m1), m3=bf(m3), m1_full=bf(m1_full),
                twr=jnp.asarray(twr, F32), twi=jnp.asarray(twi, F32), rt=bf(rt), rit=bf(rit))


CONV_CH = 8


def _load_channels(ref, lead):
    return jnp.concatenate([ref[lead + (slice(None), c, slice(None))] for c in range(CONV_CH)], axis=1)


def _fwd_lane_dft(ar, ai, twr, twi, rt):
    pieces = []
    for c in range(CONV_CH):
        r = ar[:, c * LANES:(c + 1) * LANES]
        i = ai[:, c * LANES:(c + 1) * LANES]
        pieces.append(jnp.concatenate([r * twr - i * twi, r * twi + i * twr], axis=1))
    return _dot(jnp.concatenate(pieces, axis=0).astype(BF16), rt)


def _spectrum_kernel(k_ref, ssq_ref, m1_ref, twr_ref, twi_ref, rt_ref, o_ref):
    n1 = twr_ref.shape[0]
    a = _dot(m1_ref[...], _load_channels(k_ref, ()).astype(BF16))
    x = _fwd_lane_dft(a[:n1], a[n1:], twr_ref[...], twi_ref[...], rt_ref[...])
    scale = lax.rsqrt(jnp.sum(ssq_ref[...], axis=1, keepdims=True) + EPS)
    for c in range(CONV_CH):
        o_ref[0, :, c * 2 * LANES:(c + 1) * 2 * LANES] = x[c * n1:(c + 1) * n1] * scale[c:c + 1, :]


def _filter_spectrum(k3, ssq, fc, hy_w):
    n1 = fc['n1']
    ct = CONV_CH
    ntile = hy_w // ct
    full = lambda o, j: (0, 0)
    return pl.pallas_call(
        _spectrum_kernel,
        out_shape=jax.ShapeDtypeStruct((HY_ORDER, n1, hy_w * 2 * LANES), F32),
        grid=(HY_ORDER, ntile),
        in_specs=[pl.BlockSpec((n1, ct, LANES), lambda o, j: (0, o * ntile + j, 0)),
                  pl.BlockSpec((ct, LANES), lambda o, j: (o * ntile + j, 0)),
                  pl.BlockSpec((2 * n1, n1), full),
                  pl.BlockSpec((n1, LANES), full), pl.BlockSpec((n1, LANES), full),
                  pl.BlockSpec((2 * LANES, 2 * LANES), full)],
        out_specs=pl.BlockSpec((1, n1, ct * 2 * LANES), lambda o, j: (o, 0, j)),
        compiler_params=_params("arbitrary", "arbitrary"),
        name="filter_spectrum",
    )(k3, ssq, fc['m1_full'], fc['twr'], fc['twi'], fc['rt'])


def _longconv_kernel(z_ref, g_ref, kf_ref, b_ref, m1_ref, twr_ref, twi_ref, rt_ref, rit_ref, m3_ref, o_ref):
    ct = CONV_CH
    n1 = twr_ref.shape[0]
    planes, n1h = z_ref.shape[1], z_ref.shape[2]
    z = jnp.concatenate([_load_channels(z_ref, (0, p)) for p in range(planes)], axis=0)
    a = _dot(m1_ref[...], z.astype(BF16))
    twr = twr_ref[...]
    twi = twi_ref[...]
    x = _fwd_lane_dft(a[:n1], a[n1:], twr, twi, rt_ref[...])
    ys = []
    for c in range(ct):
        xc = x[c * n1:(c + 1) * n1]
        kc = kf_ref[0, :, c * 2 * LANES:(c + 1) * 2 * LANES]
        xr, xi = xc[:, :LANES], xc[:, LANES:]
        kr, ki = kc[:, :LANES], kc[:, LANES:]
        ys.append(jnp.concatenate([xr * kr - xi * ki, xr * ki + xi * kr], axis=1))
    bp = _dot(jnp.concatenate(ys, axis=0).astype(BF16), rit_ref[...])
    brs, bis = [], []
    for c in range(ct):
        bc = bp[c * n1:(c + 1) * n1]
        r, i = bc[:, :LANES], bc[:, LANES:]
        brs.append(r * twr + i * twi)
        bis.append(i * twr - r * twi)
    bst = jnp.concatenate([jnp.concatenate(brs, axis=1), jnp.concatenate(bis, axis=1)], axis=0)
    yt = _dot(m3_ref[...], bst.astype(BF16))
    for p in range(planes):
        rows = slice(p * n1h, (p + 1) * n1h)
        for c in range(ct):
            lanes = slice(c * LANES, (c + 1) * LANES)
            o_ref[0, p, :, c, :] = g_ref[0, p, :, c, :] * (yt[rows, lanes] + b_ref[0, 0:1, c:c + 1] * z[rows, lanes])


def _longconv(z5, g5, kf, order, bias_t, fc):
    npair, planes, n1h, ch, _ = z5.shape
    n1 = fc['n1']
    ct = CONV_CH
    ntile = ch // ct
    full = lambda p, j: (0, 0)
    m1r, m1c = 2 * n1, planes * n1h
    blk = pl.BlockSpec((1, planes, n1h, ct, LANES), lambda p, j: (p, 0, 0, j, 0))
    return pl.pallas_call(
        _longconv_kernel,
        out_shape=jax.ShapeDtypeStruct(z5.shape, F32),
        grid=(npair, ntile),
        in_specs=[blk, blk,
                  pl.BlockSpec((1, n1, ct * 2 * LANES), lambda p, j: (order, 0, j)),
                  pl.BlockSpec((1, 8, ct), lambda p, j: (order * ntile + j, 0, 0)),
                  pl.BlockSpec((m1r, m1c), full),
                  pl.BlockSpec((n1, LANES), full), pl.BlockSpec((n1, LANES), full),
                  pl.BlockSpec((2 * LANES, 2 * LANES), full), pl.BlockSpec((2 * LANES, 2 * LANES), full),
                  pl.BlockSpec((m1c, m1r), full)],
        out_specs=blk,
        compiler_params=_params("arbitrary", "arbitrary"),
        name="hyena_longconv",
    )(z5, g5, kf, bias_t, fc['m1'], fc['twr'], fc['twi'], fc['rt'], fc['rit'], fc['m3'])


def _attn_kernel(sink_ref, q_ref, kp_ref, km_ref, kn_ref, vp_ref, vm_ref, vn_ref, o_ref, *, L, tq):
    i = pl.program_id(1)
    t0 = i * tq
    tk = tq + 2 * WINDOW
    kk = jnp.concatenate([kp_ref[...], km_ref[...], kn_ref[...]], axis=0)
    vv = jnp.concatenate([vp_ref[...], vm_ref[...], vn_ref[...]], axis=0)
    qpos = t0 + lax.broadcasted_iota(I32, (tq, tk), 0)
    kpos = t0 - WINDOW + lax.broadcasted_iota(I32, (tq, tk), 1)
    valid = (jnp.abs(kpos - qpos) <= WINDOW) & (kpos >= 0) & (kpos < L)
    for h in range(N_HEADS):
        s = _dot_nt(q_ref[:, h * LANES:(h + 1) * LANES], kk)
        s = jnp.where(valid, s, -1e30)
        sink = sink_ref[h]
        m = jnp.maximum(jnp.max(s, axis=-1, keepdims=True), sink)
        p = jnp.exp(s - m)
        denom = jnp.sum(p, axis=-1, keepdims=True) + jnp.exp(sink - m)
        o_ref[:, h * LANES:(h + 1) * LANES] = _dot((p / denom).astype(BF16), vv).astype(BF16)


def _attention(q, k, v, sink, B, L):
    T = q.shape[0]
    tq = 256
    nq = L // tq
    r = tq // WINDOW
    nb = L // WINDOW
    kv_w = k.shape[1]
    kp = pl.BlockSpec((WINDOW, kv_w), lambda b, i: (b * nb + jnp.maximum(i * r - 1, 0), 0))
    km = pl.BlockSpec((tq, kv_w), lambda b, i: (b * nq + i, 0))
    kn = pl.BlockSpec((WINDOW, kv_w), lambda b, i: (b * nb + jnp.minimum((i + 1) * r, nb - 1), 0))
    qs = pl.BlockSpec((tq, N_HEADS * LANES), lambda b, i: (b * nq + i, 0))
    return pl.pallas_call(
        functools.partial(_attn_kernel, L=L, tq=tq),
        out_shape=jax.ShapeDtypeStruct((T, N_HEADS * LANES), BF16),
        grid=(B, nq),
        in_specs=[pl.BlockSpec(memory_space=pltpu.SMEM), qs, kp, km, kn, kp, km, kn],
        out_specs=qs,
        compiler_params=_params("arbitrary", "arbitrary"),
        name="band_attention",
    )(sink, q, k, k, k, v, v, v)


def _merge_kernel(x_ref, z_ref, a_ref, gs_ref, mod_ref, why_ref, wat_ref, wo_ref, g2_ref, rwh_ref, rwl_ref,
                  xo_ref, h2_ref, aff_ref):
    d = x_ref.shape[1]
    mod = mod_ref[0]
    z = jnp.concatenate([z_ref[r].T for r in range(z_ref.shape[0])], axis=0)
    y_hy = _dot(z.astype(BF16), why_ref[...])
    y_at = _dot(a_ref[...], wat_ref[...])
    gs = gs_ref[...]
    merged = gs[:, :d] * y_hy + gs[:, d:] * y_at
    x = x_ref[...] + mod[2:3] * _dot(merged.astype(BF16), wo_ref[...])
    xo_ref[...] = x
    ms = jnp.mean(x * x, axis=-1, keepdims=True)
    h = (x * lax.rsqrt(ms + EPS) * g2_ref[...]) * (1.0 + mod[4:5]) + mod[3:4]
    h2_ref[...] = h
    hh, hl = _split(h)
    rh, rl = rwh_ref[...], rwl_ref[...]
    logits = _dot_nt(rh, hh) + _dot_nt(rl, hh) + _dot_nt(rh, hl)
    m = jnp.max(logits, axis=0, keepdims=True)
    e = jnp.exp(logits - m)
    aff_ref[...] = e / jnp.sum(e, axis=0, keepdims=True)


def _merge(x, z, att, gs, mod_l, b0, L, w_hy, w_at_x, w_o, norm2_g, rw_hi, rw_lo):
    T, D = x.shape
    tm = 256
    nper = L // tm
    E = rw_hi.shape[0]
    full = lambda i: (0, 0)
    row = lambda w: pl.BlockSpec((tm, w), lambda i: (i, 0))
    return pl.pallas_call(
        _merge_kernel,
        out_shape=(jax.ShapeDtypeStruct((T, D), F32), jax.ShapeDtypeStruct((T, D), F32),
                   jax.ShapeDtypeStruct((E, T), F32)),
        grid=(T // tm,),
        in_specs=[row(D), pl.BlockSpec((tm // LANES,) + z.shape[1:], lambda i: (i, 0, 0)), row(att.shape[1]), row(gs.shape[1]),
                  pl.BlockSpec((1, 6, D), lambda i: (b0 + i // nper, 0, 0)),
                  pl.BlockSpec(w_hy.shape, full), pl.BlockSpec(w_at_x.shape, full), pl.BlockSpec(w_o.shape, full),
                  pl.BlockSpec((1, D), full), pl.BlockSpec((E, D), full), pl.BlockSpec((E, D), full)],
        out_specs=(row(D), row(D), pl.BlockSpec((E, tm), lambda i: (0, i))),
        compiler_params=_params("arbitrary"),
        name="merge_router",
    )(x, z, att, gs, mod_l, w_hy, w_at_x, w_o, norm2_g, rw_hi, rw_lo)


def _cumsum_rows(mb, u, ones, ls):
    rc = _dot(mb, u)
    totb = _dot(mb, ones)
    offs = _dot(ls, totb.astype(BF16))
    return rc, totb, offs


def _topk_kernel(aff_ref, u_ref, ut_ref, ones_ref, ls_ref, idx_ref, gate_ref, dst_ref, cum_ref, cnt_ref,
                 thr_ref, sel_ref, rank_ref, *, cap):
    E, R, _ = aff_ref.shape
    bits = pltpu.bitcast(aff_ref[...], I32)

    def search(i, cur):
        cand = cur | jnp.left_shift(jnp.int32(1), 30 - i)
        ge = jnp.where(bits >= cand, 1.0, 0.0)
        c = jnp.sum(jnp.sum(ge, axis=2, keepdims=True), axis=1, keepdims=True)
        return jnp.where(c >= cap, cand, cur)

    thr = lax.fori_loop(0, 31, search, jnp.zeros((E, 1, 1), I32))
    thr_ref[...] = jnp.broadcast_to(thr, thr_ref.shape)
    u = u_ref[...]
    ut = ut_ref[...]
    ones = ones_ref[...]
    ls = ls_ref[...]

    def select(e, carry):
        cnt_run, cum = carry
        b = pltpu.bitcast(aff_ref[e], I32)
        t = thr_ref[e][0:1, :]
        gt = b > t
        eq = b == t
        ngt = jnp.sum(jnp.sum(jnp.where(gt, 1.0, 0.0), axis=1, keepdims=True), axis=0, keepdims=True)
        need = cap - ngt
        erc, _, eoffs = _cumsum_rows(jnp.where(eq, 1.0, 0.0).astype(BF16), u, ones, ls)
        sel = jnp.where(gt | (eq & ((erc + eoffs) <= need)), 1.0, 0.0)
        rc, _, offs = _cumsum_rows(sel.astype(BF16), u, ones, ls)
        sel_ref[e] = sel
        rank_ref[e] = cnt_run
        return cnt_run + sel, cum + (rc + offs - sel)

    zero = jnp.zeros((R, LANES), F32)
    cnt, cum = lax.fori_loop(0, E, select, (zero, zero))
    cnt_ref[...] = cnt.astype(I32)
    cum_ref[...] = cum.astype(I32)
    s_row = lax.broadcasted_iota(I32, (1, cap), 1).astype(F32)
    r_col = lax.broadcasted_iota(I32, (R, 1), 0).astype(F32)
    c_col = lax.broadcasted_iota(I32, (LANES, 1), 0).astype(F32)

    def pick(vt_pieces, ohb, hit):
        acc = _dot(vt_pieces[0], ohb)
        for piece in vt_pieces[1:]:
            acc = acc + _dot(piece, ohb)
        return jnp.sum(jnp.where(hit, acc, 0.0), axis=0, keepdims=True)

    def compact(e, carry):
        sel = sel_ref[e]
        _, totb, offs = _cumsum_rows(sel.astype(BF16), u, ones, ls)
        offs_col = offs[:, 0:1]
        end_col = offs_col + totb[:, 0:1]
        oht = (offs_col <= s_row) & (s_row < end_col)
        row_s = jnp.sum(jnp.where(oht, r_col, 0.0), axis=0, keepdims=True)
        offs_s = jnp.sum(jnp.where(oht, offs_col, 0.0), axis=0, keepdims=True)
        ohb = jnp.where(oht, 1.0, 0.0).astype(BF16)
        rct = _dot(ut, sel.T.astype(BF16))
        rcs = _dot(rct.astype(BF16), ohb)
        col_s = jnp.sum(jnp.where(rcs <= (s_row - offs_s), 1.0, 0.0), axis=0, keepdims=True)
        hit = c_col == col_s
        idx_ref[pl.ds(e, 1), :] = (row_s * LANES + col_s).astype(I32)
        at = aff_ref[e].T
        p0 = at.astype(BF16)
        r1 = at - p0.astype(F32)
        p1 = r1.astype(BF16)
        p2 = (r1 - p1.astype(F32)).astype(BF16)
        gate_ref[pl.ds(e, 1), :] = pick((p0, p1, p2), ohb, hit)
        d = (cum + rank_ref[e]).T
        d_hi = (d * (1.0 / 256.0)).astype(I32).astype(F32)
        d_lo = d - 256.0 * d_hi
        dst = 256.0 * pick((d_hi.astype(BF16),), ohb, hit) + pick((d_lo.astype(BF16),), ohb, hit)
        dst_ref[pl.ds(e, 1), :] = dst.astype(I32)
        return carry

    lax.fori_loop(0, E, compact, 0)


def _topk(aff_t, cap):
    E, T = aff_t.shape
    R = T // LANES
    u = jnp.asarray(np.triu(np.ones((LANES, LANES))), BF16)
    ut = jnp.asarray(np.tril(np.ones((LANES, LANES))), BF16)
    ones = jnp.ones((LANES, LANES), BF16)
    ls = jnp.asarray(np.tril(np.ones((R, R)), -1), BF16)
    z2 = lambda: (0, 0)
    z3 = lambda: (0, 0, 0)
    slot = jax.ShapeDtypeStruct((E, cap), I32)
    tok = jax.ShapeDtypeStruct((R, LANES), I32)
    idx, gate, dst, cum, cnt = pl.pallas_call(
        functools.partial(_topk_kernel, cap=cap),
        out_shape=(slot, jax.ShapeDtypeStruct((E, cap), F32), slot, tok, tok),
        in_specs=[pl.BlockSpec((E, R, LANES), z3), pl.BlockSpec((LANES, LANES), z2), pl.BlockSpec((LANES, LANES), z2),
                  pl.BlockSpec((LANES, LANES), z2), pl.BlockSpec((R, R), z2)],
        out_specs=(pl.BlockSpec((E, cap), z2), pl.BlockSpec((E, cap), z2), pl.BlockSpec((E, cap), z2),
                   pl.BlockSpec((R, LANES), z2), pl.BlockSpec((R, LANES), z2)),
        scratch_shapes=[pltpu.VMEM((E, 8, LANES), I32), pltpu.VMEM((E, R, LANES), F32),
                        pltpu.VMEM((E, R, LANES), F32)],
        compiler_params=pltpu.CompilerParams(vmem_limit_bytes=VMEM_LIMIT),
        name="expert_topk",
    )(aff_t.reshape(E, R, LANES), u, ut, ones, ls)
    return idx, gate, dst, cum.reshape(T), cnt.reshape(T)


GATHER_STEPS = 8
ROW_CHUNK = 512


def _expert_kernel(idx_ref, dst_ref, h_ref, wg_ref, wu_ref, wd_ref, gate_ref, out_ref,
                   xbuf, xb, acc, gsem, ssem, *, cap):
    e = pl.program_id(0)
    f = pl.program_id(1)
    ne = pl.num_programs(0)
    nf = pl.num_programs(1)
    slot = e % 2
    chunk = cap // GATHER_STEPS

    def gather_rows(base, lo, n):
        def body(s, c):
            r = idx_ref[base + lo + s]
            pltpu.make_async_copy(h_ref.at[pl.ds(r, 1), :], xbuf.at[pl.ds(lo + s, 1), :], gsem).start()
            return c

        lax.fori_loop(0, n, body, 0, unroll=8)

    def scatter_done(s):
        pltpu.make_async_copy(acc.at[s], out_ref.at[pl.ds(0, cap), :], ssem.at[s]).wait()

    @pl.when((e == 0) & (f == 0))
    def _():
        gather_rows(0, 0, cap)

    @pl.when(f == 0)
    def _():
        pltpu.make_async_copy(h_ref.at[pl.ds(0, cap), :], xbuf, gsem).wait()
        xb[...] = xbuf[...].astype(BF16)

    @pl.when((f == 0) & (e >= 2))
    def _():
        scatter_done(slot)

    @pl.when(f == 0)
    def _():
        acc[slot] = jnp.zeros(acc.shape[1:], F32)

    @pl.when((f < GATHER_STEPS) & (e + 1 < ne))
    def _():
        gather_rows((e + 1) * cap, f * chunk, chunk)

    wg = wg_ref[0].astype(BF16)
    wu = wu_ref[0].astype(BF16)
    wd = wd_ref[0].astype(BF16)
    for r0 in range(0, cap, ROW_CHUNK):
        rows = pl.ds(r0, min(ROW_CHUNK, cap - r0))
        x = xb[rows, :]
        g = _dot(x, wg)
        u = _dot(x, wu)
        hid = (g * jax.nn.sigmoid(g) * u).astype(BF16)
        acc[slot, rows, :] += _dot(hid, wd)

    @pl.when(f == nf - 1)
    def _():
        acc[slot] = gate_ref[0] * acc[slot]

        def body(s, c):
            d = dst_ref[e * cap + s]
            pltpu.make_async_copy(acc.at[slot, pl.ds(s, 1), :], out_ref.at[pl.ds(d, 1), :], ssem.at[slot]).start()
            return c

        lax.fori_loop(0, cap, body, 0, unroll=8)

    @pl.when((f == nf - 1) & (e == ne - 1))
    def _():
        scatter_done(slot)
        scatter_done(1 - slot)


def _experts(h2, idx_flat, dst_flat, w_gate, w_up, w_down, gate_col, cap):
    E, D, F = w_gate.shape
    ft = 256
    assert E >= 2 and cap % GATHER_STEPS == 0 and F // ft >= GATHER_STEPS
    return pl.pallas_call(
        functools.partial(_expert_kernel, cap=cap),
        out_shape=jax.ShapeDtypeStruct((E * cap, D), F32),
        grid_spec=pltpu.PrefetchScalarGridSpec(
            num_scalar_prefetch=2,
            grid=(E, F // ft),
            in_specs=[pl.BlockSpec(memory_space=pl.ANY),
                      pl.BlockSpec((1, D, ft), lambda e, f, i, d: (e, 0, f)),
                      pl.BlockSpec((1, D, ft), lambda e, f, i, d: (e, 0, f)),
                      pl.BlockSpec((1, ft, D), lambda e, f, i, d: (e, f, 0)),
                      pl.BlockSpec((1, cap, 1), lambda e, f, i, d: (e, 0, 0))],
            out_specs=pl.BlockSpec(memory_space=pl.ANY),
            scratch_shapes=[pltpu.VMEM((cap, D), F32), pltpu.VMEM((cap, D), BF16), pltpu.VMEM((2, cap, D), F32),
                            pltpu.SemaphoreType.DMA, pltpu.SemaphoreType.DMA((2,))]),
        compiler_params=_params("arbitrary", "arbitrary"),
        name="expert_ffn",
    )(idx_flat, dst_flat, h2, w_gate, w_up, w_down, gate_col)


COMBINE_ROWS = 256


def _combine_kernel(tile_ref, blk_ref, flag_ref, c_ref, x_ref, mod_ref, tab_ref, eye_ref, o_ref):
    j = pl.program_id(0)
    flag = flag_ref[j]
    first = (flag & 1) != 0
    last = (flag & 2) != 0
    skip = (flag & 4) != 0
    rb = c_ref.shape[0]
    cols = _dot_nt(eye_ref[...], tab_ref[0].astype(BF16))
    start = cols[:, 0:1] * 256.0 + cols[:, 1:2]
    count = cols[:, 2:3]
    rabs = (blk_ref[j] * rb + lax.broadcasted_iota(I32, (1, rb), 1)).astype(F32)
    seg = jnp.where((start <= rabs) & (rabs < start + count), 1.0, 0.0).astype(BF16)
    hi, lo = _split(c_ref[...])
    part = _dot(seg, hi) + _dot(seg, lo)

    @pl.when(first)
    def _():
        o_ref[...] = part

    @pl.when(jnp.logical_not(first | skip))
    def _():
        o_ref[...] += part

    @pl.when(last & jnp.logical_not(skip))
    def _():
        o_ref[...] = x_ref[...] + mod_ref[0][5:6] * o_ref[...]


def _combine(x, mod_l, b0, L, contrib, cum, cnt):
    T, D = x.shape
    tt = LANES
    R = T // tt
    rb = COMBINE_ROWS
    nr = contrib.shape[0]
    nb = nr // rb
    nper = L // tt
    nitem = R + nb
    lo = cum[::tt]
    hi = jnp.concatenate([lo[1:], jnp.full((1,), nr, I32)])
    b_lo = jnp.minimum(lo // rb, nb - 1)
    b_hi = jnp.minimum(jnp.maximum((hi - 1) // rb, b_lo), nb - 1)
    n = b_hi - b_lo + 1
    cn = jnp.cumsum(n)
    start = cn - n
    j = jnp.arange(nitem, dtype=I32)
    valid = j < cn[-1]
    ti = jnp.minimum(jnp.searchsorted(cn, j, side='right').astype(I32), R - 1)
    blk = jnp.where(valid, b_lo[ti] + (j - start[ti]), b_hi[R - 1])
    flag = jnp.where(valid, (j == start[ti]) * 1 + (j == cn[ti] - 1) * 2, 4).astype(I32)
    tab = jnp.stack([(cum >> 8).astype(F32), (cum & 255).astype(F32), cnt.astype(F32)], axis=0)
    tab = jnp.concatenate([tab, jnp.zeros((5, T), F32)], axis=0).reshape(8, R, tt).transpose(1, 0, 2)
    eye = jnp.eye(tt, dtype=BF16)
    return pl.pallas_call(
        _combine_kernel,
        out_shape=jax.ShapeDtypeStruct((T, D), F32),
        grid_spec=pltpu.PrefetchScalarGridSpec(
            num_scalar_prefetch=3,
            grid=(nitem,),
            in_specs=[pl.BlockSpec((rb, D), lambda j, t, b, f: (b[j], 0)),
                      pl.BlockSpec((tt, D), lambda j, t, b, f: (t[j], 0)),
                      pl.BlockSpec((1, 6, D), lambda j, t, b, f: (b0 + t[j] // nper, 0, 0)),
                      pl.BlockSpec((1, 8, tt), lambda j, t, b, f: (t[j], 0, 0)),
                      pl.BlockSpec((tt, tt), lambda j, t, b, f: (0, 0))],
            out_specs=pl.BlockSpec((tt, D), lambda j, t, b, f: (t[j], 0))),
        compiler_params=_params("arbitrary"),
        name="moe_combine",
    )(ti, blk, flag, contrib, x, mod_l, tab, eye)


def _layer(x, B, L, b0, mod_l, p, kf, fc, tabs):
    hy_w = p['hy_bias'].shape[1]
    att_w = N_HEADS * HEAD_DIM
    kv_w = N_KV_HEADS * HEAD_DIM
    planes = 2 if B % 2 == 0 else 1
    hy, q, k, v, gs = _inproj(x, mod_l, b0, L, p['norm1_g'][None], p['w_in_bf'], tabs['cos'], tabs['sin'],
                              tabs['qg'], tabs['kg'], tabs['blk'], hy_w, att_w, kv_w)
    z, ga, gb = _shortconv(hy, L, p['hy_conv_w'], p['hy_conv_b'], hy_w)
    shape5 = (B // planes, planes, L // LANES, hy_w, LANES)
    z = z.reshape(shape5)
    gates = (ga.reshape(shape5), gb.reshape(shape5))
    for o in range(HY_ORDER):
        z = _longconv(z, gates[o], kf, o, p['bias_t'], fc)
    z = z.reshape(B * L // LANES, hy_w, LANES)
    att = _attention(q, k, v, p['sink_logit'], B, L)
    x, h2, aff_t = _merge(x, z, att, gs, mod_l, b0, L, p['w_hy_bf'], p['w_at_x'], p['w_o_bf'], p['norm2_g'][None],
                          p['rw_hi'], p['rw_lo'])
    T = x.shape[0]
    cap = CAPACITY_FACTOR * T // N_EXPERTS
    idx, gate, dst, cum, cnt = _topk(aff_t, cap)
    contrib = _experts(h2, idx.reshape(-1), dst.reshape(-1), p['exp_w_gate'], p['exp_w_up'], p['exp_w_down'],
                       gate.reshape(N_EXPERTS, cap, 1), cap)
    return _combine(x, mod_l, b0, L, contrib, cum, cnt)


def _rope_tables(L):
    inv = 1.0 / (ROPE_THETA ** (jnp.arange(0, HEAD_DIM, 2, dtype=F32) / HEAD_DIM))
    ang = jnp.arange(L, dtype=F32)[:, None] * inv[None, :]
    cos, sin = jnp.cos(ang), jnp.sin(ang)
    cos_t = jnp.tile(cos, (1, LANES // (HEAD_DIM // 2)))
    sin_t = jnp.tile(jnp.concatenate([-sin, sin], axis=1), (1, LANES // HEAD_DIM))
    return cos_t, sin_t


def _filter_spectrum_for(L, p, fc):
    hy_w = p['hy_bias'].shape[1]
    k3, ssq = _filter_rows(L, p)
    return _filter_spectrum(k3, ssq, fc, hy_w)


def kernel(x_prompt, x_sample, c_prompt, c_sample, ada_w, ada_b, norm1_g, norm2_g, w_in, hy_conv_w, hy_conv_b,
           flt_w1, flt_b1, flt_w2, flt_b2, flt_w3, flt_b3, flt_freq, flt_w_out, hy_bias, q_norm_g, k_norm_g,
           sink_logit, w_hy_proj, w_at_proj, w_o, router_w, exp_w_gate, exp_w_up, exp_w_down):
    depth = ada_w.shape[0]
    Bp, Lp, D = x_prompt.shape
    Bs, Ls, _ = x_sample.shape
    nb = Bp + Bs
    c_all = jnp.concatenate([c_prompt, c_sample, jnp.zeros(((-nb) % 8, D), F32)], axis=0)
    mod = _adaln(c_all, ada_w, ada_b).reshape(depth, c_all.shape[0], 6, D)
    hy_w = hy_bias.shape[2]
    att_w = N_HEADS * HEAD_DIM
    groups = [dict(B=Bp, L=Lp, b0=0, x=x_prompt.reshape(Bp * Lp, D)),
              dict(B=Bs, L=Ls, b0=Bp, x=x_sample.reshape(Bs * Ls, D))]
    blk = jnp.asarray(np.kron(np.eye(LANES // HEAD_DIM), np.ones((HEAD_DIM, HEAD_DIM))), BF16)
    for g in groups:
        g['fc'] = _fft_consts(g['L'], 2 if g['B'] % 2 == 0 else 1)
        cos_t, sin_t = _rope_tables(g['L'])
        g['tabs'] = dict(cos=cos_t, sin=sin_t, blk=blk)
    heads_per_kv = N_HEADS // N_KV_HEADS
    for l in range(depth):
        w_at_x = jnp.zeros((N_HEADS, LANES, D), BF16)
        w_at_h = w_at_proj[l].astype(BF16).reshape(N_HEADS, HEAD_DIM, D)
        for h in range(N_HEADS):
            off = (h // heads_per_kv) * HEAD_DIM
            w_at_x = w_at_x.at[h, off:off + HEAD_DIM].set(w_at_h[h])
        rw_hi, rw_lo = _split(router_w[l].T)
        p = dict(norm1_g=norm1_g[l], norm2_g=norm2_g[l], w_in_bf=w_in[l].astype(BF16),
                 hy_conv_w=hy_conv_w[l], hy_conv_b=hy_conv_b[l],
                 flt_w1=flt_w1[l], flt_b1=flt_b1[l], flt_w2=flt_w2[l], flt_b2=flt_b2[l], flt_w3=flt_w3[l],
                 flt_b3=flt_b3[l], flt_freq=flt_freq[l], flt_w_out=flt_w_out[l], hy_bias=hy_bias[l],
                 bias_t=jnp.broadcast_to(hy_bias[l].reshape(-1, 1, CONV_CH), (HY_ORDER * hy_w // CONV_CH, 8, CONV_CH)),
                 sink_logit=sink_logit[l], w_hy_bf=w_hy_proj[l].astype(BF16),
                 w_at_x=w_at_x.reshape(N_HEADS * LANES, D), w_o_bf=w_o[l].astype(BF16),
                 rw_hi=rw_hi, rw_lo=rw_lo,
                 exp_w_gate=exp_w_gate[l], exp_w_up=exp_w_up[l], exp_w_down=exp_w_down[l])
        qg = jnp.tile(q_norm_g[l], LANES // HEAD_DIM)[None]
        kg = jnp.tile(k_norm_g[l], LANES // HEAD_DIM)[None]
        for g in groups:
            kf = _filter_spectrum_for(g['L'], p, g['fc'])
            tabs = dict(g['tabs'], qg=qg, kg=kg)
            g['x'] = _layer(g['x'], g['B'], g['L'], g['b0'], mod[l], p, kf, g['fc'], tabs)
    return (groups[0]['x'].reshape(Bp, Lp, D), groups[1]['x'].reshape(Bs, Ls, D))
```

```python
import functools
import math

import numpy as np
import jax
import jax.numpy as jnp
from jax import lax
from jax.experimental import pallas as pl
from jax.experimental.pallas import tpu as pltpu

F32 = jnp.float32
BF16 = jnp.bfloat16
I32 = jnp.int32

EPS = 1e-6
HEAD_DIM = 64
N_HEADS = 8
N_KV_HEADS = 2
WINDOW = 128
ROPE_THETA = 10000.0
HY_ORDER = 2
FILTER_EMB = 33
DECAY_TARGET = 0.01
FAST_DECAY_PCT = 0.3
SLOW_DECAY_PCT = 1.5
N_EXPERTS = 16
CAPACITY_FACTOR = 2
LANES = 128
FFT_N2 = LANES
VMEM_LIMIT = 52 << 20


def _params(*sem):
    return pltpu.CompilerParams(dimension_semantics=sem, vmem_limit_bytes=VMEM_LIMIT)


def _dot(a, b):
    return jnp.dot(a, b, preferred_element_type=F32)


def _dot_nt(a, b):
    return lax.dot_general(a, b, (((1,), (1,)), ((), ())), preferred_element_type=F32)


def _split(x):
    hi = x.astype(BF16)
    lo = (x - hi.astype(F32)).astype(BF16)
    return hi, lo


def _dot3(a_hi, a_lo, b_hi, b_lo):
    return _dot(a_hi, b_hi) + _dot(a_lo, b_hi) + _dot(a_hi, b_lo)


def _ada_kernel(c_ref, w_ref, b_ref, o_ref):
    c = c_ref[...]
    s = c * jax.nn.sigmoid(c)
    o_ref[0] = _dot(s.astype(BF16), w_ref[0].astype(BF16)) + b_ref[0]


def _adaln(c_all, ada_w, ada_b):
    depth, d, n = ada_w.shape
    tn = n // 4
    return pl.pallas_call(
        _ada_kernel,
        out_shape=jax.ShapeDtypeStruct((depth, c_all.shape[0], n), F32),
        grid=(depth, n // tn),
        in_specs=[pl.BlockSpec(c_all.shape, lambda l, j: (0, 0)),
                  pl.BlockSpec((1, d, tn), lambda l, j: (l, 0, j)),
                  pl.BlockSpec((1, 1, tn), lambda l, j: (l, 0, j))],
        out_specs=pl.BlockSpec((1, c_all.shape[0], tn), lambda l, j: (l, 0, j)),
        compiler_params=_params("arbitrary", "arbitrary"),
        name="adaln",
    )(c_all, ada_w, ada_b.reshape(depth, 1, n))


def _head_norm_rope(x, gain, cos, sin, blk_hi):
    sq = x * x
    hi, lo = _split(sq)
    ssum = _dot(hi, blk_hi) + _dot(lo, blk_hi)
    xn = x * lax.rsqrt(ssum * (1.0 / HEAD_DIM) + EPS) * gain
    lane = lax.broadcasted_iota(I32, x.shape, 1)
    first_half = (lane % HEAD_DIM) < (HEAD_DIM // 2)
    rot = jnp.where(first_half, pltpu.roll(xn, LANES - HEAD_DIM // 2, 1), pltpu.roll(xn, HEAD_DIM // 2, 1))
    return xn * cos + rot * sin


def _inproj_kernel(x_ref, mod_ref, g_ref, w_ref, cos_ref, sin_ref, qg_ref, kg_ref, blk_ref,
                   hy_ref, q_ref, k_ref, v_ref, gs_ref, *, hy_w, att_w, kv_w):
    x = x_ref[...]
    mod = mod_ref[0]
    ms = jnp.mean(x * x, axis=-1, keepdims=True)
    h = (x * lax.rsqrt(ms + EPS) * g_ref[...]) * (1.0 + mod[1:2]) + mod[0:1]
    hb = h.astype(BF16)
    s0 = 3 * hy_w
    s1 = s0 + att_w
    s2 = s1 + kv_w
    s3 = s2 + kv_w
    hy_ref[...] = _dot(hb, w_ref[:, 0:s0])
    q = _dot(hb, w_ref[:, s0:s1])
    k = _dot(hb, w_ref[:, s1:s2])
    v_ref[...] = _dot(hb, w_ref[:, s2:s3]).astype(BF16)
    gs_ref[...] = jax.nn.sigmoid(_dot(hb, w_ref[:, s3:]))
    cos = cos_ref[...]
    sin = sin_ref[...]
    blk = blk_ref[...]
    k_ref[...] = _head_norm_rope(k, kg_ref[...], cos, sin, blk).astype(BF16)
    lane = lax.broadcasted_iota(I32, cos.shape, 1)
    low = lane < HEAD_DIM
    heads_per_kv = N_HEADS // N_KV_HEADS
    for ci in range(att_w // LANES):
        qr = _head_norm_rope(q[:, ci * LANES:(ci + 1) * LANES], qg_ref[...], cos, sin, blk) * (HEAD_DIM ** -0.5)
        qs = pltpu.roll(qr, HEAD_DIM, 1)
        kvh = (2 * ci) // heads_per_kv
        if kvh == 0:
            h0 = jnp.where(low, qr, 0.0)
            h1 = jnp.where(low, qs, 0.0)
        else:
            h0 = jnp.where(low, 0.0, qs)
            h1 = jnp.where(low, 0.0, qr)
        q_ref[:, (2 * ci) * LANES:(2 * ci + 1) * LANES] = h0.astype(BF16)
        q_ref[:, (2 * ci + 1) * LANES:(2 * ci + 2) * LANES] = h1.astype(BF16)


def _inproj(x, mod_l, b0, L, norm_g, w_in_bf, cos_t, sin_t, qg, kg, blk, hy_w, att_w, kv_w):
    T, D = x.shape
    tm = 256
    nper = L // tm
    in_w = w_in_bf.shape[1]
    kern = functools.partial(_inproj_kernel, hy_w=hy_w, att_w=att_w, kv_w=kv_w)
    return pl.pallas_call(
        kern,
        out_shape=(jax.ShapeDtypeStruct((T, 3 * hy_w), F32),
                   jax.ShapeDtypeStruct((T, N_HEADS * LANES), BF16),
                   jax.ShapeDtypeStruct((T, kv_w), BF16),
                   jax.ShapeDtypeStruct((T, kv_w), BF16),
                   jax.ShapeDtypeStruct((T, in_w - 3 * hy_w - att_w - 2 * kv_w), F32)),
        grid=(T // tm,),
        in_specs=[pl.BlockSpec((tm, D), lambda i: (i, 0)),
                  pl.BlockSpec((1, 6, D), lambda i: (b0 + i // nper, 0, 0)),
                  pl.BlockSpec((1, D), lambda i: (0, 0)),
                  pl.BlockSpec((D, in_w), lambda i: (0, 0)),
                  pl.BlockSpec((tm, LANES), lambda i: (i % nper, 0)),
                  pl.BlockSpec((tm, LANES), lambda i: (i % nper, 0)),
                  pl.BlockSpec((1, LANES), lambda i: (0, 0)),
                  pl.BlockSpec((1, LANES), lambda i: (0, 0)),
                  pl.BlockSpec((LANES, LANES), lambda i: (0, 0))],
        out_specs=(pl.BlockSpec((tm, 3 * hy_w), lambda i: (i, 0)),
                   pl.BlockSpec((tm, N_HEADS * LANES), lambda i: (i, 0)),
                   pl.BlockSpec((tm, kv_w), lambda i: (i, 0)),
                   pl.BlockSpec((tm, kv_w), lambda i: (i, 0)),
                   pl.BlockSpec((tm, in_w - 3 * hy_w - att_w - 2 * kv_w), lambda i: (i, 0))),
        compiler_params=_params("arbitrary"),
        name="inproj",
    )(x, mod_l, norm_g, w_in_bf, cos_t, sin_t, qg, kg, blk)


def _shortconv_kernel(u_ref, p_ref, n_ref, w_ref, b_ref, z_ref, ga_ref, gb_ref, *, nper, hy_w):
    i = pl.program_id(0)
    u = u_ref[...]
    tm = u.shape[0]
    first = (i % nper) == 0
    last = (i % nper) == nper - 1
    prev_row = jnp.where(first, 0.0, p_ref[7:8, :])
    next_row = jnp.where(last, 0.0, n_ref[0:1, :])
    row = lax.broadcasted_iota(I32, u.shape, 0)
    up = jnp.where(row == 0, prev_row, pltpu.roll(u, 1, 0))
    dn = jnp.where(row == tm - 1, next_row, pltpu.roll(u, tm - 1, 0))
    y = up * w_ref[0:1, :] + u * w_ref[1:2, :] + dn * w_ref[2:3, :] + b_ref[...]
    for r in range(tm // LANES):
        blk = y[r * LANES:(r + 1) * LANES, :]
        z_ref[r] = blk[:, 0:hy_w].T
        ga_ref[r] = blk[:, hy_w:2 * hy_w].T
        gb_ref[r] = blk[:, 2 * hy_w:3 * hy_w].T


def _shortconv(hy, L, conv_w, conv_b, hy_w):
    T, W = hy.shape
    tm = 512
    nper = L // tm
    r8 = tm // 8
    nb8 = T // 8
    kern = functools.partial(_shortconv_kernel, nper=nper, hy_w=hy_w)
    o = jax.ShapeDtypeStruct((T // LANES, hy_w, LANES), F32)
    ospec = pl.BlockSpec((tm // LANES, hy_w, LANES), lambda i: (i, 0, 0))
    return pl.pallas_call(
        kern,
        out_shape=(o, o, o),
        grid=(T // tm,),
        in_specs=[pl.BlockSpec((tm, W), lambda i: (i, 0)),
                  pl.BlockSpec((8, W), lambda i: (jnp.maximum(i * r8 - 1, 0), 0)),
                  pl.BlockSpec((8, W), lambda i: (jnp.minimum((i + 1) * r8, nb8 - 1), 0)),
                  pl.BlockSpec((3, W), lambda i: (0, 0)),
                  pl.BlockSpec((1, W), lambda i: (0, 0))],
        out_specs=(ospec, ospec, ospec),
        compiler_params=_params("arbitrary"),
        name="shortconv",
    )(hy, hy, hy, conv_w, conv_b.reshape(1, W))


def _filter_kernel(z_ref, tl_ref, w1_ref, b1_ref, w2_ref, b2_ref, w3_ref, b3_ref, fr_ref, woa_ref, wob_ref, dl_ref,
                   k_ref, ssq_ref, *, L):
    i = pl.program_id(0)
    z = z_ref[...]
    tr = z.shape[0]
    half = tr // 2
    fr = fr_ref[...]

    def lin(a, w_ref_, b_ref_):
        ah, al = _split(a)
        wh, wl = _split(w_ref_[...])
        return _dot3(ah, al, wh, wl) + b_ref_[...]

    zz = jnp.concatenate([z[:half], z[half:]], axis=1)
    h = jnp.sin(fr * lin(zz, w1_ref, b1_ref))
    h = jnp.sin(fr * lin(h, w2_ref, b2_ref))
    h = jnp.sin(fr * lin(h, w3_ref, b3_ref))
    hh, hl = _split(h)
    wts = (_split(woa_ref[0]), _split(wob_ref[0]))

    @pl.when(i == 0)
    def _():
        ssq_ref[...] = jnp.zeros_like(ssq_ref)

    lane = lax.broadcasted_iota(I32, (1, LANES), 1)
    per_half = half // LANES
    for r in range(tr // LANES):
        wh, wl = wts[r // per_half]
        rows = slice((r % per_half) * LANES, (r % per_half + 1) * LANES)
        kt = _dot_nt(wh, hh[rows]) + _dot_nt(wl, hh[rows]) + _dot_nt(wh, hl[rows])
        kt = kt * jnp.exp(-dl_ref[...] * tl_ref[r])
        kt = jnp.where((i * tr + r * LANES + lane) == L, 0.0, kt)
        k_ref[r] = kt
        ssq_ref[...] += kt * kt


def _filter_rows(L, p):
    hid = p['flt_w2'].shape[0]
    hy_w = p['hy_bias'].shape[1]
    cw = HY_ORDER * hy_w
    tr = 512
    N = 2 * L
    n1 = N // LANES
    t = jnp.linspace(0.0, 1.0, L, dtype=F32)[:, None]
    bands = (FILTER_EMB - 1) // 2
    f = jnp.linspace(1e-4, bands - 1, bands, dtype=F32)[None, :]
    w = (2.0 * math.pi / L) * jnp.arange(L, dtype=F32)[:, None]
    zf = jnp.concatenate([t, jnp.cos(f * w), -jnp.sin(f * w)], axis=-1)
    z = jnp.concatenate([zf, zf[L - 1:L], zf[:0:-1]], axis=0)
    tl = z[:, 0].reshape(n1, 1, LANES)
    z = jnp.pad(z, ((0, 0), (0, LANES - FILTER_EMB)))
    eye2 = jnp.eye(2, dtype=F32)
    w1 = jnp.concatenate([jnp.pad(p['flt_w1'], ((0, LANES - FILTER_EMB), (0, hid))),
                          jnp.pad(p['flt_w1'], ((0, LANES - FILTER_EMB), (hid, 0)))], axis=0)
    w2 = jnp.kron(eye2, p['flt_w2'])
    w3 = jnp.kron(eye2, p['flt_w3'])
    two = lambda v: jnp.tile(v, 2)[None]
    wo_t = p['flt_w_out'].reshape(hid, 2, cw).transpose(1, 2, 0)
    woa = jnp.pad(wo_t, ((0, 0), (0, 0), (0, hid)))
    wob = jnp.pad(wo_t, ((0, 0), (0, 0), (hid, 0)))
    max_decay = math.log(DECAY_TARGET) / FAST_DECAY_PCT
    min_decay = math.log(DECAY_TARGET) / SLOW_DECAY_PCT
    deltas = jnp.abs(jnp.linspace(min_decay, max_decay, hy_w, dtype=F32))
    dl = jnp.tile(deltas, HY_ORDER)[:, None]
    nblk = N // tr
    halfb = nblk // 2
    full = lambda i: (0, 0)
    h2 = 2 * hid
    return pl.pallas_call(
        functools.partial(_filter_kernel, L=L),
        out_shape=(jax.ShapeDtypeStruct((n1, cw, LANES), F32), jax.ShapeDtypeStruct((cw, LANES), F32)),
        grid=(nblk,),
        in_specs=[pl.BlockSpec((tr, LANES), lambda i: (i, 0)),
                  pl.BlockSpec((tr // LANES, 1, LANES), lambda i: (i, 0, 0)),
                  pl.BlockSpec((2 * LANES, h2), full), pl.BlockSpec((1, h2), full),
                  pl.BlockSpec((h2, h2), full), pl.BlockSpec((1, h2), full),
                  pl.BlockSpec((h2, h2), full), pl.BlockSpec((1, h2), full),
                  pl.BlockSpec((1, h2), full),
                  pl.BlockSpec((1, cw, h2), lambda i: (i // halfb, 0, 0)),
                  pl.BlockSpec((1, cw, h2), lambda i: (i // halfb, 0, 0)),
                  pl.BlockSpec((cw, 1), full)],
        out_specs=(pl.BlockSpec((tr // LANES, cw, LANES), lambda i: (i, 0, 0)), pl.BlockSpec((cw, LANES), full)),
        compiler_params=_params("arbitrary"),
        name="hyena_filter",
    )(z, tl, w1, two(p['flt_b1']), w2, two(p['flt_b2']), w3, two(p['flt_b3']), two(p['flt_freq']), woa, wob, dl)


def _fft_consts(L, planes):
    N = 2 * L
    n1 = N // FFT_N2
    n1h = n1 // 2
    k1 = np.arange(n1)[:, None]
    th = 2.0 * np.pi * k1 * np.arange(n1)[None, :] / n1
    c, s = np.cos(th), np.sin(th)
    ch, sh = c[:, :n1h], s[:, :n1h]
    if planes == 1:
        m1 = np.concatenate([ch, -sh], axis=0)
        m3 = np.concatenate([ch.T, -sh.T], axis=1) / N
    else:
        m1 = np.block([[ch, sh], [-sh, ch]])
        m3 = np.block([[ch.T, -sh.T], [sh.T, ch.T]]) / N
    m1_full = np.concatenate([c, -s], axis=0)
    ph = 2.0 * np.pi * k1 * np.arange(FFT_N2)[None, :] / N
    twr, twi = np.cos(ph), -np.sin(ph)
    a = 2.0 * np.pi * np.arange(FFT_N2)[:, None] * np.arange(FFT_N2)[None, :] / FFT_N2
    ca, sa = np.cos(a), np.sin(a)
    rt = np.block([[ca, -sa], [sa, ca]])
    rit = np.block([[ca, sa], [-sa, ca]])
    bf = lambda m: jnp.asarray(np.asarray(m, np.float32).astype(BF16))
    return dict(n1=n1, n1h=n1h, m1=bf(m1), m3=bf(m3), m1_full=bf(m1_full),
                twr=jnp.asarray(twr, F32), twi=jnp.asarray(twi, F32), rt=bf(rt), rit=bf(rit))


CONV_CH = 8


def _load_channels(ref, lead):
    return jnp.concatenate([ref[lead + (slice(None), c, slice(None))] for c in range(CONV_CH)], axis=1)


def _fwd_lane_dft(ar, ai, twr, twi, rt):
    pieces = []
    for c in range(CONV_CH):
        r = ar[:, c * LANES:(c + 1) * LANES]
        i = ai[:, c * LANES:(c + 1) * LANES]
        pieces.append(jnp.concatenate([r * twr - i * twi, r * twi + i * twr], axis=1))
    return _dot(jnp.concatenate(pieces, axis=0).astype(BF16), rt)


def _spectrum_kernel(k_ref, ssq_ref, m1_ref, twr_ref, twi_ref, rt_ref, o_ref):
    n1 = twr_ref.shape[0]
    a = _dot(m1_ref[...], _load_channels(k_ref, ()).astype(BF16))
    x = _fwd_lane_dft(a[:n1], a[n1:], twr_ref[...], twi_ref[...], rt_ref[...])
    scale = lax.rsqrt(jnp.sum(ssq_ref[...], axis=1, keepdims=True) + EPS)
    for c in range(CONV_CH):
        o_ref[0, :, c * 2 * LANES:(c + 1) * 2 * LANES] = x[c * n1:(c + 1) * n1] * scale[c:c + 1, :]


def _filter_spectrum(k3, ssq, fc, hy_w):
    n1 = fc['n1']
    ct = CONV_CH
    ntile = hy_w // ct
    full = lambda o, j: (0, 0)
    return pl.pallas_call(
        _spectrum_kernel,
        out_shape=jax.ShapeDtypeStruct((HY_ORDER, n1, hy_w * 2 * LANES), F32),
        grid=(HY_ORDER, ntile),
        in_specs=[pl.BlockSpec((n1, ct, LANES), lambda o, j: (0, o * ntile + j, 0)),
                  pl.BlockSpec((ct, LANES), lambda o, j: (o * ntile + j, 0)),
                  pl.BlockSpec((2 * n1, n1), full),
                  pl.BlockSpec((n1, LANES), full), pl.BlockSpec((n1, LANES), full),
                  pl.BlockSpec((2 * LANES, 2 * LANES), full)],
        out_specs=pl.BlockSpec((1, n1, ct * 2 * LANES), lambda o, j: (o, 0, j)),
        compiler_params=_params("arbitrary", "arbitrary"),
        name="filter_spectrum",
    )(k3, ssq, fc['m1_full'], fc['twr'], fc['twi'], fc['rt'])


def _longconv_kernel(z_ref, g_ref, kf_ref, b_ref, m1_ref, twr_ref, twi_ref, rt_ref, rit_ref, m3_ref, o_ref):
    ct = CONV_CH
    n1 = twr_ref.shape[0]
    planes, n1h = z_ref.shape[1], z_ref.shape[2]
    z = jnp.concatenate([_load_channels(z_ref, (0, p)) for p in range(planes)], axis=0)
    a = _dot(m1_ref[...], z.astype(BF16))
    twr = twr_ref[...]
    twi = twi_ref[...]
    x = _fwd_lane_dft(a[:n1], a[n1:], twr, twi, rt_ref[...])
    ys = []
    for c in range(ct):
        xc = x[c * n1:(c + 1) * n1]
        kc = kf_ref[0, :, c * 2 * LANES:(c + 1) * 2 * LANES]
        xr, xi = xc[:, :LANES], xc[:, LANES:]
        kr, ki = kc[:, :LANES], kc[:, LANES:]
        ys.append(jnp.concatenate([xr * kr - xi * ki, xr * ki + xi * kr], axis=1))
    bp = _dot(jnp.concatenate(ys, axis=0).astype(BF16), rit_ref[...])
    brs, bis = [], []
    for c in range(ct):
        bc = bp[c * n1:(c + 1) * n1]
        r, i = bc[:, :LANES], bc[:, LANES:]
        brs.append(r * twr + i * twi)
        bis.append(i * twr - r * twi)
    bst = jnp.concatenate([jnp.concatenate(brs, axis=1), jnp.concatenate(bis, axis=1)], axis=0)
    yt = _dot(m3_ref[...], bst.astype(BF16))
    for p in range(planes):
        rows = slice(p * n1h, (p + 1) * n1h)
        for c in range(ct):
            lanes = slice(c * LANES, (c + 1) * LANES)
            o_ref[0, p, :, c, :] = g_ref[0, p, :, c, :] * (yt[rows, lanes] + b_ref[0, 0:1, c:c + 1] * z[rows, lanes])


def _longconv(z5, g5, kf, order, bias_t, fc):
    npair, planes, n1h, ch, _ = z5.shape
    n1 = fc['n1']
    ct = CONV_CH
    ntile = ch // ct
    full = lambda p, j: (0, 0)
    m1r, m1c = 2 * n1, planes * n1h
    blk = pl.BlockSpec((1, planes, n1h, ct, LANES), lambda p, j: (p, 0, 0, j, 0))
    return pl.pallas_call(
        _longconv_kernel,
        out_shape=jax.ShapeDtypeStruct(z5.shape, F32),
        grid=(npair, ntile),
        in_specs=[blk, blk,
                  pl.BlockSpec((1, n1, ct * 2 * LANES), lambda p, j: (order, 0, j)),
                  pl.BlockSpec((1, 8, ct), lambda p, j: (order * ntile + j, 0, 0)),
                  pl.BlockSpec((m1r, m1c), full),
                  pl.BlockSpec((n1, LANES), full), pl.BlockSpec((n1, LANES), full),
                  pl.BlockSpec((2 * LANES, 2 * LANES), full), pl.BlockSpec((2 * LANES, 2 * LANES), full),
                  pl.BlockSpec((m1c, m1r), full)],
        out_specs=blk,
        compiler_params=_params("arbitrary", "arbitrary"),
        name="hyena_longconv",
    )(z5, g5, kf, bias_t, fc['m1'], fc['twr'], fc['twi'], fc['rt'], fc['rit'], fc['m3'])


def _attn_kernel(sink_ref, q_ref, kp_ref, km_ref, kn_ref, vp_ref, vm_ref, vn_ref, o_ref, *, L, tq):
    i = pl.program_id(1)
    t0 = i * tq
    tk = tq + 2 * WINDOW
    kk = jnp.concatenate([kp_ref[...], km_ref[...], kn_ref[...]], axis=0)
    vv = jnp.concatenate([vp_ref[...], vm_ref[...], vn_ref[...]], axis=0)
    qpos = t0 + lax.broadcasted_iota(I32, (tq, tk), 0)
    kpos = t0 - WINDOW + lax.broadcasted_iota(I32, (tq, tk), 1)
    valid = (jnp.abs(kpos - qpos) <= WINDOW) & (kpos >= 0) & (kpos < L)
    for h in range(N_HEADS):
        s = _dot_nt(q_ref[:, h * LANES:(h + 1) * LANES], kk)
        s = jnp.where(valid, s, -1e30)
        sink = sink_ref[h]
        m = jnp.maximum(jnp.max(s, axis=-1, keepdims=True), sink)
        p = jnp.exp(s - m)
        denom = jnp.sum(p, axis=-1, keepdims=True) + jnp.exp(sink - m)
        o_ref[:, h * LANES:(h + 1) * LANES] = _dot((p / denom).astype(BF16), vv).astype(BF16)


def _attention(q, k, v, sink, B, L):
    T = q.shape[0]
    tq = 256
    nq = L // tq
    r = tq // WINDOW
    nb = L // WINDOW
    kv_w = k.shape[1]
    kp = pl.BlockSpec((WINDOW, kv_w), lambda b, i: (b * nb + jnp.maximum(i * r - 1, 0), 0))
    km = pl.BlockSpec((tq, kv_w), lambda b, i: (b * nq + i, 0))
    kn = pl.BlockSpec((WINDOW, kv_w), lambda b, i: (b * nb + jnp.minimum((i + 1) * r, nb - 1), 0))
    qs = pl.BlockSpec((tq, N_HEADS * LANES), lambda b, i: (b * nq + i, 0))
    return pl.pallas_call(
        functools.partial(_attn_kernel, L=L, tq=tq),
        out_shape=jax.ShapeDtypeStruct((T, N_HEADS * LANES), BF16),
        grid=(B, nq),
        in_specs=[pl.BlockSpec(memory_space=pltpu.SMEM), qs, kp, km, kn, kp, km, kn],
        out_specs=qs,
        compiler_params=_params("arbitrary", "arbitrary"),
        name="band_attention",
    )(sink, q, k, k, k, v, v, v)


def _merge_kernel(x_ref, z_ref, a_ref, gs_ref, mod_ref, why_ref, wat_ref, wo_ref, g2_ref, rwh_ref, rwl_ref,
                  xo_ref, h2_ref, aff_ref):
    d = x_ref.shape[1]
    mod = mod_ref[0]
    z = jnp.concatenate([z_ref[r].T for r in range(z_ref.shape[0])], axis=0)
    y_hy = _dot(z.astype(BF16), why_ref[...])
    y_at = _dot(a_ref[...], wat_ref[...])
    gs = gs_ref[...]
    merged = gs[:, :d] * y_hy + gs[:, d:] * y_at
    x = x_ref[...] + mod[2:3] * _dot(merged.astype(BF16), wo_ref[...])
    xo_ref[...] = x
    ms = jnp.mean(x * x, axis=-1, keepdims=True)
    h = (x * lax.rsqrt(ms + EPS) * g2_ref[...]) * (1.0 + mod[4:5]) + mod[3:4]
    h2_ref[...] = h
    hh, hl = _split(h)
    rh, rl = rwh_ref[...], rwl_ref[...]
    logits = _dot_nt(rh, hh) + _dot_nt(rl, hh) + _dot_nt(rh, hl)
    m = jnp.max(logits, axis=0, keepdims=True)
    e = jnp.exp(logits - m)
    aff_ref[...] = e / jnp.sum(e, axis=0, keepdims=True)


def _merge(x, z, att, gs, mod_l, b0, L, w_hy, w_at_x, w_o, norm2_g, rw_hi, rw_lo):
    T, D = x.shape
    tm = 256
    nper = L // tm
    E = rw_hi.shape[0]
    full = lambda i: (0, 0)
    row = lambda w: pl.BlockSpec((tm, w), lambda i: (i, 0))
    return pl.pallas_call(
        _merge_kernel,
        out_shape=(jax.ShapeDtypeStruct((T, D), F32), jax.ShapeDtypeStruct((T, D), F32),
                   jax.ShapeDtypeStruct((E, T), F32)),
        grid=(T // tm,),
        in_specs=[row(D), pl.BlockSpec((tm // LANES,) + z.shape[1:], lambda i: (i, 0, 0)), row(att.shape[1]), row(gs.shape[1]),
                  pl.BlockSpec((1, 6, D), lambda i: (b0 + i // nper, 0, 0)),
                  pl.BlockSpec(w_hy.shape, full), pl.BlockSpec(w_at_x.shape, full), pl.BlockSpec(w_o.shape, full),
                  pl.BlockSpec((1, D), full), pl.BlockSpec((E, D), full), pl.BlockSpec((E, D), full)],
        out_specs=(row(D), row(D), pl.BlockSpec((E, tm), lambda i: (0, i))),
        compiler_params=_params("arbitrary"),
        name="merge_router",
    )(x, z, att, gs, mod_l, w_hy, w_at_x, w_o, norm2_g, rw_hi, rw_lo)


def _cumsum_rows(mb, u, ones, ls):
    rc = _dot(mb, u)
    totb = _dot(mb, ones)
    offs = _dot(ls, totb.astype(BF16))
    return rc, totb, offs


def _topk_kernel(aff_ref, u_ref, ut_ref, ones_ref, ls_ref, idx_ref, gate_ref, dst_ref, cum_ref, cnt_ref,
                 thr_ref, sel_ref, rank_ref, *, cap):
    E, R, _ = aff_ref.shape
    bits = pltpu.bitcast(aff_ref[...], I32)

    def search(i, cur):
        cand = cur | jnp.left_shift(jnp.int32(1), 30 - i)
        ge = jnp.where(bits >= cand, 1.0, 0.0)
        c = jnp.sum(jnp.sum(ge, axis=2, keepdims=True), axis=1, keepdims=True)
        return jnp.where(c >= cap, cand, cur)

    thr = lax.fori_loop(0, 31, search, jnp.zeros((E, 1, 1), I32))
    thr_ref[...] = jnp.broadcast_to(thr, thr_ref.shape)
    u = u_ref[...]
    ut = ut_ref[...]
    ones = ones_ref[...]
    ls = ls_ref[...]

    def select(e, carry):
        cnt_run, cum = carry
        b = pltpu.bitcast(aff_ref[e], I32)
        t = thr_ref[e][0:1, :]
        gt = b > t
        eq = b == t
        ngt = jnp.sum(jnp.sum(jnp.where(gt, 1.0, 0.0), axis=1, keepdims=True), axis=0, keepdims=True)
        need = cap - ngt
        erc, _, eoffs = _cumsum_rows(jnp.where(eq, 1.0, 0.0).astype(BF16), u, ones, ls)
        sel = jnp.where(gt | (eq & ((erc + eoffs) <= need)), 1.0, 0.0)
        rc, _, offs = _cumsum_rows(sel.astype(BF16), u, ones, ls)
        sel_ref[e] = sel
        rank_ref[e] = cnt_run
        return cnt_run + sel, cum + (rc + offs - sel)

    zero = jnp.zeros((R, LANES), F32)
    cnt, cum = lax.fori_loop(0, E, select, (zero, zero))
    cnt_ref[...] = cnt.astype(I32)
    cum_ref[...] = cum.astype(I32)
    s_row = lax.broadcasted_iota(I32, (1, cap), 1).astype(F32)
    r_col = lax.broadcasted_iota(I32, (R, 1), 0).astype(F32)
    c_col = lax.broadcasted_iota(I32, (LANES, 1), 0).astype(F32)

    def pick(vt_pieces, ohb, hit):
        acc = _dot(vt_pieces[0], ohb)
        for piece in vt_pieces[1:]:
            acc = acc + _dot(piece, ohb)
        return jnp.sum(jnp.where(hit, acc, 0.0), axis=0, keepdims=True)

    def compact(e, carry):
        sel = sel_ref[e]
        _, totb, offs = _cumsum_rows(sel.astype(BF16), u, ones, ls)
        offs_col = offs[:, 0:1]
        end_col = offs_col + totb[:, 0:1]
        oht = (offs_col <= s_row) & (s_row < end_col)
        row_s = jnp.sum(jnp.where(oht, r_col, 0.0), axis=0, keepdims=True)
        offs_s = jnp.sum(jnp.where(oht, offs_col, 0.0), axis=0, keepdims=True)
        ohb = jnp.where(oht, 1.0, 0.0).astype(BF16)
        rct = _dot(ut, sel.T.astype(BF16))
        rcs = _dot(rct.astype(BF16), ohb)
        col_s = jnp.sum(jnp.where(rcs <= (s_row - offs_s), 1.0, 0.0), axis=0, keepdims=True)
        hit = c_col == col_s
        idx_ref[pl.ds(e, 1), :] = (row_s * LANES + col_s).astype(I32)
        at = aff_ref[e].T
        p0 = at.astype(BF16)
        r1 = at - p0.astype(F32)
        p1 = r1.astype(BF16)
        p2 = (r1 - p1.astype(F32)).astype(BF16)
        gate_ref[pl.ds(e, 1), :] = pick((p0, p1, p2), ohb, hit)
        d = (cum + rank_ref[e]).T
        d_hi = (d * (1.0 / 256.0)).astype(I32).astype(F32)
        d_lo = d - 256.0 * d_hi
        dst = 256.0 * pick((d_hi.astype(BF16),), ohb, hit) + pick((d_lo.astype(BF16),), ohb, hit)
        dst_ref[pl.ds(e, 1), :] = dst.astype(I32)
        return carry

    lax.fori_loop(0, E, compact, 0)


def _topk(aff_t, cap):
    E, T = aff_t.shape
    R = T // LANES
    u = jnp.asarray(np.triu(np.ones((LANES, LANES))), BF16)
    ut = jnp.asarray(np.tril(np.ones((LANES, LANES))), BF16)
    ones = jnp.ones((LANES, LANES), BF16)
    ls = jnp.asarray(np.tril(np.ones((R, R)), -1), BF16)
    z2 = lambda: (0, 0)
    z3 = lambda: (0, 0, 0)
    slot = jax.ShapeDtypeStruct((E, cap), I32)
    tok = jax.ShapeDtypeStruct((R, LANES), I32)
    idx, gate, dst, cum, cnt = pl.pallas_call(
        functools.partial(_topk_kernel, cap=cap),
        out_shape=(slot, jax.ShapeDtypeStruct((E, cap), F32), slot, tok, tok),
        in_specs=[pl.BlockSpec((E, R, LANES), z3), pl.BlockSpec((LANES, LANES), z2), pl.BlockSpec((LANES, LANES), z2),
                  pl.BlockSpec((LANES, LANES), z2), pl.BlockSpec((R, R), z2)],
        out_specs=(pl.BlockSpec((E, cap), z2), pl.BlockSpec((E, cap), z2), pl.BlockSpec((E, cap), z2),
                   pl.BlockSpec((R, LANES), z2), pl.BlockSpec((R, LANES), z2)),
        scratch_shapes=[pltpu.VMEM((E, 8, LANES), I32), pltpu.VMEM((E, R, LANES), F32),
                        pltpu.VMEM((E, R, LANES), F32)],
        compiler_params=pltpu.CompilerParams(vmem_limit_bytes=VMEM_LIMIT),
        name="expert_topk",
    )(aff_t.reshape(E, R, LANES), u, ut, ones, ls)
    return idx, gate, dst, cum.reshape(T), cnt.reshape(T)


ROW_CHUNKS = 4
FF_TILE = 256


def _expert_kernel(src_ref, dst_ref, h_ref, wg_ref, wu_ref, wd_ref, gate_ref, out_ref,
                   xbuf, xb, acc, gsem, ssem, *, cap, per):
    e = pl.program_id(0)
    f = pl.program_id(1)
    ne = pl.num_programs(0)
    nf = pl.num_programs(1)
    slot = e % 2
    other = 1 - slot
    sched = nf * per

    def gather_row(entry, r):
        pltpu.make_async_copy(h_ref.at[pl.ds(src_ref[entry], 1), :], xbuf.at[pl.ds(r, 1), :], gsem).start()

    def scatter_row(src_slot, entry, r):
        pltpu.make_async_copy(acc.at[src_slot, pl.ds(r, 1), :], out_ref.at[pl.ds(dst_ref[entry], 1), :],
                              ssem.at[src_slot]).start()

    def schedule_done(sem):
        pltpu.make_async_copy(h_ref.at[pl.ds(0, sched), :], xbuf, sem).wait()

    @pl.when((e == 0) & (f == 0))
    def _():
        def body(s, c):
            gather_row(s, s)
            return c

        lax.fori_loop(0, sched, body, 0, unroll=8)
        acc[...] = jnp.zeros(acc.shape, F32)

    @pl.when((e > 0) & (f == 0))
    def _():
        schedule_done(ssem.at[slot])
        acc[slot, pl.ds(0, cap), :] = jnp.zeros((cap, acc.shape[2]), F32)

    @pl.when(f == 0)
    def _():
        schedule_done(gsem)
        xb[...] = xbuf[pl.ds(0, cap), :].astype(BF16)

    wg = wg_ref[0, 0].astype(BF16)
    wu = wu_ref[0, 0].astype(BF16)
    wd = wd_ref[0, 0].astype(BF16)
    gbase = (jnp.minimum(e + 1, ne - 1) * nf + f) * per
    sbase = (e * nf + f) * per
    row0 = f * per
    share = per // ROW_CHUNKS
    chunk = cap // ROW_CHUNKS
    for c in range(ROW_CHUNKS):
        rows = pl.ds(c * chunk, chunk)
        x = xb[rows, :]
        g = _dot(x, wg)
        u = _dot(x, wu)
        hid = (g * jax.nn.sigmoid(g) * u).astype(BF16)
        acc[slot, rows, :] += _dot(hid, wd)
        for s in range(c * share, (c + 1) * share):
            gather_row(gbase + s, row0 + s)
            scatter_row(other, sbase + s, row0 + s)

    @pl.when(f == nf - 1)
    def _():
        acc[slot, pl.ds(0, cap), :] = gate_ref[0] * acc[slot, pl.ds(0, cap), :]

    @pl.when((f == nf - 1) & (e == ne - 1))
    def _():
        def body(s, c):
            scatter_row(slot, ne * sched + s, s)
            return c

        lax.fori_loop(0, sched, body, 0, unroll=8)
        schedule_done(gsem)
        schedule_done(ssem.at[other])
        schedule_done(ssem.at[slot])


def _experts(h2, idx, dst, w_gate, w_up, w_down, layer, gate_col, cap):
    _, E, D, F = w_gate.shape
    ft = FF_TILE
    nf = F // ft
    per = -(-cap // (nf * 8 * ROW_CHUNKS)) * (8 * ROW_CHUNKS)
    sched = nf * per
    nr = E * cap
    assert E >= 2 and cap % ROW_CHUNKS == 0 and sched >= cap
    rows = jnp.arange(sched, dtype=I32)
    real = rows < cap
    rc = jnp.minimum(rows, cap - 1)
    src_tab = idx[:, rc]
    odd = (jnp.arange(E + 1, dtype=I32)[:, None] % 2) == 1
    spare = jnp.where(odd, nr + sched + (rows[None, :] - cap), nr + rows[None, :])
    dst_prev = jnp.concatenate([jnp.zeros((1, sched), I32), dst[:, rc]], axis=0)
    dst_tab = jnp.where(real[None, :] & (jnp.arange(E + 1)[:, None] > 0), dst_prev, spare)
    return pl.pallas_call(
        functools.partial(_expert_kernel, cap=cap, per=per),
        out_shape=jax.ShapeDtypeStruct((nr + 2 * sched - cap, D), F32),
        grid_spec=pltpu.PrefetchScalarGridSpec(
            num_scalar_prefetch=2,
            grid=(E, nf),
            in_specs=[pl.BlockSpec(memory_space=pl.ANY),
                      pl.BlockSpec((1, 1, D, ft), lambda e, f, i, d: (layer, e, 0, f)),
                      pl.BlockSpec((1, 1, D, ft), lambda e, f, i, d: (layer, e, 0, f)),
                      pl.BlockSpec((1, 1, ft, D), lambda e, f, i, d: (layer, e, f, 0)),
                      pl.BlockSpec((1, cap, 1), lambda e, f, i, d: (e, 0, 0))],
            out_specs=pl.BlockSpec(memory_space=pl.ANY),
            scratch_shapes=[pltpu.VMEM((sched, D), F32), pltpu.VMEM((cap, D), BF16), pltpu.VMEM((2, sched, D), F32),
                            pltpu.SemaphoreType.DMA, pltpu.SemaphoreType.DMA((2,))]),
        compiler_params=_params("arbitrary", "arbitrary"),
        name="expert_ffn",
    )(src_tab.reshape(-1), dst_tab.reshape(-1), h2, w_gate, w_up, w_down, gate_col)


COMBINE_ROWS = 512
COMBINE_TOKENS = 256


def _combine_kernel(tile_ref, blk_ref, flag_ref, c_ref, x_ref, mod_ref, tab_ref, eye_ref, o_ref):
    j = pl.program_id(0)
    flag = flag_ref[j]
    first = (flag & 1) != 0
    last = (flag & 2) != 0
    skip = (flag & 4) != 0
    rb = c_ref.shape[0]
    cols = _dot_nt(eye_ref[...], tab_ref[0].astype(BF16))
    start = cols[:, 0:1] * 256.0 + cols[:, 1:2]
    count = cols[:, 2:3]
    rabs = (blk_ref[j] * rb + lax.broadcasted_iota(I32, (1, rb), 1)).astype(F32)
    seg = jnp.where((start <= rabs) & (rabs < start + count), 1.0, 0.0).astype(BF16)
    hi, lo = _split(c_ref[...])
    part = _dot(seg, hi) + _dot(seg, lo)

    @pl.when(first)
    def _():
        o_ref[...] = part

    @pl.when(jnp.logical_not(first | skip))
    def _():
        o_ref[...] += part

    @pl.when(last & jnp.logical_not(skip))
    def _():
        o_ref[...] = x_ref[...] + mod_ref[0][5:6] * o_ref[...]


def _combine(x, mod_l, b0, L, contrib, cum, cnt):
    T, D = x.shape
    tt = COMBINE_TOKENS
    R = T // tt
    rb = COMBINE_ROWS
    nr = CAPACITY_FACTOR * T
    nb = nr // rb
    nper = L // tt
    nitem = R + nb
    lo = cum[::tt]
    hi = jnp.concatenate([lo[1:], jnp.full((1,), nr, I32)])
    b_lo = jnp.minimum(lo // rb, nb - 1)
    b_hi = jnp.minimum(jnp.maximum((hi - 1) // rb, b_lo), nb - 1)
    n = b_hi - b_lo + 1
    cn = jnp.cumsum(n)
    start = cn - n
    j = jnp.arange(nitem, dtype=I32)
    valid = j < cn[-1]
    ti = jnp.minimum(jnp.searchsorted(cn, j, side='right').astype(I32), R - 1)
    blk = jnp.where(valid, b_lo[ti] + (j - start[ti]), b_hi[R - 1])
    flag = jnp.where(valid, (j == start[ti]) * 1 + (j == cn[ti] - 1) * 2, 4).astype(I32)
    tab = jnp.stack([(cum >> 8).astype(F32), (cum & 255).astype(F32), cnt.astype(F32)], axis=0)
    tab = jnp.concatenate([tab, jnp.zeros((5, T), F32)], axis=0).reshape(8, R, tt).transpose(1, 0, 2)
    eye = jnp.eye(tt, dtype=BF16)
    return pl.pallas_call(
        _combine_kernel,
        out_shape=jax.ShapeDtypeStruct((T, D), F32),
        grid_spec=pltpu.PrefetchScalarGridSpec(
            num_scalar_prefetch=3,
            grid=(nitem,),
            in_specs=[pl.BlockSpec((rb, D), lambda j, t, b, f: (b[j], 0)),
                      pl.BlockSpec((tt, D), lambda j, t, b, f: (t[j], 0)),
                      pl.BlockSpec((1, 6, D), lambda j, t, b, f: (b0 + t[j] // nper, 0, 0)),
                      pl.BlockSpec((1, 8, tt), lambda j, t, b, f: (t[j], 0, 0)),
                      pl.BlockSpec((tt, tt), lambda j, t, b, f: (0, 0))],
            out_specs=pl.BlockSpec((tt, D), lambda j, t, b, f: (t[j], 0))),
        compiler_params=_params("arbitrary"),
        name="moe_combine",
    )(ti, blk, flag, contrib, x, mod_l, tab, eye)


def _layer(x, B, L, b0, mod_l, p, kf, fc, tabs):
    hy_w = p['hy_bias'].shape[1]
    att_w = N_HEADS * HEAD_DIM
    kv_w = N_KV_HEADS * HEAD_DIM
    planes = 2 if B % 2 == 0 else 1
    hy, q, k, v, gs = _inproj(x, mod_l, b0, L, p['norm1_g'][None], p['w_in_bf'], tabs['cos'], tabs['sin'],
                              tabs['qg'], tabs['kg'], tabs['blk'], hy_w, att_w, kv_w)
    z, ga, gb = _shortconv(hy, L, p['hy_conv_w'], p['hy_conv_b'], hy_w)
    shape5 = (B // planes, planes, L // LANES, hy_w, LANES)
    z = z.reshape(shape5)
    gates = (ga.reshape(shape5), gb.reshape(shape5))
    for o in range(HY_ORDER):
        z = _longconv(z, gates[o], kf, o, p['bias_t'], fc)
    z = z.reshape(B * L // LANES, hy_w, LANES)
    att = _attention(q, k, v, p['sink_logit'], B, L)
    x, h2, aff_t = _merge(x, z, att, gs, mod_l, b0, L, p['w_hy_bf'], p['w_at_x'], p['w_o_bf'], p['norm2_g'][None],
                          p['rw_hi'], p['rw_lo'])
    T = x.shape[0]
    cap = CAPACITY_FACTOR * T // N_EXPERTS
    idx, gate, dst, cum, cnt = _topk(aff_t, cap)
    contrib = _experts(h2, idx, dst, p['exp_w_gate'], p['exp_w_up'], p['exp_w_down'], p['layer'],
                       gate.reshape(N_EXPERTS, cap, 1), cap)
    return _combine(x, mod_l, b0, L, contrib, cum, cnt)


def _rope_tables(L):
    inv = 1.0 / (ROPE_THETA ** (jnp.arange(0, HEAD_DIM, 2, dtype=F32) / HEAD_DIM))
    ang = jnp.arange(L, dtype=F32)[:, None] * inv[None, :]
    cos, sin = jnp.cos(ang), jnp.sin(ang)
    cos_t = jnp.tile(cos, (1, LANES // (HEAD_DIM // 2)))
    sin_t = jnp.tile(jnp.concatenate([-sin, sin], axis=1), (1, LANES // HEAD_DIM))
    return cos_t, sin_t


def _filter_spectrum_for(L, p, fc):
    hy_w = p['hy_bias'].shape[1]
    k3, ssq = _filter_rows(L, p)
    return _filter_spectrum(k3, ssq, fc, hy_w)


def kernel(x_prompt, x_sample, c_prompt, c_sample, ada_w, ada_b, norm1_g, norm2_g, w_in, hy_conv_w, hy_conv_b,
           flt_w1, flt_b1, flt_w2, flt_b2, flt_w3, flt_b3, flt_freq, flt_w_out, hy_bias, q_norm_g, k_norm_g,
           sink_logit, w_hy_proj, w_at_proj, w_o, router_w, exp_w_gate, exp_w_up, exp_w_down):
    depth = ada_w.shape[0]
    Bp, Lp, D = x_prompt.shape
    Bs, Ls, _ = x_sample.shape
    nb = Bp + Bs
    c_all = jnp.concatenate([c_prompt, c_sample, jnp.zeros(((-nb) % 8, D), F32)], axis=0)
    mod = _adaln(c_all, ada_w, ada_b).reshape(depth, c_all.shape[0], 6, D)
    hy_w = hy_bias.shape[2]
    att_w = N_HEADS * HEAD_DIM
    groups = [dict(B=Bp, L=Lp, b0=0, x=x_prompt.reshape(Bp * Lp, D)),
              dict(B=Bs, L=Ls, b0=Bp, x=x_sample.reshape(Bs * Ls, D))]
    blk = jnp.asarray(np.kron(np.eye(LANES // HEAD_DIM), np.ones((HEAD_DIM, HEAD_DIM))), BF16)
    for g in groups:
        g['fc'] = _fft_consts(g['L'], 2 if g['B'] % 2 == 0 else 1)
        cos_t, sin_t = _rope_tables(g['L'])
        g['tabs'] = dict(cos=cos_t, sin=sin_t, blk=blk)
    heads_per_kv = N_HEADS // N_KV_HEADS
    for l in range(depth):
        w_at_x = jnp.zeros((N_HEADS, LANES, D), BF16)
        w_at_h = w_at_proj[l].astype(BF16).reshape(N_HEADS, HEAD_DIM, D)
        for h in range(N_HEADS):
            off = (h // heads_per_kv) * HEAD_DIM
            w_at_x = w_at_x.at[h, off:off + HEAD_DIM].set(w_at_h[h])
        rw_hi, rw_lo = _split(router_w[l].T)
        p = dict(norm1_g=norm1_g[l], norm2_g=norm2_g[l], w_in_bf=w_in[l].astype(BF16),
                 hy_conv_w=hy_conv_w[l], hy_conv_b=hy_conv_b[l],
                 flt_w1=flt_w1[l], flt_b1=flt_b1[l], flt_w2=flt_w2[l], flt_b2=flt_b2[l], flt_w3=flt_w3[l],
                 flt_b3=flt_b3[l], flt_freq=flt_freq[l], flt_w_out=flt_w_out[l], hy_bias=hy_bias[l],
                 bias_t=jnp.broadcast_to(hy_bias[l].reshape(-1, 1, CONV_CH), (HY_ORDER * hy_w // CONV_CH, 8, CONV_CH)),
                 sink_logit=sink_logit[l], w_hy_bf=w_hy_proj[l].astype(BF16),
                 w_at_x=w_at_x.reshape(N_HEADS * LANES, D), w_o_bf=w_o[l].astype(BF16),
                 rw_hi=rw_hi, rw_lo=rw_lo,
                 layer=l, exp_w_gate=exp_w_gate, exp_w_up=exp_w_up, exp_w_down=exp_w_down)
        qg = jnp.tile(q_norm_g[l], LANES // HEAD_DIM)[None]
        kg = jnp.tile(k_norm_g[l], LANES // HEAD_DIM)[None]
        for g in groups:
            kf = _filter_spectrum_for(g['L'], p, g['fc'])
            tabs = dict(g['tabs'], qg=qg, kg=kg)
            g['x'] = _layer(g['x'], g['B'], g['L'], g['b0'], mod[l], p, kf, g['fc'], tabs)
    return (groups[0]['x'].reshape(Bp, Lp, D), groups[1]['x'].reshape(Bs, Ls, D))
```

```python
import functools
import math

import numpy as np
import jax
import jax.numpy as jnp
from jax import lax
from jax.experimental import pallas as pl
from jax.experimental.pallas import tpu as pltpu

F32 = jnp.float32
BF16 = jnp.bfloat16
I32 = jnp.int32

EPS = 1e-6
HEAD_DIM = 64
N_HEADS = 8
N_KV_HEADS = 2
WINDOW = 128
ROPE_THETA = 10000.0
HY_ORDER = 2
FILTER_EMB = 33
DECAY_TARGET = 0.01
FAST_DECAY_PCT = 0.3
SLOW_DECAY_PCT = 1.5
N_EXPERTS = 16
CAPACITY_FACTOR = 2
LANES = 128
FFT_N2 = LANES
VMEM_LIMIT = 52 << 20


def _params(*sem):
    return pltpu.CompilerParams(dimension_semantics=sem, vmem_limit_bytes=VMEM_LIMIT)


def _dot(a, b):
    return jnp.dot(a, b, preferred_element_type=F32)


def _dot_nt(a, b):
    return lax.dot_general(a, b, (((1,), (1,)), ((), ())), preferred_element_type=F32)


def _split(x):
    hi = x.astype(BF16)
    lo = (x - hi.astype(F32)).astype(BF16)
    return hi, lo


def _dot3(a_hi, a_lo, b_hi, b_lo):
    return _dot(a_hi, b_hi) + _dot(a_lo, b_hi) + _dot(a_hi, b_lo)


def _ada_kernel(c_ref, w_ref, b_ref, o_ref):
    c = c_ref[...]
    s = c * jax.nn.sigmoid(c)
    o_ref[0] = _dot(s.astype(BF16), w_ref[0].astype(BF16)) + b_ref[0]


def _adaln(c_all, ada_w, ada_b):
    depth, d, n = ada_w.shape
    tn = n // 4
    return pl.pallas_call(
        _ada_kernel,
        out_shape=jax.ShapeDtypeStruct((depth, c_all.shape[0], n), F32),
        grid=(depth, n // tn),
        in_specs=[pl.BlockSpec(c_all.shape, lambda l, j: (0, 0)),
                  pl.BlockSpec((1, d, tn), lambda l, j: (l, 0, j)),
                  pl.BlockSpec((1, 1, tn), lambda l, j: (l, 0, j))],
        out_specs=pl.BlockSpec((1, c_all.shape[0], tn), lambda l, j: (l, 0, j)),
        compiler_params=_params("arbitrary", "arbitrary"),
        name="adaln",
    )(c_all, ada_w, ada_b.reshape(depth, 1, n))


def _head_norm_rope(x, gain, cos, sin, blk_hi):
    sq = x * x
    hi, lo = _split(sq)
    ssum = _dot(hi, blk_hi) + _dot(lo, blk_hi)
    xn = x * lax.rsqrt(ssum * (1.0 / HEAD_DIM) + EPS) * gain
    lane = lax.broadcasted_iota(I32, x.shape, 1)
    first_half = (lane % HEAD_DIM) < (HEAD_DIM // 2)
    rot = jnp.where(first_half, pltpu.roll(xn, LANES - HEAD_DIM // 2, 1), pltpu.roll(xn, HEAD_DIM // 2, 1))
    return xn * cos + rot * sin


def _inproj_kernel(x_ref, xp_ref, xn_ref, mod_ref, g_ref, w_ref, cw_ref, cb_ref, cos_ref, sin_ref, qg_ref, kg_ref, blk_ref,
                   z_ref, ga_ref, gb_ref, q_ref, k_ref, v_ref, gs_ref, *, hy_w, att_w, kv_w, nper):
    i = pl.program_id(0)
    tm = x_ref.shape[0]
    halo = xp_ref.shape[0]
    x = jnp.concatenate([xp_ref[...], x_ref[...], xn_ref[...]], axis=0)
    mod = mod_ref[0]
    ms = jnp.mean(x * x, axis=-1, keepdims=True)
    h = (x * lax.rsqrt(ms + EPS) * g_ref[...]) * (1.0 + mod[1:2]) + mod[0:1]
    hb_all = h.astype(BF16)
    hb = hb_all[halo:halo + tm]
    s0 = 3 * hy_w
    s1 = s0 + att_w
    s2 = s1 + kv_w
    s3 = s2 + kv_w
    hy = _dot(hb_all, w_ref[:, 0:s0])
    n_all = tm + 2 * halo
    row = lax.broadcasted_iota(I32, (tm, 1), 0)
    first = (i % nper) == 0
    last = (i % nper) == nper - 1
    up = jnp.where((row == 0) & first, 0.0, pltpu.roll(hy, 1, 0)[halo:halo + tm])
    dn = jnp.where((row == tm - 1) & last, 0.0, pltpu.roll(hy, n_all - 1, 0)[halo:halo + tm])
    y = up * cw_ref[0:1, :] + hy[halo:halo + tm] * cw_ref[1:2, :] + dn * cw_ref[2:3, :] + cb_ref[...]
    for r in range(tm // LANES):
        blk_y = y[r * LANES:(r + 1) * LANES, :]
        z_ref[r] = blk_y[:, 0:hy_w].T
        ga_ref[r] = blk_y[:, hy_w:2 * hy_w].T
        gb_ref[r] = blk_y[:, 2 * hy_w:3 * hy_w].T
    q = _dot(hb, w_ref[:, s0:s1])
    k = _dot(hb, w_ref[:, s1:s2])
    v_ref[...] = _dot(hb, w_ref[:, s2:s3]).astype(BF16)
    gs_ref[...] = jax.nn.sigmoid(_dot(hb, w_ref[:, s3:]))
    cos = cos_ref[...]
    sin = sin_ref[...]
    blk = blk_ref[...]
    k_ref[...] = _head_norm_rope(k, kg_ref[...], cos, sin, blk).astype(BF16)
    lane = lax.broadcasted_iota(I32, cos.shape, 1)
    low = lane < HEAD_DIM
    heads_per_kv = N_HEADS // N_KV_HEADS
    for ci in range(att_w // LANES):
        qr = _head_norm_rope(q[:, ci * LANES:(ci + 1) * LANES], qg_ref[...], cos, sin, blk) * (HEAD_DIM ** -0.5)
        qs = pltpu.roll(qr, HEAD_DIM, 1)
        kvh = (2 * ci) // heads_per_kv
        if kvh == 0:
            h0 = jnp.where(low, qr, 0.0)
            h1 = jnp.where(low, qs, 0.0)
        else:
            h0 = jnp.where(low, 0.0, qs)
            h1 = jnp.where(low, 0.0, qr)
        q_ref[:, (2 * ci) * LANES:(2 * ci + 1) * LANES] = h0.astype(BF16)
        q_ref[:, (2 * ci + 1) * LANES:(2 * ci + 2) * LANES] = h1.astype(BF16)


def _inproj(x, mod_l, b0, L, norm_g, w_in_bf, conv_w, conv_b, cos_t, sin_t, qg, kg, blk, hy_w, att_w, kv_w):
    T, D = x.shape
    tm = 512
    halo = 8
    nper = L // tm
    r8 = tm // halo
    nb8 = T // halo
    in_w = w_in_bf.shape[1]
    kern = functools.partial(_inproj_kernel, hy_w=hy_w, att_w=att_w, kv_w=kv_w, nper=nper)
    stream = jax.ShapeDtypeStruct((T // LANES, hy_w, LANES), F32)
    sspec = pl.BlockSpec((tm // LANES, hy_w, LANES), lambda i: (i, 0, 0))
    full = lambda i: (0, 0)
    return pl.pallas_call(
        kern,
        out_shape=(stream, stream, stream,
                   jax.ShapeDtypeStruct((T, N_HEADS * LANES), BF16),
                   jax.ShapeDtypeStruct((T, kv_w), BF16),
                   jax.ShapeDtypeStruct((T, kv_w), BF16),
                   jax.ShapeDtypeStruct((T, in_w - 3 * hy_w - att_w - 2 * kv_w), F32)),
        grid=(T // tm,),
        in_specs=[pl.BlockSpec((tm, D), lambda i: (i, 0)),
                  pl.BlockSpec((halo, D), lambda i: (jnp.maximum(i * r8 - 1, 0), 0)),
                  pl.BlockSpec((halo, D), lambda i: (jnp.minimum((i + 1) * r8, nb8 - 1), 0)),
                  pl.BlockSpec((1, 6, D), lambda i: (b0 + i // nper, 0, 0)),
                  pl.BlockSpec((1, D), full),
                  pl.BlockSpec((D, in_w), full),
                  pl.BlockSpec((3, 3 * hy_w), full),
                  pl.BlockSpec((1, 3 * hy_w), full),
                  pl.BlockSpec((tm, LANES), lambda i: (i % nper, 0)),
                  pl.BlockSpec((tm, LANES), lambda i: (i % nper, 0)),
                  pl.BlockSpec((1, LANES), full),
                  pl.BlockSpec((1, LANES), full),
                  pl.BlockSpec((LANES, LANES), full)],
        out_specs=(sspec, sspec, sspec,
                   pl.BlockSpec((tm, N_HEADS * LANES), lambda i: (i, 0)),
                   pl.BlockSpec((tm, kv_w), lambda i: (i, 0)),
                   pl.BlockSpec((tm, kv_w), lambda i: (i, 0)),
                   pl.BlockSpec((tm, in_w - 3 * hy_w - att_w - 2 * kv_w), lambda i: (i, 0))),
        compiler_params=_params("arbitrary"),
        name="inproj",
    )(x, x, x, mod_l, norm_g, w_in_bf, conv_w, conv_b.reshape(1, -1), cos_t, sin_t, qg, kg, blk)


def _filter_kernel(z_ref, tl_ref, w1_ref, b1_ref, w2_ref, b2_ref, w3_ref, b3_ref, fr_ref, woa_ref, wob_ref, dl_ref,
                   k_ref, ssq_ref, *, L):
    i = pl.program_id(0)
    z = z_ref[...]
    tr = z.shape[0]
    half = tr // 2
    fr = fr_ref[...]

    def lin(a, w_ref_, b_ref_):
        ah, al = _split(a)
        wh, wl = _split(w_ref_[...])
        return _dot3(ah, al, wh, wl) + b_ref_[...]

    zz = jnp.concatenate([z[:half], z[half:]], axis=1)
    h = jnp.sin(fr * lin(zz, w1_ref, b1_ref))
    h = jnp.sin(fr * lin(h, w2_ref, b2_ref))
    h = jnp.sin(fr * lin(h, w3_ref, b3_ref))
    hh, hl = _split(h)
    wts = (_split(woa_ref[0]), _split(wob_ref[0]))

    @pl.when(i == 0)
    def _():
        ssq_ref[...] = jnp.zeros_like(ssq_ref)

    lane = lax.broadcasted_iota(I32, (1, LANES), 1)
    per_half = half // LANES
    for r in range(tr // LANES):
        wh, wl = wts[r // per_half]
        rows = slice((r % per_half) * LANES, (r % per_half + 1) * LANES)
        kt = _dot_nt(wh, hh[rows]) + _dot_nt(wl, hh[rows]) + _dot_nt(wh, hl[rows])
        kt = kt * jnp.exp(-dl_ref[...] * tl_ref[r])
        kt = jnp.where((i * tr + r * LANES + lane) == L, 0.0, kt)
        k_ref[r] = kt
        ssq_ref[...] += kt * kt


def _filter_rows(L, p):
    hid = p['flt_w2'].shape[0]
    hy_w = p['hy_bias'].shape[1]
    cw = HY_ORDER * hy_w
    tr = 512
    N = 2 * L
    n1 = N // LANES
    t = jnp.linspace(0.0, 1.0, L, dtype=F32)[:, None]
    bands = (FILTER_EMB - 1) // 2
    f = jnp.linspace(1e-4, bands - 1, bands, dtype=F32)[None, :]
    w = (2.0 * math.pi / L) * jnp.arange(L, dtype=F32)[:, None]
    zf = jnp.concatenate([t, jnp.cos(f * w), -jnp.sin(f * w)], axis=-1)
    z = jnp.concatenate([zf, zf[L - 1:L], zf[:0:-1]], axis=0)
    tl = z[:, 0].reshape(n1, 1, LANES)
    z = jnp.pad(z, ((0, 0), (0, LANES - FILTER_EMB)))
    eye2 = jnp.eye(2, dtype=F32)
    w1 = jnp.concatenate([jnp.pad(p['flt_w1'], ((0, LANES - FILTER_EMB), (0, hid))),
                          jnp.pad(p['flt_w1'], ((0, LANES - FILTER_EMB), (hid, 0)))], axis=0)
    w2 = jnp.kron(eye2, p['flt_w2'])
    w3 = jnp.kron(eye2, p['flt_w3'])
    two = lambda v: jnp.tile(v, 2)[None]
    wo_t = p['flt_w_out'].reshape(hid, 2, cw).transpose(1, 2, 0)
    woa = jnp.pad(wo_t, ((0, 0), (0, 0), (0, hid)))
    wob = jnp.pad(wo_t, ((0, 0), (0, 0), (hid, 0)))
    max_decay = math.log(DECAY_TARGET) / FAST_DECAY_PCT
    min_decay = math.log(DECAY_TARGET) / SLOW_DECAY_PCT
    deltas = jnp.abs(jnp.linspace(min_decay, max_decay, hy_w, dtype=F32))
    dl = jnp.tile(deltas, HY_ORDER)[:, None]
    nblk = N // tr
    halfb = nblk // 2
    full = lambda i: (0, 0)
    h2 = 2 * hid
    return pl.pallas_call(
        functools.partial(_filter_kernel, L=L),
        out_shape=(jax.ShapeDtypeStruct((n1, cw, LANES), F32), jax.ShapeDtypeStruct((cw, LANES), F32)),
        grid=(nblk,),
        in_specs=[pl.BlockSpec((tr, LANES), lambda i: (i, 0)),
                  pl.BlockSpec((tr // LANES, 1, LANES), lambda i: (i, 0, 0)),
                  pl.BlockSpec((2 * LANES, h2), full), pl.BlockSpec((1, h2), full),
                  pl.BlockSpec((h2, h2), full), pl.BlockSpec((1, h2), full),
                  pl.BlockSpec((h2, h2), full), pl.BlockSpec((1, h2), full),
                  pl.BlockSpec((1, h2), full),
                  pl.BlockSpec((1, cw, h2), lambda i: (i // halfb, 0, 0)),
                  pl.BlockSpec((1, cw, h2), lambda i: (i // halfb, 0, 0)),
                  pl.BlockSpec((cw, 1), full)],
        out_specs=(pl.BlockSpec((tr // LANES, cw, LANES), lambda i: (i, 0, 0)), pl.BlockSpec((cw, LANES), full)),
        compiler_params=_params("arbitrary"),
        name="hyena_filter",
    )(z, tl, w1, two(p['flt_b1']), w2, two(p['flt_b2']), w3, two(p['flt_b3']), two(p['flt_freq']), woa, wob, dl)


def _fft_consts(L, planes):
    N = 2 * L
    n1 = N // FFT_N2
    n1h = n1 // 2
    k1 = np.arange(n1)[:, None]
    th = 2.0 * np.pi * k1 * np.arange(n1)[None, :] / n1
    c, s = np.cos(th), np.sin(th)
    ch, sh = c[:, :n1h], s[:, :n1h]
    if planes == 1:
        m1 = np.concatenate([ch, -sh], axis=0)
        m3 = np.concatenate([ch.T, -sh.T], axis=1) / N
    else:
        m1 = np.block([[ch, sh], [-sh, ch]])
        m3 = np.block([[ch.T, -sh.T], [sh.T, ch.T]]) / N
    m1_full = np.concatenate([c, -s], axis=0)
    ph = 2.0 * np.pi * k1 * np.arange(FFT_N2)[None, :] / N
    twr, twi = np.cos(ph), -np.sin(ph)
    a = 2.0 * np.pi * np.arange(FFT_N2)[:, None] * np.arange(FFT_N2)[None, :] / FFT_N2
    ca, sa = np.cos(a), np.sin(a)
    rt = np.block([[ca, -sa], [sa, ca]])
    rit = np.block([[ca, sa], [-sa, ca]])
    bf = lambda m: jnp.asarray(np.asarray(m, np.float32).astype(BF16))
    return dict(n1=n1, n1h=n1h, m1=bf(m1), m3=bf(m3), m1_full=bf(m1_full),
                twr=jnp.asarray(twr, F32), twi=jnp.asarray(twi, F32), rt=bf(rt), rit=bf(rit))


CONV_CH = 16


def _load_channels(ref, lead):
    return jnp.concatenate([ref[lead + (slice(None), c, slice(None))] for c in range(CONV_CH)], axis=1)


def _fwd_lane_dft(ar, ai, twr, twi, rt):
    pieces = []
    for c in range(CONV_CH):
        r = ar[:, c * LANES:(c + 1) * LANES]
        i = ai[:, c * LANES:(c + 1) * LANES]
        pieces.append(jnp.concatenate([r * twr - i * twi, r * twi + i * twr], axis=1))
    return _dot(jnp.concatenate(pieces, axis=0).astype(BF16), rt)


def _spectrum_kernel(k_ref, ssq_ref, m1_ref, twr_ref, twi_ref, rt_ref, o_ref):
    n1 = twr_ref.shape[0]
    a = _dot(m1_ref[...], _load_channels(k_ref, ()).astype(BF16))
    x = _fwd_lane_dft(a[:n1], a[n1:], twr_ref[...], twi_ref[...], rt_ref[...])
    scale = lax.rsqrt(jnp.sum(ssq_ref[...], axis=1, keepdims=True) + EPS)
    for c in range(CONV_CH):
        o_ref[0, :, c * 2 * LANES:(c + 1) * 2 * LANES] = x[c * n1:(c + 1) * n1] * scale[c:c + 1, :]


def _filter_spectrum(k3, ssq, fc, hy_w):
    n1 = fc['n1']
    ct = CONV_CH
    ntile = hy_w // ct
    full = lambda o, j: (0, 0)
    return pl.pallas_call(
        _spectrum_kernel,
        out_shape=jax.ShapeDtypeStruct((HY_ORDER, n1, hy_w * 2 * LANES), F32),
        grid=(HY_ORDER, ntile),
        in_specs=[pl.BlockSpec((n1, ct, LANES), lambda o, j: (0, o * ntile + j, 0)),
                  pl.BlockSpec((ct, LANES), lambda o, j: (o * ntile + j, 0)),
                  pl.BlockSpec((2 * n1, n1), full),
                  pl.BlockSpec((n1, LANES), full), pl.BlockSpec((n1, LANES), full),
                  pl.BlockSpec((2 * LANES, 2 * LANES), full)],
        out_specs=pl.BlockSpec((1, n1, ct * 2 * LANES), lambda o, j: (o, 0, j)),
        compiler_params=_params("arbitrary", "arbitrary"),
        name="filter_spectrum",
    )(k3, ssq, fc['m1_full'], fc['twr'], fc['twi'], fc['rt'])


def _longconv_kernel(z_ref, g_ref, kf_ref, b_ref, m1_ref, twr_ref, twi_ref, rt_ref, rit_ref, m3_ref, o_ref):
    ct = CONV_CH
    n1 = twr_ref.shape[0]
    planes, n1h = z_ref.shape[1], z_ref.shape[2]
    z = jnp.concatenate([_load_channels(z_ref, (0, p)) for p in range(planes)], axis=0)
    a = _dot(m1_ref[...], z.astype(BF16))
    twr = twr_ref[...]
    twi = twi_ref[...]
    x = _fwd_lane_dft(a[:n1], a[n1:], twr, twi, rt_ref[...])
    ys = []
    for c in range(ct):
        xc = x[c * n1:(c + 1) * n1]
        kc = kf_ref[0, :, c * 2 * LANES:(c + 1) * 2 * LANES]
        xr, xi = xc[:, :LANES], xc[:, LANES:]
        kr, ki = kc[:, :LANES], kc[:, LANES:]
        ys.append(jnp.concatenate([xr * kr - xi * ki, xr * ki + xi * kr], axis=1))
    bp = _dot(jnp.concatenate(ys, axis=0).astype(BF16), rit_ref[...])
    brs, bis = [], []
    for c in range(ct):
        bc = bp[c * n1:(c + 1) * n1]
        r, i = bc[:, :LANES], bc[:, LANES:]
        brs.append(r * twr + i * twi)
        bis.append(i * twr - r * twi)
    bst = jnp.concatenate([jnp.concatenate(brs, axis=1), jnp.concatenate(bis, axis=1)], axis=0)
    yt = _dot(m3_ref[...], bst.astype(BF16))
    for p in range(planes):
        rows = slice(p * n1h, (p + 1) * n1h)
        for c in range(ct):
            lanes = slice(c * LANES, (c + 1) * LANES)
            o_ref[0, p, :, c, :] = g_ref[0, p, :, c, :] * (yt[rows, lanes] + b_ref[0, 0:1, c:c + 1] * z[rows, lanes])


def _longconv(z5, g5, kf, order, bias_t, fc):
    npair, planes, n1h, ch, _ = z5.shape
    n1 = fc['n1']
    ct = CONV_CH
    ntile = ch // ct
    full = lambda p, j: (0, 0)
    m1r, m1c = 2 * n1, planes * n1h
    blk = pl.BlockSpec((1, planes, n1h, ct, LANES), lambda p, j: (p, 0, 0, j, 0))
    return pl.pallas_call(
        _longconv_kernel,
        out_shape=jax.ShapeDtypeStruct(z5.shape, F32),
        grid=(npair, ntile),
        in_specs=[blk, blk,
                  pl.BlockSpec((1, n1, ct * 2 * LANES), lambda p, j: (order, 0, j)),
                  pl.BlockSpec((1, 8, ct), lambda p, j: (order * ntile + j, 0, 0)),
                  pl.BlockSpec((m1r, m1c), full),
                  pl.BlockSpec((n1, LANES), full), pl.BlockSpec((n1, LANES), full),
                  pl.BlockSpec((2 * LANES, 2 * LANES), full), pl.BlockSpec((2 * LANES, 2 * LANES), full),
                  pl.BlockSpec((m1c, m1r), full)],
        out_specs=blk,
        compiler_params=_params("arbitrary", "arbitrary"),
        name="hyena_longconv",
    )(z5, g5, kf, bias_t, fc['m1'], fc['twr'], fc['twi'], fc['rt'], fc['rit'], fc['m3'])


def _attn_kernel(sink_ref, q_ref, kp_ref, km_ref, kn_ref, vp_ref, vm_ref, vn_ref, o_ref, *, L, tq):
    i = pl.program_id(1)
    t0 = i * tq
    tk = tq + 2 * WINDOW
    kk = jnp.concatenate([kp_ref[...], km_ref[...], kn_ref[...]], axis=0)
    vv = jnp.concatenate([vp_ref[...], vm_ref[...], vn_ref[...]], axis=0)
    qpos = t0 + lax.broadcasted_iota(I32, (tq, tk), 0)
    kpos = t0 - WINDOW + lax.broadcasted_iota(I32, (tq, tk), 1)
    valid = (jnp.abs(kpos - qpos) <= WINDOW) & (kpos >= 0) & (kpos < L)
    for h in range(N_HEADS):
        s = _dot_nt(q_ref[:, h * LANES:(h + 1) * LANES], kk)
        s = jnp.where(valid, s, -1e30)
        sink = sink_ref[h]
        m = jnp.maximum(jnp.max(s, axis=-1, keepdims=True), sink)
        p = jnp.exp(s - m)
        denom = jnp.sum(p, axis=-1, keepdims=True) + jnp.exp(sink - m)
        o_ref[:, h * LANES:(h + 1) * LANES] = _dot((p / denom).astype(BF16), vv).astype(BF16)


def _attention(q, k, v, sink, B, L):
    T = q.shape[0]
    tq = 256
    nq = L // tq
    r = tq // WINDOW
    nb = L // WINDOW
    kv_w = k.shape[1]
    kp = pl.BlockSpec((WINDOW, kv_w), lambda b, i: (b * nb + jnp.maximum(i * r - 1, 0), 0))
    km = pl.BlockSpec((tq, kv_w), lambda b, i: (b * nq + i, 0))
    kn = pl.BlockSpec((WINDOW, kv_w), lambda b, i: (b * nb + jnp.minimum((i + 1) * r, nb - 1), 0))
    qs = pl.BlockSpec((tq, N_HEADS * LANES), lambda b, i: (b * nq + i, 0))
    return pl.pallas_call(
        functools.partial(_attn_kernel, L=L, tq=tq),
        out_shape=jax.ShapeDtypeStruct((T, N_HEADS * LANES), BF16),
        grid=(B, nq),
        in_specs=[pl.BlockSpec(memory_space=pltpu.SMEM), qs, kp, km, kn, kp, km, kn],
        out_specs=qs,
        compiler_params=_params("arbitrary", "arbitrary"),
        name="band_attention",
    )(sink, q, k, k, k, v, v, v)


def _merge_kernel(x_ref, z_ref, a_ref, gs_ref, mod_ref, why_ref, wat_ref, wo_ref, g2_ref, rwh_ref, rwl_ref,
                  xo_ref, h2_ref, aff_ref):
    d = x_ref.shape[1]
    mod = mod_ref[0]
    z = jnp.concatenate([z_ref[r].T for r in range(z_ref.shape[0])], axis=0)
    y_hy = _dot(z.astype(BF16), why_ref[...])
    y_at = _dot(a_ref[...], wat_ref[...])
    gs = gs_ref[...]
    merged = gs[:, :d] * y_hy + gs[:, d:] * y_at
    x = x_ref[...] + mod[2:3] * _dot(merged.astype(BF16), wo_ref[...])
    xo_ref[...] = x
    ms = jnp.mean(x * x, axis=-1, keepdims=True)
    h = (x * lax.rsqrt(ms + EPS) * g2_ref[...]) * (1.0 + mod[4:5]) + mod[3:4]
    h2_ref[...] = h
    hh, hl = _split(h)
    rh, rl = rwh_ref[...], rwl_ref[...]
    logits = _dot_nt(rh, hh) + _dot_nt(rl, hh) + _dot_nt(rh, hl)
    m = jnp.max(logits, axis=0, keepdims=True)
    e = jnp.exp(logits - m)
    aff_ref[...] = e / jnp.sum(e, axis=0, keepdims=True)


def _merge(x, z, att, gs, mod_l, b0, L, w_hy, w_at_x, w_o, norm2_g, rw_hi, rw_lo):
    T, D = x.shape
    tm = 512
    nper = L // tm
    E = rw_hi.shape[0]
    full = lambda i: (0, 0)
    row = lambda w: pl.BlockSpec((tm, w), lambda i: (i, 0))
    return pl.pallas_call(
        _merge_kernel,
        out_shape=(jax.ShapeDtypeStruct((T, D), F32), jax.ShapeDtypeStruct((T, D), F32),
                   jax.ShapeDtypeStruct((E, T), F32)),
        grid=(T // tm,),
        in_specs=[row(D), pl.BlockSpec((tm // LANES,) + z.shape[1:], lambda i: (i, 0, 0)), row(att.shape[1]), row(gs.shape[1]),
                  pl.BlockSpec((1, 6, D), lambda i: (b0 + i // nper, 0, 0)),
                  pl.BlockSpec(w_hy.shape, full), pl.BlockSpec(w_at_x.shape, full), pl.BlockSpec(w_o.shape, full),
                  pl.BlockSpec((1, D), full), pl.BlockSpec((E, D), full), pl.BlockSpec((E, D), full)],
        out_specs=(row(D), row(D), pl.BlockSpec((E, tm), lambda i: (0, i))),
        compiler_params=_params("arbitrary"),
        name="merge_router",
    )(x, z, att, gs, mod_l, w_hy, w_at_x, w_o, norm2_g, rw_hi, rw_lo)


def _cumsum_rows(mb, u, ones, ls):
    rc = _dot(mb, u)
    totb = _dot(mb, ones)
    offs = _dot(ls, totb.astype(BF16))
    return rc, totb, offs


def _topk_kernel(aff_ref, u_ref, ut_ref, ones_ref, ls_ref, idx_ref, gate_ref, dst_ref, cum_ref, cnt_ref,
                 thr_ref, sel_ref, rank_ref, *, cap):
    E, R, _ = aff_ref.shape
    bits = pltpu.bitcast(aff_ref[...], I32)

    def search(i, cur):
        cand = cur | jnp.left_shift(jnp.int32(1), 30 - i)
        ge = jnp.where(bits >= cand, 1.0, 0.0)
        c = jnp.sum(jnp.sum(ge, axis=2, keepdims=True), axis=1, keepdims=True)
        return jnp.where(c >= cap, cand, cur)

    thr = lax.fori_loop(0, 31, search, jnp.zeros((E, 1, 1), I32))
    thr_ref[...] = jnp.broadcast_to(thr, thr_ref.shape)
    u = u_ref[...]
    ut = ut_ref[...]
    ones = ones_ref[...]
    ls = ls_ref[...]

    def select(e, carry):
        cnt_run, cum = carry
        b = pltpu.bitcast(aff_ref[e], I32)
        t = thr_ref[e][0:1, :]
        gt = b > t
        eq = b == t
        ngt = jnp.sum(jnp.sum(jnp.where(gt, 1.0, 0.0), axis=1, keepdims=True), axis=0, keepdims=True)
        need = cap - ngt
        erc, _, eoffs = _cumsum_rows(jnp.where(eq, 1.0, 0.0).astype(BF16), u, ones, ls)
        sel = jnp.where(gt | (eq & ((erc + eoffs) <= need)), 1.0, 0.0)
        rc, _, offs = _cumsum_rows(sel.astype(BF16), u, ones, ls)
        sel_ref[e] = sel
        rank_ref[e] = cnt_run
        return cnt_run + sel, cum + (rc + offs - sel)

    zero = jnp.zeros((R, LANES), F32)
    cnt, cum = lax.fori_loop(0, E, select, (zero, zero))
    cnt_ref[...] = cnt.astype(I32)
    cum_ref[...] = cum.astype(I32)
    s_row = lax.broadcasted_iota(I32, (1, cap), 1).astype(F32)
    r_col = lax.broadcasted_iota(I32, (R, 1), 0).astype(F32)
    c_col = lax.broadcasted_iota(I32, (LANES, 1), 0).astype(F32)

    def pick(vt_pieces, ohb, hit):
        acc = _dot(vt_pieces[0], ohb)
        for piece in vt_pieces[1:]:
            acc = acc + _dot(piece, ohb)
        return jnp.sum(jnp.where(hit, acc, 0.0), axis=0, keepdims=True)

    def compact(e, carry):
        sel = sel_ref[e]
        _, totb, offs = _cumsum_rows(sel.astype(BF16), u, ones, ls)
        offs_col = offs[:, 0:1]
        end_col = offs_col + totb[:, 0:1]
        oht = (offs_col <= s_row) & (s_row < end_col)
        row_s = jnp.sum(jnp.where(oht, r_col, 0.0), axis=0, keepdims=True)
        offs_s = jnp.sum(jnp.where(oht, offs_col, 0.0), axis=0, keepdims=True)
        ohb = jnp.where(oht, 1.0, 0.0).astype(BF16)
        rct = _dot(ut, sel.T.astype(BF16))
        rcs = _dot(rct.astype(BF16), ohb)
        col_s = jnp.sum(jnp.where(rcs <= (s_row - offs_s), 1.0, 0.0), axis=0, keepdims=True)
        hit = c_col == col_s
        idx_ref[pl.ds(e, 1), :] = (row_s * LANES + col_s).astype(I32)
        at = aff_ref[e].T
        p0 = at.astype(BF16)
        r1 = at - p0.astype(F32)
        p1 = r1.astype(BF16)
        p2 = (r1 - p1.astype(F32)).astype(BF16)
        gate_ref[pl.ds(e, 1), :] = pick((p0, p1, p2), ohb, hit)
        d = (cum + rank_ref[e]).T
        d_hi = (d * (1.0 / 256.0)).astype(I32).astype(F32)
        d_lo = d - 256.0 * d_hi
        dst = 256.0 * pick((d_hi.astype(BF16),), ohb, hit) + pick((d_lo.astype(BF16),), ohb, hit)
        dst_ref[pl.ds(e, 1), :] = dst.astype(I32)
        return carry

    lax.fori_loop(0, E, compact, 0)


def _topk(aff_t, cap):
    E, T = aff_t.shape
    R = T // LANES
    u = jnp.asarray(np.triu(np.ones((LANES, LANES))), BF16)
    ut = jnp.asarray(np.tril(np.ones((LANES, LANES))), BF16)
    ones = jnp.ones((LANES, LANES), BF16)
    ls = jnp.asarray(np.tril(np.ones((R, R)), -1), BF16)
    z2 = lambda: (0, 0)
    z3 = lambda: (0, 0, 0)
    slot = jax.ShapeDtypeStruct((E, cap), I32)
    tok = jax.ShapeDtypeStruct((R, LANES), I32)
    idx, gate, dst, cum, cnt = pl.pallas_call(
        functools.partial(_topk_kernel, cap=cap),
        out_shape=(slot, jax.ShapeDtypeStruct((E, cap), F32), slot, tok, tok),
        in_specs=[pl.BlockSpec((E, R, LANES), z3), pl.BlockSpec((LANES, LANES), z2), pl.BlockSpec((LANES, LANES), z2),
                  pl.BlockSpec((LANES, LANES), z2), pl.BlockSpec((R, R), z2)],
        out_specs=(pl.BlockSpec((E, cap), z2), pl.BlockSpec((E, cap), z2), pl.BlockSpec((E, cap), z2),
                   pl.BlockSpec((R, LANES), z2), pl.BlockSpec((R, LANES), z2)),
        scratch_shapes=[pltpu.VMEM((E, 8, LANES), I32), pltpu.VMEM((E, R, LANES), F32),
                        pltpu.VMEM((E, R, LANES), F32)],
        compiler_params=pltpu.CompilerParams(vmem_limit_bytes=VMEM_LIMIT),
        name="expert_topk",
    )(aff_t.reshape(E, R, LANES), u, ut, ones, ls)
    return idx, gate, dst, cum.reshape(T), cnt.reshape(T)


ROW_CHUNKS = 4
FF_TILE = 256


def _expert_kernel(src_ref, dst_ref, h_ref, wg_ref, wu_ref, wd_ref, gate_ref, out_ref,
                   xbuf, xb, acc, gsem, ssem, *, cap, per):
    e = pl.program_id(0)
    f = pl.program_id(1)
    ne = pl.num_programs(0)
    nf = pl.num_programs(1)
    slot = e % 2
    other = 1 - slot
    sched = nf * per

    def gather_row(entry, r):
        pltpu.make_async_copy(h_ref.at[pl.ds(src_ref[entry], 1), :], xbuf.at[pl.ds(r, 1), :], gsem).start()

    def scatter_row(src_slot, entry, r):
        pltpu.make_async_copy(acc.at[src_slot, pl.ds(r, 1), :], out_ref.at[pl.ds(dst_ref[entry], 1), :],
                              ssem.at[src_slot]).start()

    def schedule_done(sem):
        pltpu.make_async_copy(h_ref.at[pl.ds(0, sched), :], xbuf, sem).wait()

    @pl.when((e == 0) & (f == 0))
    def _():
        def body(s, c):
            gather_row(s, s)
            return c

        lax.fori_loop(0, sched, body, 0, unroll=8)
        acc[...] = jnp.zeros(acc.shape, F32)

    @pl.when((e > 0) & (f == 0))
    def _():
        schedule_done(ssem.at[slot])
        acc[slot, pl.ds(0, cap), :] = jnp.zeros((cap, acc.shape[2]), F32)

    @pl.when(f == 0)
    def _():
        schedule_done(gsem)
        xb[...] = xbuf[pl.ds(0, cap), :].astype(BF16)

    wg = wg_ref[0, 0].astype(BF16)
    wu = wu_ref[0, 0].astype(BF16)
    wd = wd_ref[0, 0].astype(BF16)
    gbase = (jnp.minimum(e + 1, ne - 1) * nf + f) * per
    sbase = (e * nf + f) * per
    row0 = f * per
    share = per // ROW_CHUNKS
    chunk = cap // ROW_CHUNKS
    for c in range(ROW_CHUNKS):
        for s in range(c * share, (c + 1) * share):
            gather_row(gbase + s, row0 + s)
            scatter_row(other, sbase + s, row0 + s)
        rows = pl.ds(c * chunk, chunk)
        x = xb[rows, :]
        g = _dot(x, wg)
        u = _dot(x, wu)
        hid = (g * jax.nn.sigmoid(g) * u).astype(BF16)
        acc[slot, rows, :] += _dot(hid, wd)

    @pl.when(f == nf - 1)
    def _():
        acc[slot, pl.ds(0, cap), :] = gate_ref[0] * acc[slot, pl.ds(0, cap), :]

    @pl.when((f == nf - 1) & (e == ne - 1))
    def _():
        def body(s, c):
            scatter_row(slot, ne * sched + s, s)
            return c

        lax.fori_loop(0, sched, body, 0, unroll=8)
        schedule_done(gsem)
        schedule_done(ssem.at[other])
        schedule_done(ssem.at[slot])


def _experts(h2, idx, dst, w_gate, w_up, w_down, layer, gate_col, cap):
    _, E, D, F = w_gate.shape
    ft = FF_TILE
    nf = F // ft
    per = -(-cap // (nf * 8 * ROW_CHUNKS)) * (8 * ROW_CHUNKS)
    sched = nf * per
    nr = E * cap
    assert E >= 2 and cap % ROW_CHUNKS == 0 and sched >= cap
    rows = jnp.arange(sched, dtype=I32)
    real = rows < cap
    rc = jnp.minimum(rows, cap - 1)
    src_tab = idx[:, rc]
    odd = (jnp.arange(E + 1, dtype=I32)[:, None] % 2) == 1
    spare = jnp.where(odd, nr + sched + (rows[None, :] - cap), nr + rows[None, :])
    dst_prev = jnp.concatenate([jnp.zeros((1, sched), I32), dst[:, rc]], axis=0)
    dst_tab = jnp.where(real[None, :] & (jnp.arange(E + 1)[:, None] > 0), dst_prev, spare)
    return pl.pallas_call(
        functools.partial(_expert_kernel, cap=cap, per=per),
        out_shape=jax.ShapeDtypeStruct((nr + 2 * sched - cap, D), F32),
        grid_spec=pltpu.PrefetchScalarGridSpec(
            num_scalar_prefetch=2,
            grid=(E, nf),
            in_specs=[pl.BlockSpec(memory_space=pl.ANY),
                      pl.BlockSpec((1, 1, D, ft), lambda e, f, i, d: (layer, e, 0, f)),
                      pl.BlockSpec((1, 1, D, ft), lambda e, f, i, d: (layer, e, 0, f)),
                      pl.BlockSpec((1, 1, ft, D), lambda e, f, i, d: (layer, e, f, 0)),
                      pl.BlockSpec((1, cap, 1), lambda e, f, i, d: (e, 0, 0))],
            out_specs=pl.BlockSpec(memory_space=pl.ANY),
            scratch_shapes=[pltpu.VMEM((sched, D), F32), pltpu.VMEM((cap, D), BF16), pltpu.VMEM((2, sched, D), F32),
                            pltpu.SemaphoreType.DMA, pltpu.SemaphoreType.DMA((2,))]),
        compiler_params=_params("arbitrary", "arbitrary"),
        name="expert_ffn",
    )(src_tab.reshape(-1), dst_tab.reshape(-1), h2, w_gate, w_up, w_down, gate_col)


COMBINE_ROWS = 512
COMBINE_TOKENS = 256


def _combine_kernel(tile_ref, blk_ref, flag_ref, c_ref, x_ref, mod_ref, tab_ref, eye_ref, o_ref):
    j = pl.program_id(0)
    flag = flag_ref[j]
    first = (flag & 1) != 0
    last = (flag & 2) != 0
    skip = (flag & 4) != 0
    rb = c_ref.shape[0]
    cols = _dot_nt(eye_ref[...], tab_ref[0].astype(BF16))
    start = cols[:, 0:1] * 256.0 + cols[:, 1:2]
    count = cols[:, 2:3]
    rabs = (blk_ref[j] * rb + lax.broadcasted_iota(I32, (1, rb), 1)).astype(F32)
    seg = jnp.where((start <= rabs) & (rabs < start + count), 1.0, 0.0).astype(BF16)
    hi, lo = _split(c_ref[...])
    part = _dot(seg, hi) + _dot(seg, lo)

    @pl.when(first)
    def _():
        o_ref[...] = part

    @pl.when(jnp.logical_not(first | skip))
    def _():
        o_ref[...] += part

    @pl.when(last & jnp.logical_not(skip))
    def _():
        o_ref[...] = x_ref[...] + mod_ref[0][5:6] * o_ref[...]


def _combine(x, mod_l, b0, L, contrib, cum, cnt):
    T, D = x.shape
    tt = COMBINE_TOKENS
    R = T // tt
    rb = COMBINE_ROWS
    nr = CAPACITY_FACTOR * T
    nb = nr // rb
    nper = L // tt
    nitem = R + nb
    lo = cum[::tt]
    hi = jnp.concatenate([lo[1:], jnp.full((1,), nr, I32)])
    b_lo = jnp.minimum(lo // rb, nb - 1)
    b_hi = jnp.minimum(jnp.maximum((hi - 1) // rb, b_lo), nb - 1)
    n = b_hi - b_lo + 1
    cn = jnp.cumsum(n)
    start = cn - n
    j = jnp.arange(nitem, dtype=I32)
    valid = j < cn[-1]
    ti = jnp.minimum(jnp.searchsorted(cn, j, side='right').astype(I32), R - 1)
    blk = jnp.where(valid, b_lo[ti] + (j - start[ti]), b_hi[R - 1])
    flag = jnp.where(valid, (j == start[ti]) * 1 + (j == cn[ti] - 1) * 2, 4).astype(I32)
    tab = jnp.stack([(cum >> 8).astype(F32), (cum & 255).astype(F32), cnt.astype(F32)], axis=0)
    tab = jnp.concatenate([tab, jnp.zeros((5, T), F32)], axis=0).reshape(8, R, tt).transpose(1, 0, 2)
    eye = jnp.eye(tt, dtype=BF16)
    return pl.pallas_call(
        _combine_kernel,
        out_shape=jax.ShapeDtypeStruct((T, D), F32),
        grid_spec=pltpu.PrefetchScalarGridSpec(
            num_scalar_prefetch=3,
            grid=(nitem,),
            in_specs=[pl.BlockSpec((rb, D), lambda j, t, b, f: (b[j], 0)),
                      pl.BlockSpec((tt, D), lambda j, t, b, f: (t[j], 0)),
                      pl.BlockSpec((1, 6, D), lambda j, t, b, f: (b0 + t[j] // nper, 0, 0)),
                      pl.BlockSpec((1, 8, tt), lambda j, t, b, f: (t[j], 0, 0)),
                      pl.BlockSpec((tt, tt), lambda j, t, b, f: (0, 0))],
            out_specs=pl.BlockSpec((tt, D), lambda j, t, b, f: (t[j], 0))),
        compiler_params=_params("arbitrary"),
        name="moe_combine",
    )(ti, blk, flag, contrib, x, mod_l, tab, eye)


def _layer(x, B, L, b0, mod_l, p, kf, fc, tabs):
    hy_w = p['hy_bias'].shape[1]
    att_w = N_HEADS * HEAD_DIM
    kv_w = N_KV_HEADS * HEAD_DIM
    planes = 2 if B % 2 == 0 else 1
    z, ga, gb, q, k, v, gs = _inproj(x, mod_l, b0, L, p['norm1_g'][None], p['w_in_bf'], p['hy_conv_w'], p['hy_conv_b'],
                                     tabs['cos'], tabs['sin'], tabs['qg'], tabs['kg'], tabs['blk'], hy_w, att_w, kv_w)
    shape5 = (B // planes, planes, L // LANES, hy_w, LANES)
    z = z.reshape(shape5)
    gates = (ga.reshape(shape5), gb.reshape(shape5))
    for o in range(HY_ORDER):
        z = _longconv(z, gates[o], kf, o, p['bias_t'], fc)
    z = z.reshape(B * L // LANES, hy_w, LANES)
    att = _attention(q, k, v, p['sink_logit'], B, L)
    x, h2, aff_t = _merge(x, z, att, gs, mod_l, b0, L, p['w_hy_bf'], p['w_at_x'], p['w_o_bf'], p['norm2_g'][None],
                          p['rw_hi'], p['rw_lo'])
    T = x.shape[0]
    cap = CAPACITY_FACTOR * T // N_EXPERTS
    idx, gate, dst, cum, cnt = _topk(aff_t, cap)
    contrib = _experts(h2, idx, dst, p['exp_w_gate'], p['exp_w_up'], p['exp_w_down'], p['layer'],
                       gate.reshape(N_EXPERTS, cap, 1), cap)
    return _combine(x, mod_l, b0, L, contrib, cum, cnt)


def _rope_tables(L):
    inv = 1.0 / (ROPE_THETA ** (jnp.arange(0, HEAD_DIM, 2, dtype=F32) / HEAD_DIM))
    ang = jnp.arange(L, dtype=F32)[:, None] * inv[None, :]
    cos, sin = jnp.cos(ang), jnp.sin(ang)
    cos_t = jnp.tile(cos, (1, LANES // (HEAD_DIM // 2)))
    sin_t = jnp.tile(jnp.concatenate([-sin, sin], axis=1), (1, LANES // HEAD_DIM))
    return cos_t, sin_t


def _filter_spectrum_for(L, p, fc):
    hy_w = p['hy_bias'].shape[1]
    k3, ssq = _filter_rows(L, p)
    return _filter_spectrum(k3, ssq, fc, hy_w)


def kernel(x_prompt, x_sample, c_prompt, c_sample, ada_w, ada_b, norm1_g, norm2_g, w_in, hy_conv_w, hy_conv_b,
           flt_w1, flt_b1, flt_w2, flt_b2, flt_w3, flt_b3, flt_freq, flt_w_out, hy_bias, q_norm_g, k_norm_g,
           sink_logit, w_hy_proj, w_at_proj, w_o, router_w, exp_w_gate, exp_w_up, exp_w_down):
    depth = ada_w.shape[0]
    Bp, Lp, D = x_prompt.shape
    Bs, Ls, _ = x_sample.shape
    nb = Bp + Bs
    c_all = jnp.concatenate([c_prompt, c_sample, jnp.zeros(((-nb) % 8, D), F32)], axis=0)
    mod = _adaln(c_all, ada_w, ada_b).reshape(depth, c_all.shape[0], 6, D)
    hy_w = hy_bias.shape[2]
    att_w = N_HEADS * HEAD_DIM
    groups = [dict(B=Bp, L=Lp, b0=0, x=x_prompt.reshape(Bp * Lp, D)),
              dict(B=Bs, L=Ls, b0=Bp, x=x_sample.reshape(Bs * Ls, D))]
    blk = jnp.asarray(np.kron(np.eye(LANES // HEAD_DIM), np.ones((HEAD_DIM, HEAD_DIM))), BF16)
    for g in groups:
        g['fc'] = _fft_consts(g['L'], 2 if g['B'] % 2 == 0 else 1)
        cos_t, sin_t = _rope_tables(g['L'])
        g['tabs'] = dict(cos=cos_t, sin=sin_t, blk=blk)
    heads_per_kv = N_HEADS // N_KV_HEADS
    for l in range(depth):
        w_at_x = jnp.zeros((N_HEADS, LANES, D), BF16)
        w_at_h = w_at_proj[l].astype(BF16).reshape(N_HEADS, HEAD_DIM, D)
        for h in range(N_HEADS):
            off = (h // heads_per_kv) * HEAD_DIM
            w_at_x = w_at_x.at[h, off:off + HEAD_DIM].set(w_at_h[h])
        rw_hi, rw_lo = _split(router_w[l].T)
        p = dict(norm1_g=norm1_g[l], norm2_g=norm2_g[l], w_in_bf=w_in[l].astype(BF16),
                 hy_conv_w=hy_conv_w[l], hy_conv_b=hy_conv_b[l],
                 flt_w1=flt_w1[l], flt_b1=flt_b1[l], flt_w2=flt_w2[l], flt_b2=flt_b2[l], flt_w3=flt_w3[l],
                 flt_b3=flt_b3[l], flt_freq=flt_freq[l], flt_w_out=flt_w_out[l], hy_bias=hy_bias[l],
                 bias_t=jnp.broadcast_to(hy_bias[l].reshape(-1, 1, CONV_CH), (HY_ORDER * hy_w // CONV_CH, 8, CONV_CH)),
                 sink_logit=sink_logit[l], w_hy_bf=w_hy_proj[l].astype(BF16),
                 w_at_x=w_at_x.reshape(N_HEADS * LANES, D), w_o_bf=w_o[l].astype(BF16),
                 rw_hi=rw_hi, rw_lo=rw_lo,
                 layer=l, exp_w_gate=exp_w_gate, exp_w_up=exp_w_up, exp_w_down=exp_w_down)
        qg = jnp.tile(q_norm_g[l], LANES // HEAD_DIM)[None]
        kg = jnp.tile(k_norm_g[l], LANES // HEAD_DIM)[None]
        for g in groups:
            kf = _filter_spectrum_for(g['L'], p, g['fc'])
            tabs = dict(g['tabs'], qg=qg, kg=kg)
            g['x'] = _layer(g['x'], g['B'], g['L'], g['b0'], mod[l], p, kf, g['fc'], tabs)
    return (groups[0]['x'].reshape(Bp, Lp, D), groups[1]['x'].reshape(Bs, Ls, D))
```

```python
import functools
import math

import numpy as np
import jax
import jax.numpy as jnp
from jax import lax
from jax.experimental import pallas as pl
from jax.experimental.pallas import tpu as pltpu

F32 = jnp.float32
BF16 = jnp.bfloat16
I32 = jnp.int32

EPS = 1e-6
HEAD_DIM = 64
N_HEADS = 8
N_KV_HEADS = 2
WINDOW = 128
ROPE_THETA = 10000.0
HY_ORDER = 2
FILTER_EMB = 33
DECAY_TARGET = 0.01
FAST_DECAY_PCT = 0.3
SLOW_DECAY_PCT = 1.5
N_EXPERTS = 16
CAPACITY_FACTOR = 2
LANES = 128
FFT_N2 = LANES
VMEM_LIMIT = 52 << 20


def _params(*sem):
    return pltpu.CompilerParams(dimension_semantics=sem, vmem_limit_bytes=VMEM_LIMIT)


def _dot(a, b):
    return jnp.dot(a, b, preferred_element_type=F32)


def _dot_nt(a, b):
    return lax.dot_general(a, b, (((1,), (1,)), ((), ())), preferred_element_type=F32)


def _split(x):
    hi = x.astype(BF16)
    lo = (x - hi.astype(F32)).astype(BF16)
    return hi, lo


def _dot3(a_hi, a_lo, b_hi, b_lo):
    return _dot(a_hi, b_hi) + _dot(a_lo, b_hi) + _dot(a_hi, b_lo)


def _ada_kernel(c_ref, w_ref, b_ref, o_ref):
    c = c_ref[...]
    s = c * jax.nn.sigmoid(c)
    o_ref[0] = _dot(s.astype(BF16), w_ref[0].astype(BF16)) + b_ref[0]


def _adaln(c_all, ada_w, ada_b):
    depth, d, n = ada_w.shape
    tn = n // 4
    return pl.pallas_call(
        _ada_kernel,
        out_shape=jax.ShapeDtypeStruct((depth, c_all.shape[0], n), F32),
        grid=(depth, n // tn),
        in_specs=[pl.BlockSpec(c_all.shape, lambda l, j: (0, 0)),
                  pl.BlockSpec((1, d, tn), lambda l, j: (l, 0, j)),
                  pl.BlockSpec((1, 1, tn), lambda l, j: (l, 0, j))],
        out_specs=pl.BlockSpec((1, c_all.shape[0], tn), lambda l, j: (l, 0, j)),
        compiler_params=_params("arbitrary", "arbitrary"),
        name="adaln",
    )(c_all, ada_w, ada_b.reshape(depth, 1, n))


def _head_norm_rope(x, gain, cos, sin, blk_hi):
    sq = x * x
    hi, lo = _split(sq)
    ssum = _dot(hi, blk_hi) + _dot(lo, blk_hi)
    xn = x * lax.rsqrt(ssum * (1.0 / HEAD_DIM) + EPS) * gain
    lane = lax.broadcasted_iota(I32, x.shape, 1)
    first_half = (lane % HEAD_DIM) < (HEAD_DIM // 2)
    rot = jnp.where(first_half, pltpu.roll(xn, LANES - HEAD_DIM // 2, 1), pltpu.roll(xn, HEAD_DIM // 2, 1))
    return xn * cos + rot * sin


def _inproj_kernel(x_ref, xp_ref, xn_ref, mod_ref, g_ref, w_ref, cw_ref, cb_ref, cos_ref, sin_ref, qg_ref, kg_ref, blk_ref,
                   z_ref, ga_ref, gb_ref, q_ref, k_ref, v_ref, gs_ref, *, hy_w, att_w, kv_w, nper):
    i = pl.program_id(0)
    tm = x_ref.shape[0]
    halo = xp_ref.shape[0]
    x = jnp.concatenate([xp_ref[...], x_ref[...], xn_ref[...]], axis=0)
    mod = mod_ref[0]
    ms = jnp.mean(x * x, axis=-1, keepdims=True)
    h = (x * lax.rsqrt(ms + EPS) * g_ref[...]) * (1.0 + mod[1:2]) + mod[0:1]
    hb_all = h.astype(BF16)
    hb = hb_all[halo:halo + tm]
    s0 = 3 * hy_w
    s1 = s0 + att_w
    s2 = s1 + kv_w
    s3 = s2 + kv_w
    hy = _dot(hb_all, w_ref[:, 0:s0])
    n_all = tm + 2 * halo
    row = lax.broadcasted_iota(I32, (tm, 1), 0)
    first = (i % nper) == 0
    last = (i % nper) == nper - 1
    up = jnp.where((row == 0) & first, 0.0, pltpu.roll(hy, 1, 0)[halo:halo + tm])
    dn = jnp.where((row == tm - 1) & last, 0.0, pltpu.roll(hy, n_all - 1, 0)[halo:halo + tm])
    y = up * cw_ref[0:1, :] + hy[halo:halo + tm] * cw_ref[1:2, :] + dn * cw_ref[2:3, :] + cb_ref[...]
    for r in range(tm // LANES):
        blk_y = y[r * LANES:(r + 1) * LANES, :]
        z_ref[r] = blk_y[:, 0:hy_w].T
        ga_ref[r] = blk_y[:, hy_w:2 * hy_w].T
        gb_ref[r] = blk_y[:, 2 * hy_w:3 * hy_w].T
    q = _dot(hb, w_ref[:, s0:s1])
    k = _dot(hb, w_ref[:, s1:s2])
    v_ref[...] = _dot(hb, w_ref[:, s2:s3]).astype(BF16)
    gs_ref[...] = jax.nn.sigmoid(_dot(hb, w_ref[:, s3:]))
    cos = cos_ref[...]
    sin = sin_ref[...]
    blk = blk_ref[...]
    k_ref[...] = _head_norm_rope(k, kg_ref[...], cos, sin, blk).astype(BF16)
    lane = lax.broadcasted_iota(I32, cos.shape, 1)
    low = lane < HEAD_DIM
    heads_per_kv = N_HEADS // N_KV_HEADS
    for ci in range(att_w // LANES):
        qr = _head_norm_rope(q[:, ci * LANES:(ci + 1) * LANES], qg_ref[...], cos, sin, blk) * (HEAD_DIM ** -0.5)
        qs = pltpu.roll(qr, HEAD_DIM, 1)
        kvh = (2 * ci) // heads_per_kv
        if kvh == 0:
            h0 = jnp.where(low, qr, 0.0)
            h1 = jnp.where(low, qs, 0.0)
        else:
            h0 = jnp.where(low, 0.0, qs)
            h1 = jnp.where(low, 0.0, qr)
        q_ref[:, (2 * ci) * LANES:(2 * ci + 1) * LANES] = h0.astype(BF16)
        q_ref[:, (2 * ci + 1) * LANES:(2 * ci + 2) * LANES] = h1.astype(BF16)


def _inproj(x, mod_l, b0, L, norm_g, w_in_bf, conv_w, conv_b, cos_t, sin_t, qg, kg, blk, hy_w, att_w, kv_w):
    T, D = x.shape
    tm = 512
    halo = 8
    nper = L // tm
    r8 = tm // halo
    nb8 = T // halo
    in_w = w_in_bf.shape[1]
    kern = functools.partial(_inproj_kernel, hy_w=hy_w, att_w=att_w, kv_w=kv_w, nper=nper)
    stream = jax.ShapeDtypeStruct((T // LANES, hy_w, LANES), F32)
    sspec = pl.BlockSpec((tm // LANES, hy_w, LANES), lambda i: (i, 0, 0))
    full = lambda i: (0, 0)
    return pl.pallas_call(
        kern,
        out_shape=(stream, stream, stream,
                   jax.ShapeDtypeStruct((T, N_HEADS * LANES), BF16),
                   jax.ShapeDtypeStruct((T, kv_w), BF16),
                   jax.ShapeDtypeStruct((T, kv_w), BF16),
                   jax.ShapeDtypeStruct((T, in_w - 3 * hy_w - att_w - 2 * kv_w), F32)),
        grid=(T // tm,),
        in_specs=[pl.BlockSpec((tm, D), lambda i: (i, 0)),
                  pl.BlockSpec((halo, D), lambda i: (jnp.maximum(i * r8 - 1, 0), 0)),
                  pl.BlockSpec((halo, D), lambda i: (jnp.minimum((i + 1) * r8, nb8 - 1), 0)),
                  pl.BlockSpec((1, 6, D), lambda i: (b0 + i // nper, 0, 0)),
                  pl.BlockSpec((1, D), full),
                  pl.BlockSpec((D, in_w), full),
                  pl.BlockSpec((3, 3 * hy_w), full),
                  pl.BlockSpec((1, 3 * hy_w), full),
                  pl.BlockSpec((tm, LANES), lambda i: (i % nper, 0)),
                  pl.BlockSpec((tm, LANES), lambda i: (i % nper, 0)),
                  pl.BlockSpec((1, LANES), full),
                  pl.BlockSpec((1, LANES), full),
                  pl.BlockSpec((LANES, LANES), full)],
        out_specs=(sspec, sspec, sspec,
                   pl.BlockSpec((tm, N_HEADS * LANES), lambda i: (i, 0)),
                   pl.BlockSpec((tm, kv_w), lambda i: (i, 0)),
                   pl.BlockSpec((tm, kv_w), lambda i: (i, 0)),
                   pl.BlockSpec((tm, in_w - 3 * hy_w - att_w - 2 * kv_w), lambda i: (i, 0))),
        compiler_params=_params("arbitrary"),
        name="inproj",
    )(x, x, x, mod_l, norm_g, w_in_bf, conv_w, conv_b.reshape(1, -1), cos_t, sin_t, qg, kg, blk)


def _filter_kernel(z_ref, tl_ref, w1_ref, b1_ref, w2_ref, b2_ref, w3_ref, b3_ref, fr_ref, woa_ref, wob_ref, dl_ref,
                   k_ref, ssq_ref, *, L):
    i = pl.program_id(0)
    z = z_ref[...]
    tr = z.shape[0]
    half = tr // 2
    fr = fr_ref[...]

    def lin(a, w_ref_, b_ref_):
        ah, al = _split(a)
        wh, wl = _split(w_ref_[...])
        return _dot3(ah, al, wh, wl) + b_ref_[...]

    zz = jnp.concatenate([z[:half], z[half:]], axis=1)
    h = jnp.sin(fr * lin(zz, w1_ref, b1_ref))
    h = jnp.sin(fr * lin(h, w2_ref, b2_ref))
    h = jnp.sin(fr * lin(h, w3_ref, b3_ref))
    hh, hl = _split(h)
    wts = (_split(woa_ref[0]), _split(wob_ref[0]))

    @pl.when(i == 0)
    def _():
        ssq_ref[...] = jnp.zeros_like(ssq_ref)

    lane = lax.broadcasted_iota(I32, (1, LANES), 1)
    per_half = half // LANES
    for r in range(tr // LANES):
        wh, wl = wts[r // per_half]
        rows = slice((r % per_half) * LANES, (r % per_half + 1) * LANES)
        kt = _dot_nt(wh, hh[rows]) + _dot_nt(wl, hh[rows]) + _dot_nt(wh, hl[rows])
        kt = kt * jnp.exp(-dl_ref[...] * tl_ref[r])
        kt = jnp.where((i * tr + r * LANES + lane) == L, 0.0, kt)
        k_ref[r] = kt
        ssq_ref[...] += kt * kt


def _filter_rows(L, p):
    hid = p['flt_w2'].shape[0]
    hy_w = p['hy_bias'].shape[1]
    cw = HY_ORDER * hy_w
    tr = 512
    N = 2 * L
    n1 = N // LANES
    t = jnp.linspace(0.0, 1.0, L, dtype=F32)[:, None]
    bands = (FILTER_EMB - 1) // 2
    f = jnp.linspace(1e-4, bands - 1, bands, dtype=F32)[None, :]
    w = (2.0 * math.pi / L) * jnp.arange(L, dtype=F32)[:, None]
    zf = jnp.concatenate([t, jnp.cos(f * w), -jnp.sin(f * w)], axis=-1)
    z = jnp.concatenate([zf, zf[L - 1:L], zf[:0:-1]], axis=0)
    tl = z[:, 0].reshape(n1, 1, LANES)
    z = jnp.pad(z, ((0, 0), (0, LANES - FILTER_EMB)))
    eye2 = jnp.eye(2, dtype=F32)
    w1 = jnp.concatenate([jnp.pad(p['flt_w1'], ((0, LANES - FILTER_EMB), (0, hid))),
                          jnp.pad(p['flt_w1'], ((0, LANES - FILTER_EMB), (hid, 0)))], axis=0)
    w2 = jnp.kron(eye2, p['flt_w2'])
    w3 = jnp.kron(eye2, p['flt_w3'])
    two = lambda v: jnp.tile(v, 2)[None]
    wo_t = p['flt_w_out'].reshape(hid, 2, cw).transpose(1, 2, 0)
    woa = jnp.pad(wo_t, ((0, 0), (0, 0), (0, hid)))
    wob = jnp.pad(wo_t, ((0, 0), (0, 0), (hid, 0)))
    max_decay = math.log(DECAY_TARGET) / FAST_DECAY_PCT
    min_decay = math.log(DECAY_TARGET) / SLOW_DECAY_PCT
    deltas = jnp.abs(jnp.linspace(min_decay, max_decay, hy_w, dtype=F32))
    dl = jnp.tile(deltas, HY_ORDER)[:, None]
    nblk = N // tr
    halfb = nblk // 2
    full = lambda i: (0, 0)
    h2 = 2 * hid
    return pl.pallas_call(
        functools.partial(_filter_kernel, L=L),
        out_shape=(jax.ShapeDtypeStruct((n1, cw, LANES), F32), jax.ShapeDtypeStruct((cw, LANES), F32)),
        grid=(nblk,),
        in_specs=[pl.BlockSpec((tr, LANES), lambda i: (i, 0)),
                  pl.BlockSpec((tr // LANES, 1, LANES), lambda i: (i, 0, 0)),
                  pl.BlockSpec((2 * LANES, h2), full), pl.BlockSpec((1, h2), full),
                  pl.BlockSpec((h2, h2), full), pl.BlockSpec((1, h2), full),
                  pl.BlockSpec((h2, h2), full), pl.BlockSpec((1, h2), full),
                  pl.BlockSpec((1, h2), full),
                  pl.BlockSpec((1, cw, h2), lambda i: (i // halfb, 0, 0)),
                  pl.BlockSpec((1, cw, h2), lambda i: (i // halfb, 0, 0)),
                  pl.BlockSpec((cw, 1), full)],
        out_specs=(pl.BlockSpec((tr // LANES, cw, LANES), lambda i: (i, 0, 0)), pl.BlockSpec((cw, LANES), full)),
        compiler_params=_params("arbitrary"),
        name="hyena_filter",
    )(z, tl, w1, two(p['flt_b1']), w2, two(p['flt_b2']), w3, two(p['flt_b3']), two(p['flt_freq']), woa, wob, dl)


def _fft_consts(L, planes):
    N = 2 * L
    n1 = N // FFT_N2
    n1h = n1 // 2
    k1 = np.arange(n1)[:, None]
    th = 2.0 * np.pi * k1 * np.arange(n1)[None, :] / n1
    c, s = np.cos(th), np.sin(th)
    ch, sh = c[:, :n1h], s[:, :n1h]
    if planes == 1:
        m1 = np.concatenate([ch, -sh], axis=0)
        m3 = np.concatenate([ch.T, -sh.T], axis=1) / N
    else:
        m1 = np.block([[ch, sh], [-sh, ch]])
        m3 = np.block([[ch.T, -sh.T], [sh.T, ch.T]]) / N
    m1_full = np.concatenate([c, -s], axis=0)
    ph = 2.0 * np.pi * k1 * np.arange(FFT_N2)[None, :] / N
    twr, twi = np.cos(ph), -np.sin(ph)
    a = 2.0 * np.pi * np.arange(FFT_N2)[:, None] * np.arange(FFT_N2)[None, :] / FFT_N2
    ca, sa = np.cos(a), np.sin(a)
    rt = np.block([[ca, -sa], [sa, ca]])
    rit = np.block([[ca, sa], [-sa, ca]])
    bf = lambda m: jnp.asarray(np.asarray(m, np.float32).astype(BF16))
    return dict(n1=n1, n1h=n1h, m1=bf(m1), m3=bf(m3), m1_full=bf(m1_full),
                twr=jnp.asarray(twr, F32), twi=jnp.asarray(twi, F32), rt=bf(rt), rit=bf(rit))


CONV_CH = 16


def _load_channels(ref, lead):
    return jnp.concatenate([ref[lead + (slice(None), c, slice(None))] for c in range(CONV_CH)], axis=1)


def _fwd_lane_dft(ar, ai, twr, twi, rt):
    pieces = []
    for c in range(CONV_CH):
        r = ar[:, c * LANES:(c + 1) * LANES]
        i = ai[:, c * LANES:(c + 1) * LANES]
        pieces.append(jnp.concatenate([r * twr - i * twi, r * twi + i * twr], axis=1))
    return _dot(jnp.concatenate(pieces, axis=0).astype(BF16), rt)


def _spectrum_kernel(k_ref, ssq_ref, m1_ref, twr_ref, twi_ref, rt_ref, o_ref):
    n1 = twr_ref.shape[0]
    a = _dot(m1_ref[...], _load_channels(k_ref, ()).astype(BF16))
    x = _fwd_lane_dft(a[:n1], a[n1:], twr_ref[...], twi_ref[...], rt_ref[...])
    scale = lax.rsqrt(jnp.sum(ssq_ref[...], axis=1, keepdims=True) + EPS)
    for c in range(CONV_CH):
        o_ref[0, :, c * 2 * LANES:(c + 1) * 2 * LANES] = x[c * n1:(c + 1) * n1] * scale[c:c + 1, :]


def _filter_spectrum(k3, ssq, fc, hy_w):
    n1 = fc['n1']
    ct = CONV_CH
    ntile = hy_w // ct
    full = lambda o, j: (0, 0)
    return pl.pallas_call(
        _spectrum_kernel,
        out_shape=jax.ShapeDtypeStruct((HY_ORDER, n1, hy_w * 2 * LANES), F32),
        grid=(HY_ORDER, ntile),
        in_specs=[pl.BlockSpec((n1, ct, LANES), lambda o, j: (0, o * ntile + j, 0)),
                  pl.BlockSpec((ct, LANES), lambda o, j: (o * ntile + j, 0)),
                  pl.BlockSpec((2 * n1, n1), full),
                  pl.BlockSpec((n1, LANES), full), pl.BlockSpec((n1, LANES), full),
                  pl.BlockSpec((2 * LANES, 2 * LANES), full)],
        out_specs=pl.BlockSpec((1, n1, ct * 2 * LANES), lambda o, j: (o, 0, j)),
        compiler_params=_params("arbitrary", "arbitrary"),
        name="filter_spectrum",
    )(k3, ssq, fc['m1_full'], fc['twr'], fc['twi'], fc['rt'])


def _longconv_kernel(z_ref, g_ref, kf_ref, b_ref, m1_ref, twr_ref, twi_ref, rt_ref, rit_ref, m3_ref, o_ref):
    ct = CONV_CH
    n1 = twr_ref.shape[0]
    planes, n1h = z_ref.shape[1], z_ref.shape[2]
    z = jnp.concatenate([_load_channels(z_ref, (0, p)) for p in range(planes)], axis=0)
    a = _dot(m1_ref[...], z.astype(BF16))
    twr = twr_ref[...]
    twi = twi_ref[...]
    x = _fwd_lane_dft(a[:n1], a[n1:], twr, twi, rt_ref[...])
    ys = []
    for c in range(ct):
        xc = x[c * n1:(c + 1) * n1]
        kc = kf_ref[0, :, c * 2 * LANES:(c + 1) * 2 * LANES]
        xr, xi = xc[:, :LANES], xc[:, LANES:]
        kr, ki = kc[:, :LANES], kc[:, LANES:]
        ys.append(jnp.concatenate([xr * kr - xi * ki, xr * ki + xi * kr], axis=1))
    bp = _dot(jnp.concatenate(ys, axis=0).astype(BF16), rit_ref[...])
    brs, bis = [], []
    for c in range(ct):
        bc = bp[c * n1:(c + 1) * n1]
        r, i = bc[:, :LANES], bc[:, LANES:]
        brs.append(r * twr + i * twi)
        bis.append(i * twr - r * twi)
    bst = jnp.concatenate([jnp.concatenate(brs, axis=1), jnp.concatenate(bis, axis=1)], axis=0)
    yt = _dot(m3_ref[...], bst.astype(BF16))
    for p in range(planes):
        rows = slice(p * n1h, (p + 1) * n1h)
        for c in range(ct):
            lanes = slice(c * LANES, (c + 1) * LANES)
            o_ref[0, p, :, c, :] = g_ref[0, p, :, c, :] * (yt[rows, lanes] + b_ref[0, 0:1, c:c + 1] * z[rows, lanes])


def _longconv(z5, g5, kf, order, bias_t, fc):
    npair, planes, n1h, ch, _ = z5.shape
    n1 = fc['n1']
    ct = CONV_CH
    ntile = ch // ct
    full = lambda p, j: (0, 0)
    m1r, m1c = 2 * n1, planes * n1h
    blk = pl.BlockSpec((1, planes, n1h, ct, LANES), lambda p, j: (p, 0, 0, j, 0))
    return pl.pallas_call(
        _longconv_kernel,
        out_shape=jax.ShapeDtypeStruct(z5.shape, F32),
        grid=(npair, ntile),
        in_specs=[blk, blk,
                  pl.BlockSpec((1, n1, ct * 2 * LANES), lambda p, j: (order, 0, j)),
                  pl.BlockSpec((1, 8, ct), lambda p, j: (order * ntile + j, 0, 0)),
                  pl.BlockSpec((m1r, m1c), full),
                  pl.BlockSpec((n1, LANES), full), pl.BlockSpec((n1, LANES), full),
                  pl.BlockSpec((2 * LANES, 2 * LANES), full), pl.BlockSpec((2 * LANES, 2 * LANES), full),
                  pl.BlockSpec((m1c, m1r), full)],
        out_specs=blk,
        compiler_params=_params("arbitrary", "arbitrary"),
        name="hyena_longconv",
    )(z5, g5, kf, bias_t, fc['m1'], fc['twr'], fc['twi'], fc['rt'], fc['rit'], fc['m3'])


def _attn_kernel(sink_ref, q_ref, kp_ref, km_ref, kn_ref, vp_ref, vm_ref, vn_ref, o_ref, *, L, tq):
    i = pl.program_id(1)
    t0 = i * tq
    tk = tq + 2 * WINDOW
    kk = jnp.concatenate([kp_ref[...], km_ref[...], kn_ref[...]], axis=0)
    vv = jnp.concatenate([vp_ref[...], vm_ref[...], vn_ref[...]], axis=0)
    qpos = t0 + lax.broadcasted_iota(I32, (tq, tk), 0)
    kpos = t0 - WINDOW + lax.broadcasted_iota(I32, (tq, tk), 1)
    valid = (jnp.abs(kpos - qpos) <= WINDOW) & (kpos >= 0) & (kpos < L)
    for h in range(N_HEADS):
        s = _dot_nt(q_ref[:, h * LANES:(h + 1) * LANES], kk)
        s = jnp.where(valid, s, -1e30)
        sink = sink_ref[h]
        m = jnp.maximum(jnp.max(s, axis=-1, keepdims=True), sink)
        p = jnp.exp(s - m)
        denom = jnp.sum(p, axis=-1, keepdims=True) + jnp.exp(sink - m)
        o_ref[:, h * LANES:(h + 1) * LANES] = _dot((p / denom).astype(BF16), vv).astype(BF16)


def _attention(q, k, v, sink, B, L):
    T = q.shape[0]
    tq = 256
    nq = L // tq
    r = tq // WINDOW
    nb = L // WINDOW
    kv_w = k.shape[1]
    kp = pl.BlockSpec((WINDOW, kv_w), lambda b, i: (b * nb + jnp.maximum(i * r - 1, 0), 0))
    km = pl.BlockSpec((tq, kv_w), lambda b, i: (b * nq + i, 0))
    kn = pl.BlockSpec((WINDOW, kv_w), lambda b, i: (b * nb + jnp.minimum((i + 1) * r, nb - 1), 0))
    qs = pl.BlockSpec((tq, N_HEADS * LANES), lambda b, i: (b * nq + i, 0))
    return pl.pallas_call(
        functools.partial(_attn_kernel, L=L, tq=tq),
        out_shape=jax.ShapeDtypeStruct((T, N_HEADS * LANES), BF16),
        grid=(B, nq),
        in_specs=[pl.BlockSpec(memory_space=pltpu.SMEM), qs, kp, km, kn, kp, km, kn],
        out_specs=qs,
        compiler_params=_params("arbitrary", "arbitrary"),
        name="band_attention",
    )(sink, q, k, k, k, v, v, v)


def _merge_kernel(x_ref, z_ref, a_ref, gs_ref, mod_ref, why_ref, wat_ref, wo_ref, g2_ref, rwh_ref, rwl_ref,
                  xo_ref, h2_ref, aff_ref):
    d = x_ref.shape[1]
    mod = mod_ref[0]
    z = jnp.concatenate([z_ref[r].T for r in range(z_ref.shape[0])], axis=0)
    y_hy = _dot(z.astype(BF16), why_ref[...])
    y_at = _dot(a_ref[...], wat_ref[...])
    gs = gs_ref[...]
    merged = gs[:, :d] * y_hy + gs[:, d:] * y_at
    x = x_ref[...] + mod[2:3] * _dot(merged.astype(BF16), wo_ref[...])
    xo_ref[...] = x
    ms = jnp.mean(x * x, axis=-1, keepdims=True)
    h = (x * lax.rsqrt(ms + EPS) * g2_ref[...]) * (1.0 + mod[4:5]) + mod[3:4]
    for j in range(d // LANES):
        h2_ref[:, j, :] = h[:, j * LANES:(j + 1) * LANES]
    hh, hl = _split(h)
    rh, rl = rwh_ref[...], rwl_ref[...]
    logits = _dot_nt(rh, hh) + _dot_nt(rl, hh) + _dot_nt(rh, hl)
    m = jnp.max(logits, axis=0, keepdims=True)
    e = jnp.exp(logits - m)
    aff_ref[...] = e / jnp.sum(e, axis=0, keepdims=True)


def _merge(x, z, att, gs, mod_l, b0, L, w_hy, w_at_x, w_o, norm2_g, rw_hi, rw_lo):
    T, D = x.shape
    tm = 512
    nper = L // tm
    E = rw_hi.shape[0]
    full = lambda i: (0, 0)
    row = lambda w: pl.BlockSpec((tm, w), lambda i: (i, 0))
    return pl.pallas_call(
        _merge_kernel,
        out_shape=(jax.ShapeDtypeStruct((T, D), F32), jax.ShapeDtypeStruct((T, D // LANES, LANES), F32),
                   jax.ShapeDtypeStruct((E, T), F32)),
        grid=(T // tm,),
        in_specs=[row(D), pl.BlockSpec((tm // LANES,) + z.shape[1:], lambda i: (i, 0, 0)), row(att.shape[1]), row(gs.shape[1]),
                  pl.BlockSpec((1, 6, D), lambda i: (b0 + i // nper, 0, 0)),
                  pl.BlockSpec(w_hy.shape, full), pl.BlockSpec(w_at_x.shape, full), pl.BlockSpec(w_o.shape, full),
                  pl.BlockSpec((1, D), full), pl.BlockSpec((E, D), full), pl.BlockSpec((E, D), full)],
        out_specs=(row(D), pl.BlockSpec((tm, D // LANES, LANES), lambda i: (i, 0, 0)), pl.BlockSpec((E, tm), lambda i: (0, i))),
        compiler_params=_params("arbitrary"),
        name="merge_router",
    )(x, z, att, gs, mod_l, w_hy, w_at_x, w_o, norm2_g, rw_hi, rw_lo)


def _cumsum_rows(mb, u, ones, ls):
    rc = _dot(mb, u)
    totb = _dot(mb, ones)
    offs = _dot(ls, totb.astype(BF16))
    return rc, totb, offs


def _topk_kernel(aff_ref, u_ref, ut_ref, ones_ref, ls_ref, idx_ref, gate_ref, dst_ref, cum_ref, cnt_ref,
                 thr_ref, sel_ref, rank_ref, *, cap):
    E, R, _ = aff_ref.shape
    bits = pltpu.bitcast(aff_ref[...], I32)

    def search(i, cur):
        cand = cur | jnp.left_shift(jnp.int32(1), 30 - i)
        ge = jnp.where(bits >= cand, 1.0, 0.0)
        c = jnp.sum(jnp.sum(ge, axis=2, keepdims=True), axis=1, keepdims=True)
        return jnp.where(c >= cap, cand, cur)

    thr = lax.fori_loop(0, 31, search, jnp.zeros((E, 1, 1), I32))
    thr_ref[...] = jnp.broadcast_to(thr, thr_ref.shape)
    u = u_ref[...]
    ut = ut_ref[...]
    ones = ones_ref[...]
    ls = ls_ref[...]

    def select(e, carry):
        cnt_run, cum = carry
        b = pltpu.bitcast(aff_ref[e], I32)
        t = thr_ref[e][0:1, :]
        gt = b > t
        eq = b == t
        ngt = jnp.sum(jnp.sum(jnp.where(gt, 1.0, 0.0), axis=1, keepdims=True), axis=0, keepdims=True)
        need = cap - ngt
        erc, _, eoffs = _cumsum_rows(jnp.where(eq, 1.0, 0.0).astype(BF16), u, ones, ls)
        sel = jnp.where(gt | (eq & ((erc + eoffs) <= need)), 1.0, 0.0)
        rc, _, offs = _cumsum_rows(sel.astype(BF16), u, ones, ls)
        sel_ref[e] = sel
        rank_ref[e] = cnt_run
        return cnt_run + sel, cum + (rc + offs - sel)

    zero = jnp.zeros((R, LANES), F32)
    cnt, cum = lax.fori_loop(0, E, select, (zero, zero))
    cnt_ref[...] = cnt.astype(I32)
    cum_ref[...] = cum.astype(I32)
    s_row = lax.broadcasted_iota(I32, (1, cap), 1).astype(F32)
    r_col = lax.broadcasted_iota(I32, (R, 1), 0).astype(F32)
    c_col = lax.broadcasted_iota(I32, (LANES, 1), 0).astype(F32)

    def pick(vt_pieces, ohb, hit):
        acc = _dot(vt_pieces[0], ohb)
        for piece in vt_pieces[1:]:
            acc = acc + _dot(piece, ohb)
        return jnp.sum(jnp.where(hit, acc, 0.0), axis=0, keepdims=True)

    def compact(e, carry):
        sel = sel_ref[e]
        _, totb, offs = _cumsum_rows(sel.astype(BF16), u, ones, ls)
        offs_col = offs[:, 0:1]
        end_col = offs_col + totb[:, 0:1]
        oht = (offs_col <= s_row) & (s_row < end_col)
        row_s = jnp.sum(jnp.where(oht, r_col, 0.0), axis=0, keepdims=True)
        offs_s = jnp.sum(jnp.where(oht, offs_col, 0.0), axis=0, keepdims=True)
        ohb = jnp.where(oht, 1.0, 0.0).astype(BF16)
        rct = _dot(ut, sel.T.astype(BF16))
        rcs = _dot(rct.astype(BF16), ohb)
        col_s = jnp.sum(jnp.where(rcs <= (s_row - offs_s), 1.0, 0.0), axis=0, keepdims=True)
        hit = c_col == col_s
        idx_ref[pl.ds(e, 1), :] = (row_s * LANES + col_s).astype(I32)
        at = aff_ref[e].T
        p0 = at.astype(BF16)
        r1 = at - p0.astype(F32)
        p1 = r1.astype(BF16)
        p2 = (r1 - p1.astype(F32)).astype(BF16)
        gate_ref[pl.ds(e, 1), :] = pick((p0, p1, p2), ohb, hit)
        d = (cum + rank_ref[e]).T
        d_hi = (d * (1.0 / 256.0)).astype(I32).astype(F32)
        d_lo = d - 256.0 * d_hi
        dst = 256.0 * pick((d_hi.astype(BF16),), ohb, hit) + pick((d_lo.astype(BF16),), ohb, hit)
        dst_ref[pl.ds(e, 1), :] = dst.astype(I32)
        return carry

    lax.fori_loop(0, E, compact, 0)


def _topk(aff_t, cap):
    E, T = aff_t.shape
    R = T // LANES
    u = jnp.asarray(np.triu(np.ones((LANES, LANES))), BF16)
    ut = jnp.asarray(np.tril(np.ones((LANES, LANES))), BF16)
    ones = jnp.ones((LANES, LANES), BF16)
    ls = jnp.asarray(np.tril(np.ones((R, R)), -1), BF16)
    z2 = lambda: (0, 0)
    z3 = lambda: (0, 0, 0)
    slot = jax.ShapeDtypeStruct((E, cap), I32)
    tok = jax.ShapeDtypeStruct((R, LANES), I32)
    idx, gate, dst, cum, cnt = pl.pallas_call(
        functools.partial(_topk_kernel, cap=cap),
        out_shape=(slot, jax.ShapeDtypeStruct((E, cap), F32), slot, tok, tok),
        in_specs=[pl.BlockSpec((E, R, LANES), z3), pl.BlockSpec((LANES, LANES), z2), pl.BlockSpec((LANES, LANES), z2),
                  pl.BlockSpec((LANES, LANES), z2), pl.BlockSpec((R, R), z2)],
        out_specs=(pl.BlockSpec((E, cap), z2), pl.BlockSpec((E, cap), z2), pl.BlockSpec((E, cap), z2),
                   pl.BlockSpec((R, LANES), z2), pl.BlockSpec((R, LANES), z2)),
        scratch_shapes=[pltpu.VMEM((E, 8, LANES), I32), pltpu.VMEM((E, R, LANES), F32),
                        pltpu.VMEM((E, R, LANES), F32)],
        compiler_params=pltpu.CompilerParams(vmem_limit_bytes=VMEM_LIMIT),
        name="expert_topk",
    )(aff_t.reshape(E, R, LANES), u, ut, ones, ls)
    return idx, gate, dst, cum.reshape(T), cnt.reshape(T)


ROW_CHUNKS = 4
FF_TILE = 256


def _expert_kernel(src_ref, dst_ref, h_ref, wg_ref, wu_ref, wd_ref, gate_ref, out_ref,
                   xbuf, xb, acc, gsem, ssem, *, cap, per):
    e = pl.program_id(0)
    f = pl.program_id(1)
    ne = pl.num_programs(0)
    nf = pl.num_programs(1)
    slot = e % 2
    other = 1 - slot
    sched = nf * per

    def gather_row(entry, r):
        pltpu.make_async_copy(h_ref.at[pl.ds(src_ref[entry], 1)], xbuf.at[pl.ds(r, 1)], gsem).start()

    def scatter_row(src_slot, entry, r):
        pltpu.make_async_copy(acc.at[src_slot, pl.ds(r, 1), :], out_ref.at[pl.ds(dst_ref[entry], 1), :],
                              ssem.at[src_slot]).start()

    def schedule_done(sem):
        pltpu.make_async_copy(h_ref.at[pl.ds(0, sched)], xbuf, sem).wait()

    @pl.when((e == 0) & (f == 0))
    def _():
        def body(s, c):
            gather_row(s, s)
            return c

        lax.fori_loop(0, sched, body, 0, unroll=8)
        acc[...] = jnp.zeros(acc.shape, F32)

    @pl.when((e > 0) & (f == 0))
    def _():
        schedule_done(ssem.at[slot])
        acc[slot, pl.ds(0, cap), :] = jnp.zeros((cap, acc.shape[2]), F32)

    @pl.when(f == 0)
    def _():
        schedule_done(gsem)
        xb[...] = jnp.concatenate([xbuf[pl.ds(0, cap), j, :] for j in range(xbuf.shape[1])], axis=1).astype(BF16)

    wg = wg_ref[0, 0].astype(BF16)
    wu = wu_ref[0, 0].astype(BF16)
    wd = wd_ref[0, 0].astype(BF16)
    gbase = (jnp.minimum(e + 1, ne - 1) * nf + f) * per
    sbase = (e * nf + f) * per
    row0 = f * per
    share = per // ROW_CHUNKS
    chunk = cap // ROW_CHUNKS
    for c in range(ROW_CHUNKS):
        for s in range(c * share, (c + 1) * share):
            gather_row(gbase + s, row0 + s)
            scatter_row(other, sbase + s, row0 + s)
        rows = pl.ds(c * chunk, chunk)
        x = xb[rows, :]
        g = _dot(x, wg)
        u = _dot(x, wu)
        hid = (g * jax.nn.sigmoid(g) * u).astype(BF16)
        acc[slot, rows, :] += _dot(hid, wd)

    @pl.when(f == nf - 1)
    def _():
        acc[slot, pl.ds(0, cap), :] = gate_ref[0] * acc[slot, pl.ds(0, cap), :]

    @pl.when((f == nf - 1) & (e == ne - 1))
    def _():
        def body(s, c):
            scatter_row(slot, ne * sched + s, s)
            return c

        lax.fori_loop(0, sched, body, 0, unroll=8)
        schedule_done(gsem)
        schedule_done(ssem.at[other])
        schedule_done(ssem.at[slot])


def _experts(h2, idx, dst, w_gate, w_up, w_down, layer, gate_col, cap):
    _, E, D, F = w_gate.shape
    ft = FF_TILE
    nf = F // ft
    per = -(-cap // (nf * 8 * ROW_CHUNKS)) * (8 * ROW_CHUNKS)
    sched = nf * per
    nr = E * cap
    assert E >= 2 and cap % ROW_CHUNKS == 0 and sched >= cap
    rows = jnp.arange(sched, dtype=I32)
    real = rows < cap
    rc = jnp.minimum(rows, cap - 1)
    src_tab = idx[:, rc]
    odd = (jnp.arange(E + 1, dtype=I32)[:, None] % 2) == 1
    spare = jnp.where(odd, nr + sched + (rows[None, :] - cap), nr + rows[None, :])
    dst_prev = jnp.concatenate([jnp.zeros((1, sched), I32), dst[:, rc]], axis=0)
    dst_tab = jnp.where(real[None, :] & (jnp.arange(E + 1)[:, None] > 0), dst_prev, spare)
    return pl.pallas_call(
        functools.partial(_expert_kernel, cap=cap, per=per),
        out_shape=jax.ShapeDtypeStruct((nr + 2 * sched - cap, D), F32),
        grid_spec=pltpu.PrefetchScalarGridSpec(
            num_scalar_prefetch=2,
            grid=(E, nf),
            in_specs=[pl.BlockSpec(memory_space=pl.ANY),
                      pl.BlockSpec((1, 1, D, ft), lambda e, f, i, d: (layer, e, 0, f)),
                      pl.BlockSpec((1, 1, D, ft), lambda e, f, i, d: (layer, e, 0, f)),
                      pl.BlockSpec((1, 1, ft, D), lambda e, f, i, d: (layer, e, f, 0)),
                      pl.BlockSpec((1, cap, 1), lambda e, f, i, d: (e, 0, 0))],
            out_specs=pl.BlockSpec(memory_space=pl.ANY),
            scratch_shapes=[pltpu.VMEM((sched, D // LANES, LANES), F32), pltpu.VMEM((cap, D), BF16),
                            pltpu.VMEM((2, sched, D), F32),
                            pltpu.SemaphoreType.DMA, pltpu.SemaphoreType.DMA((2,))]),
        compiler_params=_params("arbitrary", "arbitrary"),
        name="expert_ffn",
    )(src_tab.reshape(-1), dst_tab.reshape(-1), h2, w_gate, w_up, w_down, gate_col)


COMBINE_ROWS = 512
COMBINE_TOKENS = 256


def _combine_kernel(tile_ref, blk_ref, flag_ref, c_ref, x_ref, mod_ref, tab_ref, eye_ref, o_ref):
    j = pl.program_id(0)
    flag = flag_ref[j]
    first = (flag & 1) != 0
    last = (flag & 2) != 0
    skip = (flag & 4) != 0
    rb = c_ref.shape[0]
    cols = _dot_nt(eye_ref[...], tab_ref[0].astype(BF16))
    start = cols[:, 0:1] * 256.0 + cols[:, 1:2]
    count = cols[:, 2:3]
    rabs = (blk_ref[j] * rb + lax.broadcasted_iota(I32, (1, rb), 1)).astype(F32)
    seg = jnp.where((start <= rabs) & (rabs < start + count), 1.0, 0.0).astype(BF16)
    hi, lo = _split(c_ref[...])
    part = _dot(seg, hi) + _dot(seg, lo)

    @pl.when(first)
    def _():
        o_ref[...] = part

    @pl.when(jnp.logical_not(first | skip))
    def _():
        o_ref[...] += part

    @pl.when(last & jnp.logical_not(skip))
    def _():
        o_ref[...] = x_ref[...] + mod_ref[0][5:6] * o_ref[...]


def _combine(x, mod_l, b0, L, contrib, cum, cnt):
    T, D = x.shape
    tt = COMBINE_TOKENS
    R = T // tt
    rb = COMBINE_ROWS
    nr = CAPACITY_FACTOR * T
    nb = nr // rb
    nper = L // tt
    nitem = R + nb
    lo = cum[::tt]
    hi = jnp.concatenate([lo[1:], jnp.full((1,), nr, I32)])
    b_lo = jnp.minimum(lo // rb, nb - 1)
    b_hi = jnp.minimum(jnp.maximum((hi - 1) // rb, b_lo), nb - 1)
    n = b_hi - b_lo + 1
    cn = jnp.cumsum(n)
    start = cn - n
    j = jnp.arange(nitem, dtype=I32)
    valid = j < cn[-1]
    ti = jnp.minimum(jnp.searchsorted(cn, j, side='right').astype(I32), R - 1)
    blk = jnp.where(valid, b_lo[ti] + (j - start[ti]), b_hi[R - 1])
    flag = jnp.where(valid, (j == start[ti]) * 1 + (j == cn[ti] - 1) * 2, 4).astype(I32)
    tab = jnp.stack([(cum >> 8).astype(F32), (cum & 255).astype(F32), cnt.astype(F32)], axis=0)
    tab = jnp.concatenate([tab, jnp.zeros((5, T), F32)], axis=0).reshape(8, R, tt).transpose(1, 0, 2)
    eye = jnp.eye(tt, dtype=BF16)
    return pl.pallas_call(
        _combine_kernel,
        out_shape=jax.ShapeDtypeStruct((T, D), F32),
        grid_spec=pltpu.PrefetchScalarGridSpec(
            num_scalar_prefetch=3,
            grid=(nitem,),
            in_specs=[pl.BlockSpec((rb, D), lambda j, t, b, f: (b[j], 0)),
                      pl.BlockSpec((tt, D), lambda j, t, b, f: (t[j], 0)),
                      pl.BlockSpec((1, 6, D), lambda j, t, b, f: (b0 + t[j] // nper, 0, 0)),
                      pl.BlockSpec((1, 8, tt), lambda j, t, b, f: (t[j], 0, 0)),
                      pl.BlockSpec((tt, tt), lambda j, t, b, f: (0, 0))],
            out_specs=pl.BlockSpec((tt, D), lambda j, t, b, f: (t[j], 0))),
        compiler_params=_params("arbitrary"),
        name="moe_combine",
    )(ti, blk, flag, contrib, x, mod_l, tab, eye)


def _layer(x, B, L, b0, mod_l, p, kf, fc, tabs):
    hy_w = p['hy_bias'].shape[1]
    att_w = N_HEADS * HEAD_DIM
    kv_w = N_KV_HEADS * HEAD_DIM
    planes = 2 if B % 2 == 0 else 1
    z, ga, gb, q, k, v, gs = _inproj(x, mod_l, b0, L, p['norm1_g'][None], p['w_in_bf'], p['hy_conv_w'], p['hy_conv_b'],
                                     tabs['cos'], tabs['sin'], tabs['qg'], tabs['kg'], tabs['blk'], hy_w, att_w, kv_w)
    shape5 = (B // planes, planes, L // LANES, hy_w, LANES)
    z = z.reshape(shape5)
    gates = (ga.reshape(shape5), gb.reshape(shape5))
    for o in range(HY_ORDER):
        z = _longconv(z, gates[o], kf, o, p['bias_t'], fc)
    z = z.reshape(B * L // LANES, hy_w, LANES)
    att = _attention(q, k, v, p['sink_logit'], B, L)
    x, h2, aff_t = _merge(x, z, att, gs, mod_l, b0, L, p['w_hy_bf'], p['w_at_x'], p['w_o_bf'], p['norm2_g'][None],
                          p['rw_hi'], p['rw_lo'])
    T = x.shape[0]
    cap = CAPACITY_FACTOR * T // N_EXPERTS
    idx, gate, dst, cum, cnt = _topk(aff_t, cap)
    contrib = _experts(h2, idx, dst, p['exp_w_gate'], p['exp_w_up'], p['exp_w_down'], p['layer'],
                       gate.reshape(N_EXPERTS, cap, 1), cap)
    return _combine(x, mod_l, b0, L, contrib, cum, cnt)


def _rope_tables(L):
    inv = 1.0 / (ROPE_THETA ** (jnp.arange(0, HEAD_DIM, 2, dtype=F32) / HEAD_DIM))
    ang = jnp.arange(L, dtype=F32)[:, None] * inv[None, :]
    cos, sin = jnp.cos(ang), jnp.sin(ang)
    cos_t = jnp.tile(cos, (1, LANES // (HEAD_DIM // 2)))
    sin_t = jnp.tile(jnp.concatenate([-sin, sin], axis=1), (1, LANES // HEAD_DIM))
    return cos_t, sin_t


def _filter_spectrum_for(L, p, fc):
    hy_w = p['hy_bias'].shape[1]
    k3, ssq = _filter_rows(L, p)
    return _filter_spectrum(k3, ssq, fc, hy_w)


def kernel(x_prompt, x_sample, c_prompt, c_sample, ada_w, ada_b, norm1_g, norm2_g, w_in, hy_conv_w, hy_conv_b,
           flt_w1, flt_b1, flt_w2, flt_b2, flt_w3, flt_b3, flt_freq, flt_w_out, hy_bias, q_norm_g, k_norm_g,
           sink_logit, w_hy_proj, w_at_proj, w_o, router_w, exp_w_gate, exp_w_up, exp_w_down):
    depth = ada_w.shape[0]
    Bp, Lp, D = x_prompt.shape
    Bs, Ls, _ = x_sample.shape
    nb = Bp + Bs
    c_all = jnp.concatenate([c_prompt, c_sample, jnp.zeros(((-nb) % 8, D), F32)], axis=0)
    mod = _adaln(c_all, ada_w, ada_b).reshape(depth, c_all.shape[0], 6, D)
    hy_w = hy_bias.shape[2]
    att_w = N_HEADS * HEAD_DIM
    groups = [dict(B=Bp, L=Lp, b0=0, x=x_prompt.reshape(Bp * Lp, D)),
              dict(B=Bs, L=Ls, b0=Bp, x=x_sample.reshape(Bs * Ls, D))]
    blk = jnp.asarray(np.kron(np.eye(LANES // HEAD_DIM), np.ones((HEAD_DIM, HEAD_DIM))), BF16)
    for g in groups:
        g['fc'] = _fft_consts(g['L'], 2 if g['B'] % 2 == 0 else 1)
        cos_t, sin_t = _rope_tables(g['L'])
        g['tabs'] = dict(cos=cos_t, sin=sin_t, blk=blk)
    heads_per_kv = N_HEADS // N_KV_HEADS
    for l in range(depth):
        w_at_x = jnp.zeros((N_HEADS, LANES, D), BF16)
        w_at_h = w_at_proj[l].astype(BF16).reshape(N_HEADS, HEAD_DIM, D)
        for h in range(N_HEADS):
            off = (h // heads_per_kv) * HEAD_DIM
            w_at_x = w_at_x.at[h, off:off + HEAD_DIM].set(w_at_h[h])
        rw_hi, rw_lo = _split(router_w[l].T)
        p = dict(norm1_g=norm1_g[l], norm2_g=norm2_g[l], w_in_bf=w_in[l].astype(BF16),
                 hy_conv_w=hy_conv_w[l], hy_conv_b=hy_conv_b[l],
                 flt_w1=flt_w1[l], flt_b1=flt_b1[l], flt_w2=flt_w2[l], flt_b2=flt_b2[l], flt_w3=flt_w3[l],
                 flt_b3=flt_b3[l], flt_freq=flt_freq[l], flt_w_out=flt_w_out[l], hy_bias=hy_bias[l],
                 bias_t=jnp.broadcast_to(hy_bias[l].reshape(-1, 1, CONV_CH), (HY_ORDER * hy_w // CONV_CH, 8, CONV_CH)),
                 sink_logit=sink_logit[l], w_hy_bf=w_hy_proj[l].astype(BF16),
                 w_at_x=w_at_x.reshape(N_HEADS * LANES, D), w_o_bf=w_o[l].astype(BF16),
                 rw_hi=rw_hi, rw_lo=rw_lo,
                 layer=l, exp_w_gate=exp_w_gate, exp_w_up=exp_w_up, exp_w_down=exp_w_down)
        qg = jnp.tile(q_norm_g[l], LANES // HEAD_DIM)[None]
        kg = jnp.tile(k_norm_g[l], LANES // HEAD_DIM)[None]
        for g in groups:
            kf = _filter_spectrum_for(g['L'], p, g['fc'])
            tabs = dict(g['tabs'], qg=qg, kg=kg)
            g['x'] = _layer(g['x'], g['B'], g['L'], g['b0'], mod[l], p, kf, g['fc'], tabs)
    return (groups[0]['x'].reshape(Bp, Lp, D), groups[1]['x'].reshape(Bs, Ls, D))
```

```python
import functools
import math

import numpy as np
import jax
import jax.numpy as jnp
from jax import lax
from jax.experimental import pallas as pl
from jax.experimental.pallas import tpu as pltpu

F32 = jnp.float32
BF16 = jnp.bfloat16
I32 = jnp.int32

EPS = 1e-6
HEAD_DIM = 64
N_HEADS = 8
N_KV_HEADS = 2
WINDOW = 128
ROPE_THETA = 10000.0
HY_ORDER = 2
FILTER_EMB = 33
DECAY_TARGET = 0.01
FAST_DECAY_PCT = 0.3
SLOW_DECAY_PCT = 1.5
N_EXPERTS = 16
CAPACITY_FACTOR = 2
LANES = 128
FFT_N2 = LANES
VMEM_LIMIT = 52 << 20


def _params(*sem):
    return pltpu.CompilerParams(dimension_semantics=sem, vmem_limit_bytes=VMEM_LIMIT)


def _dot(a, b):
    return jnp.dot(a, b, preferred_element_type=F32)


def _dot_nt(a, b):
    return lax.dot_general(a, b, (((1,), (1,)), ((), ())), preferred_element_type=F32)


def _split(x):
    hi = x.astype(BF16)
    lo = (x - hi.astype(F32)).astype(BF16)
    return hi, lo


def _dot3(a_hi, a_lo, b_hi, b_lo):
    return _dot(a_hi, b_hi) + _dot(a_lo, b_hi) + _dot(a_hi, b_lo)


def _ada_kernel(c_ref, w_ref, b_ref, o_ref):
    c = c_ref[...]
    s = c * jax.nn.sigmoid(c)
    o_ref[0] = _dot(s.astype(BF16), w_ref[0].astype(BF16)) + b_ref[0]


def _adaln(c_all, ada_w, ada_b):
    depth, d, n = ada_w.shape
    tn = n // 4
    return pl.pallas_call(
        _ada_kernel,
        out_shape=jax.ShapeDtypeStruct((depth, c_all.shape[0], n), F32),
        grid=(depth, n // tn),
        in_specs=[pl.BlockSpec(c_all.shape, lambda l, j: (0, 0)),
                  pl.BlockSpec((1, d, tn), lambda l, j: (l, 0, j)),
                  pl.BlockSpec((1, 1, tn), lambda l, j: (l, 0, j))],
        out_specs=pl.BlockSpec((1, c_all.shape[0], tn), lambda l, j: (l, 0, j)),
        compiler_params=_params("arbitrary", "arbitrary"),
        name="adaln",
    )(c_all, ada_w, ada_b.reshape(depth, 1, n))


def _head_norm_rope(x, gain, cos, sin, blk_hi):
    sq = x * x
    hi, lo = _split(sq)
    ssum = _dot(hi, blk_hi) + _dot(lo, blk_hi)
    xn = x * lax.rsqrt(ssum * (1.0 / HEAD_DIM) + EPS) * gain
    lane = lax.broadcasted_iota(I32, x.shape, 1)
    first_half = (lane % HEAD_DIM) < (HEAD_DIM // 2)
    rot = jnp.where(first_half, pltpu.roll(xn, LANES - HEAD_DIM // 2, 1), pltpu.roll(xn, HEAD_DIM // 2, 1))
    return xn * cos + rot * sin


def _inproj_kernel(x_ref, xp_ref, xn_ref, mod_ref, g_ref, w_ref, cw_ref, cb_ref, cos_ref, sin_ref, qg_ref, kg_ref, blk_ref,
                   z_ref, ga_ref, gb_ref, q_ref, k_ref, v_ref, gs_ref, *, hy_w, att_w, kv_w, nper):
    i = pl.program_id(0)
    tm = x_ref.shape[0]
    halo = xp_ref.shape[0]
    x = jnp.concatenate([xp_ref[...], x_ref[...], xn_ref[...]], axis=0)
    mod = mod_ref[0]
    ms = jnp.mean(x * x, axis=-1, keepdims=True)
    h = (x * lax.rsqrt(ms + EPS) * g_ref[...]) * (1.0 + mod[1:2]) + mod[0:1]
    hb_all = h.astype(BF16)
    hb = hb_all[halo:halo + tm]
    s0 = 3 * hy_w
    s1 = s0 + att_w
    s2 = s1 + kv_w
    s3 = s2 + kv_w
    hy = _dot(hb_all, w_ref[:, 0:s0])
    n_all = tm + 2 * halo
    row = lax.broadcasted_iota(I32, (tm, 1), 0)
    first = (i % nper) == 0
    last = (i % nper) == nper - 1
    up = jnp.where((row == 0) & first, 0.0, pltpu.roll(hy, 1, 0)[halo:halo + tm])
    dn = jnp.where((row == tm - 1) & last, 0.0, pltpu.roll(hy, n_all - 1, 0)[halo:halo + tm])
    y = up * cw_ref[0:1, :] + hy[halo:halo + tm] * cw_ref[1:2, :] + dn * cw_ref[2:3, :] + cb_ref[...]
    for r in range(tm // LANES):
        blk_y = y[r * LANES:(r + 1) * LANES, :]
        z_ref[r] = blk_y[:, 0:hy_w].T
        ga_ref[r] = blk_y[:, hy_w:2 * hy_w].T
        gb_ref[r] = blk_y[:, 2 * hy_w:3 * hy_w].T
    q = _dot(hb, w_ref[:, s0:s1])
    k = _dot(hb, w_ref[:, s1:s2])
    v_ref[...] = _dot(hb, w_ref[:, s2:s3]).astype(BF16)
    gs_ref[...] = jax.nn.sigmoid(_dot(hb, w_ref[:, s3:]))
    cos = cos_ref[...]
    sin = sin_ref[...]
    blk = blk_ref[...]
    k_ref[...] = _head_norm_rope(k, kg_ref[...], cos, sin, blk).astype(BF16)
    lane = lax.broadcasted_iota(I32, cos.shape, 1)
    low = lane < HEAD_DIM
    heads_per_kv = N_HEADS // N_KV_HEADS
    for ci in range(att_w // LANES):
        qr = _head_norm_rope(q[:, ci * LANES:(ci + 1) * LANES], qg_ref[...], cos, sin, blk) * (HEAD_DIM ** -0.5)
        qs = pltpu.roll(qr, HEAD_DIM, 1)
        kvh = (2 * ci) // heads_per_kv
        if kvh == 0:
            h0 = jnp.where(low, qr, 0.0)
            h1 = jnp.where(low, qs, 0.0)
        else:
            h0 = jnp.where(low, 0.0, qs)
            h1 = jnp.where(low, 0.0, qr)
        q_ref[:, (2 * ci) * LANES:(2 * ci + 1) * LANES] = h0.astype(BF16)
        q_ref[:, (2 * ci + 1) * LANES:(2 * ci + 2) * LANES] = h1.astype(BF16)


def _inproj(x, mod_l, b0, L, norm_g, w_in_bf, conv_w, conv_b, cos_t, sin_t, qg, kg, blk, hy_w, att_w, kv_w):
    T, D = x.shape
    tm = 512
    halo = 8
    nper = L // tm
    r8 = tm // halo
    nb8 = T // halo
    in_w = w_in_bf.shape[1]
    kern = functools.partial(_inproj_kernel, hy_w=hy_w, att_w=att_w, kv_w=kv_w, nper=nper)
    stream = jax.ShapeDtypeStruct((T // LANES, hy_w, LANES), F32)
    sspec = pl.BlockSpec((tm // LANES, hy_w, LANES), lambda i: (i, 0, 0))
    full = lambda i: (0, 0)
    return pl.pallas_call(
        kern,
        out_shape=(stream, stream, stream,
                   jax.ShapeDtypeStruct((T, N_HEADS * LANES), BF16),
                   jax.ShapeDtypeStruct((T, kv_w), BF16),
                   jax.ShapeDtypeStruct((T, kv_w), BF16),
                   jax.ShapeDtypeStruct((T, in_w - 3 * hy_w - att_w - 2 * kv_w), F32)),
        grid=(T // tm,),
        in_specs=[pl.BlockSpec((tm, D), lambda i: (i, 0)),
                  pl.BlockSpec((halo, D), lambda i: (jnp.maximum(i * r8 - 1, 0), 0)),
                  pl.BlockSpec((halo, D), lambda i: (jnp.minimum((i + 1) * r8, nb8 - 1), 0)),
                  pl.BlockSpec((1, 6, D), lambda i: (b0 + i // nper, 0, 0)),
                  pl.BlockSpec((1, D), full),
                  pl.BlockSpec((D, in_w), full),
                  pl.BlockSpec((3, 3 * hy_w), full),
                  pl.BlockSpec((1, 3 * hy_w), full),
                  pl.BlockSpec((tm, LANES), lambda i: (i % nper, 0)),
                  pl.BlockSpec((tm, LANES), lambda i: (i % nper, 0)),
                  pl.BlockSpec((1, LANES), full),
                  pl.BlockSpec((1, LANES), full),
                  pl.BlockSpec((LANES, LANES), full)],
        out_specs=(sspec, sspec, sspec,
                   pl.BlockSpec((tm, N_HEADS * LANES), lambda i: (i, 0)),
                   pl.BlockSpec((tm, kv_w), lambda i: (i, 0)),
                   pl.BlockSpec((tm, kv_w), lambda i: (i, 0)),
                   pl.BlockSpec((tm, in_w - 3 * hy_w - att_w - 2 * kv_w), lambda i: (i, 0))),
        compiler_params=_params("arbitrary"),
        name="inproj",
    )(x, x, x, mod_l, norm_g, w_in_bf, conv_w, conv_b.reshape(1, -1), cos_t, sin_t, qg, kg, blk)


def _filter_kernel(z_ref, tl_ref, w1_ref, b1_ref, w2_ref, b2_ref, w3_ref, b3_ref, fr_ref, woa_ref, wob_ref, dl_ref,
                   k_ref, ssq_ref, *, L):
    i = pl.program_id(0)
    z = z_ref[...]
    tr = z.shape[0]
    half = tr // 2
    fr = fr_ref[...]

    def lin(a, w_ref_, b_ref_):
        ah, al = _split(a)
        wh, wl = _split(w_ref_[...])
        return _dot3(ah, al, wh, wl) + b_ref_[...]

    zz = jnp.concatenate([z[:half], z[half:]], axis=1)
    h = jnp.sin(fr * lin(zz, w1_ref, b1_ref))
    h = jnp.sin(fr * lin(h, w2_ref, b2_ref))
    h = jnp.sin(fr * lin(h, w3_ref, b3_ref))
    hh, hl = _split(h)
    wts = (_split(woa_ref[0]), _split(wob_ref[0]))

    @pl.when(i == 0)
    def _():
        ssq_ref[...] = jnp.zeros_like(ssq_ref)

    lane = lax.broadcasted_iota(I32, (1, LANES), 1)
    per_half = half // LANES
    for r in range(tr // LANES):
        wh, wl = wts[r // per_half]
        rows = slice((r % per_half) * LANES, (r % per_half + 1) * LANES)
        kt = _dot_nt(wh, hh[rows]) + _dot_nt(wl, hh[rows]) + _dot_nt(wh, hl[rows])
        kt = kt * jnp.exp(-dl_ref[...] * tl_ref[r])
        kt = jnp.where((i * tr + r * LANES + lane) == L, 0.0, kt)
        k_ref[r] = kt
        ssq_ref[...] += kt * kt


def _filter_rows(L, p):
    hid = p['flt_w2'].shape[0]
    hy_w = p['hy_bias'].shape[1]
    cw = HY_ORDER * hy_w
    tr = 512
    N = 2 * L
    n1 = N // LANES
    t = jnp.linspace(0.0, 1.0, L, dtype=F32)[:, None]
    bands = (FILTER_EMB - 1) // 2
    f = jnp.linspace(1e-4, bands - 1, bands, dtype=F32)[None, :]
    w = (2.0 * math.pi / L) * jnp.arange(L, dtype=F32)[:, None]
    zf = jnp.concatenate([t, jnp.cos(f * w), -jnp.sin(f * w)], axis=-1)
    z = jnp.concatenate([zf, zf[L - 1:L], zf[:0:-1]], axis=0)
    tl = z[:, 0].reshape(n1, 1, LANES)
    z = jnp.pad(z, ((0, 0), (0, LANES - FILTER_EMB)))
    eye2 = jnp.eye(2, dtype=F32)
    w1 = jnp.concatenate([jnp.pad(p['flt_w1'], ((0, LANES - FILTER_EMB), (0, hid))),
                          jnp.pad(p['flt_w1'], ((0, LANES - FILTER_EMB), (hid, 0)))], axis=0)
    w2 = jnp.kron(eye2, p['flt_w2'])
    w3 = jnp.kron(eye2, p['flt_w3'])
    two = lambda v: jnp.tile(v, 2)[None]
    wo_t = p['flt_w_out'].reshape(hid, 2, cw).transpose(1, 2, 0)
    woa = jnp.pad(wo_t, ((0, 0), (0, 0), (0, hid)))
    wob = jnp.pad(wo_t, ((0, 0), (0, 0), (hid, 0)))
    max_decay = math.log(DECAY_TARGET) / FAST_DECAY_PCT
    min_decay = math.log(DECAY_TARGET) / SLOW_DECAY_PCT
    deltas = jnp.abs(jnp.linspace(min_decay, max_decay, hy_w, dtype=F32))
    dl = jnp.tile(deltas, HY_ORDER)[:, None]
    nblk = N // tr
    halfb = nblk // 2
    full = lambda i: (0, 0)
    h2 = 2 * hid
    return pl.pallas_call(
        functools.partial(_filter_kernel, L=L),
        out_shape=(jax.ShapeDtypeStruct((n1, cw, LANES), F32), jax.ShapeDtypeStruct((cw, LANES), F32)),
        grid=(nblk,),
        in_specs=[pl.BlockSpec((tr, LANES), lambda i: (i, 0)),
                  pl.BlockSpec((tr // LANES, 1, LANES), lambda i: (i, 0, 0)),
                  pl.BlockSpec((2 * LANES, h2), full), pl.BlockSpec((1, h2), full),
                  pl.BlockSpec((h2, h2), full), pl.BlockSpec((1, h2), full),
                  pl.BlockSpec((h2, h2), full), pl.BlockSpec((1, h2), full),
                  pl.BlockSpec((1, h2), full),
                  pl.BlockSpec((1, cw, h2), lambda i: (i // halfb, 0, 0)),
                  pl.BlockSpec((1, cw, h2), lambda i: (i // halfb, 0, 0)),
                  pl.BlockSpec((cw, 1), full)],
        out_specs=(pl.BlockSpec((tr // LANES, cw, LANES), lambda i: (i, 0, 0)), pl.BlockSpec((cw, LANES), full)),
        compiler_params=_params("arbitrary"),
        name="hyena_filter",
    )(z, tl, w1, two(p['flt_b1']), w2, two(p['flt_b2']), w3, two(p['flt_b3']), two(p['flt_freq']), woa, wob, dl)


def _fft_consts(L, planes):
    N = 2 * L
    n1 = N // FFT_N2
    n1h = n1 // 2
    k1 = np.arange(n1)[:, None]
    th = 2.0 * np.pi * k1 * np.arange(n1)[None, :] / n1
    c, s = np.cos(th), np.sin(th)
    ch, sh = c[:, :n1h], s[:, :n1h]
    if planes == 1:
        m1 = np.concatenate([ch, -sh], axis=0)
        m3 = np.concatenate([ch.T, -sh.T], axis=1) / N
    else:
        m1 = np.block([[ch, sh], [-sh, ch]])
        m3 = np.block([[ch.T, -sh.T], [sh.T, ch.T]]) / N
    m1_full = np.concatenate([c, -s], axis=0)
    ph = 2.0 * np.pi * k1 * np.arange(FFT_N2)[None, :] / N
    twr, twi = np.cos(ph), -np.sin(ph)
    a = 2.0 * np.pi * np.arange(FFT_N2)[:, None] * np.arange(FFT_N2)[None, :] / FFT_N2
    ca, sa = np.cos(a), np.sin(a)
    rt = np.block([[ca, -sa], [sa, ca]])
    rit = np.block([[ca, sa], [-sa, ca]])
    bf = lambda m: jnp.asarray(np.asarray(m, np.float32).astype(BF16))
    return dict(n1=n1, n1h=n1h, m1=bf(m1), m3=bf(m3), m1_full=bf(m1_full),
                twr=jnp.asarray(twr, F32), twi=jnp.asarray(twi, F32), rt=bf(rt), rit=bf(rit))


CONV_CH = 16


def _load_channels(ref, lead):
    return jnp.concatenate([ref[lead + (slice(None), c, slice(None))] for c in range(CONV_CH)], axis=1)


def _fwd_lane_dft(ar, ai, twr, twi, rt):
    pieces = []
    for c in range(CONV_CH):
        r = ar[:, c * LANES:(c + 1) * LANES]
        i = ai[:, c * LANES:(c + 1) * LANES]
        pieces.append(jnp.concatenate([r * twr - i * twi, r * twi + i * twr], axis=1))
    return _dot(jnp.concatenate(pieces, axis=0).astype(BF16), rt)


def _spectrum_kernel(k_ref, ssq_ref, m1_ref, twr_ref, twi_ref, rt_ref, o_ref):
    n1 = twr_ref.shape[0]
    a = _dot(m1_ref[...], _load_channels(k_ref, ()).astype(BF16))
    x = _fwd_lane_dft(a[:n1], a[n1:], twr_ref[...], twi_ref[...], rt_ref[...])
    scale = lax.rsqrt(jnp.sum(ssq_ref[...], axis=1, keepdims=True) + EPS)
    for c in range(CONV_CH):
        o_ref[0, :, c * 2 * LANES:(c + 1) * 2 * LANES] = x[c * n1:(c + 1) * n1] * scale[c:c + 1, :]


def _filter_spectrum(k3, ssq, fc, hy_w):
    n1 = fc['n1']
    ct = CONV_CH
    ntile = hy_w // ct
    full = lambda o, j: (0, 0)
    return pl.pallas_call(
        _spectrum_kernel,
        out_shape=jax.ShapeDtypeStruct((HY_ORDER, n1, hy_w * 2 * LANES), F32),
        grid=(HY_ORDER, ntile),
        in_specs=[pl.BlockSpec((n1, ct, LANES), lambda o, j: (0, o * ntile + j, 0)),
                  pl.BlockSpec((ct, LANES), lambda o, j: (o * ntile + j, 0)),
                  pl.BlockSpec((2 * n1, n1), full),
                  pl.BlockSpec((n1, LANES), full), pl.BlockSpec((n1, LANES), full),
                  pl.BlockSpec((2 * LANES, 2 * LANES), full)],
        out_specs=pl.BlockSpec((1, n1, ct * 2 * LANES), lambda o, j: (o, 0, j)),
        compiler_params=_params("arbitrary", "arbitrary"),
        name="filter_spectrum",
    )(k3, ssq, fc['m1_full'], fc['twr'], fc['twi'], fc['rt'])


def _longconv_kernel(z_ref, g_ref, kf_ref, b_ref, m1_ref, twr_ref, twi_ref, rt_ref, rit_ref, m3_ref, o_ref):
    ct = CONV_CH
    n1 = twr_ref.shape[0]
    planes, n1h = z_ref.shape[1], z_ref.shape[2]
    z = jnp.concatenate([_load_channels(z_ref, (0, p)) for p in range(planes)], axis=0)
    a = _dot(m1_ref[...], z.astype(BF16))
    twr = twr_ref[...]
    twi = twi_ref[...]
    x = _fwd_lane_dft(a[:n1], a[n1:], twr, twi, rt_ref[...])
    ys = []
    for c in range(ct):
        xc = x[c * n1:(c + 1) * n1]
        kc = kf_ref[0, :, c * 2 * LANES:(c + 1) * 2 * LANES]
        xr, xi = xc[:, :LANES], xc[:, LANES:]
        kr, ki = kc[:, :LANES], kc[:, LANES:]
        ys.append(jnp.concatenate([xr * kr - xi * ki, xr * ki + xi * kr], axis=1))
    bp = _dot(jnp.concatenate(ys, axis=0).astype(BF16), rit_ref[...])
    brs, bis = [], []
    for c in range(ct):
        bc = bp[c * n1:(c + 1) * n1]
        r, i = bc[:, :LANES], bc[:, LANES:]
        brs.append(r * twr + i * twi)
        bis.append(i * twr - r * twi)
    bst = jnp.concatenate([jnp.concatenate(brs, axis=1), jnp.concatenate(bis, axis=1)], axis=0)
    yt = _dot(m3_ref[...], bst.astype(BF16))
    for p in range(planes):
        rows = slice(p * n1h, (p + 1) * n1h)
        for c in range(ct):
            lanes = slice(c * LANES, (c + 1) * LANES)
            o_ref[0, p, :, c, :] = g_ref[0, p, :, c, :] * (yt[rows, lanes] + b_ref[0, 0:1, c:c + 1] * z[rows, lanes])


def _longconv(z5, g5, kf, order, bias_t, fc):
    npair, planes, n1h, ch, _ = z5.shape
    n1 = fc['n1']
    ct = CONV_CH
    ntile = ch // ct
    full = lambda p, j: (0, 0)
    m1r, m1c = 2 * n1, planes * n1h
    blk = pl.BlockSpec((1, planes, n1h, ct, LANES), lambda p, j: (p, 0, 0, j, 0))
    return pl.pallas_call(
        _longconv_kernel,
        out_shape=jax.ShapeDtypeStruct(z5.shape, F32),
        grid=(npair, ntile),
        in_specs=[blk, blk,
                  pl.BlockSpec((1, n1, ct * 2 * LANES), lambda p, j: (order, 0, j)),
                  pl.BlockSpec((1, 8, ct), lambda p, j: (order * ntile + j, 0, 0)),
                  pl.BlockSpec((m1r, m1c), full),
                  pl.BlockSpec((n1, LANES), full), pl.BlockSpec((n1, LANES), full),
                  pl.BlockSpec((2 * LANES, 2 * LANES), full), pl.BlockSpec((2 * LANES, 2 * LANES), full),
                  pl.BlockSpec((m1c, m1r), full)],
        out_specs=blk,
        compiler_params=_params("arbitrary", "arbitrary"),
        name="hyena_longconv",
    )(z5, g5, kf, bias_t, fc['m1'], fc['twr'], fc['twi'], fc['rt'], fc['rit'], fc['m3'])


def _attn_kernel(sink_ref, q_ref, kp_ref, km_ref, kn_ref, vp_ref, vm_ref, vn_ref, o_ref, *, L, tq):
    i = pl.program_id(1)
    t0 = i * tq
    kk = jnp.concatenate([kp_ref[...], km_ref[...], kn_ref[...]], axis=0)
    vv = jnp.concatenate([vp_ref[...], vm_ref[...], vn_ref[...]], axis=0)
    sub = WINDOW
    tk = 3 * WINDOW
    di = lax.broadcasted_iota(I32, (sub, tk), 0)
    dj = lax.broadcasted_iota(I32, (sub, tk), 1)
    band = jnp.abs(dj - WINDOW - di) <= WINDOW
    hid = lax.broadcasted_iota(I32, (N_HEADS, 1, 1), 0)
    sink = jnp.zeros((N_HEADS, 1, 1), F32)
    for h in range(N_HEADS):
        sink = jnp.where(hid == h, sink_ref[h], sink)
    for a in range(tq // sub):
        kpos = t0 + (a - 1) * sub + dj
        valid = band & (kpos >= 0) & (kpos < L)
        rows = slice(a * sub, (a + 1) * sub)
        qa = jnp.concatenate([q_ref[rows, h * LANES:(h + 1) * LANES] for h in range(N_HEADS)], axis=0)
        s = _dot_nt(qa, kk[a * sub:a * sub + tk]).reshape(N_HEADS, sub, tk)
        s = jnp.where(valid, s, -1e30)
        m = jnp.maximum(jnp.max(s, axis=-1, keepdims=True), sink)
        p = jnp.exp(s - m)
        denom = jnp.sum(p, axis=-1, keepdims=True) + jnp.exp(sink - m)
        o = _dot((p / denom).astype(BF16).reshape(N_HEADS * sub, tk), vv[a * sub:a * sub + tk]).astype(BF16)
        for h in range(N_HEADS):
            o_ref[rows, h * LANES:(h + 1) * LANES] = o[h * sub:(h + 1) * sub]


def _attention(q, k, v, sink, B, L):
    T = q.shape[0]
    tq = 512
    nq = L // tq
    r = tq // WINDOW
    nb = L // WINDOW
    kv_w = k.shape[1]
    kp = pl.BlockSpec((WINDOW, kv_w), lambda b, i: (b * nb + jnp.maximum(i * r - 1, 0), 0))
    km = pl.BlockSpec((tq, kv_w), lambda b, i: (b * nq + i, 0))
    kn = pl.BlockSpec((WINDOW, kv_w), lambda b, i: (b * nb + jnp.minimum((i + 1) * r, nb - 1), 0))
    qs = pl.BlockSpec((tq, N_HEADS * LANES), lambda b, i: (b * nq + i, 0))
    return pl.pallas_call(
        functools.partial(_attn_kernel, L=L, tq=tq),
        out_shape=jax.ShapeDtypeStruct((T, N_HEADS * LANES), BF16),
        grid=(B, nq),
        in_specs=[pl.BlockSpec(memory_space=pltpu.SMEM), qs, kp, km, kn, kp, km, kn],
        out_specs=qs,
        compiler_params=_params("arbitrary", "arbitrary"),
        name="band_attention",
    )(sink, q, k, k, k, v, v, v)


def _merge_kernel(x_ref, z_ref, a_ref, gs_ref, mod_ref, why_ref, wat_ref, wo_ref, g2_ref, rwh_ref, rwl_ref,
                  xo_ref, h2_ref, aff_ref):
    d = x_ref.shape[1]
    mod = mod_ref[0]
    z = jnp.concatenate([z_ref[r].T for r in range(z_ref.shape[0])], axis=0)
    y_hy = _dot(z.astype(BF16), why_ref[...])
    y_at = _dot(a_ref[...], wat_ref[...])
    gs = gs_ref[...]
    merged = gs[:, :d] * y_hy + gs[:, d:] * y_at
    x = x_ref[...] + mod[2:3] * _dot(merged.astype(BF16), wo_ref[...])
    xo_ref[...] = x
    ms = jnp.mean(x * x, axis=-1, keepdims=True)
    h = (x * lax.rsqrt(ms + EPS) * g2_ref[...]) * (1.0 + mod[4:5]) + mod[3:4]
    h2_ref[...] = h
    hh, hl = _split(h)
    rh, rl = rwh_ref[...], rwl_ref[...]
    logits = _dot_nt(rh, hh) + _dot_nt(rl, hh) + _dot_nt(rh, hl)
    m = jnp.max(logits, axis=0, keepdims=True)
    e = jnp.exp(logits - m)
    aff_ref[...] = e / jnp.sum(e, axis=0, keepdims=True)


def _merge(x, z, att, gs, mod_l, b0, L, w_hy, w_at_x, w_o, norm2_g, rw_hi, rw_lo):
    T, D = x.shape
    tm = 512
    nper = L // tm
    E = rw_hi.shape[0]
    full = lambda i: (0, 0)
    row = lambda w: pl.BlockSpec((tm, w), lambda i: (i, 0))
    return pl.pallas_call(
        _merge_kernel,
        out_shape=(jax.ShapeDtypeStruct((T, D), F32), jax.ShapeDtypeStruct((T, D), F32),
                   jax.ShapeDtypeStruct((E, T), F32)),
        grid=(T // tm,),
        in_specs=[row(D), pl.BlockSpec((tm // LANES,) + z.shape[1:], lambda i: (i, 0, 0)), row(att.shape[1]), row(gs.shape[1]),
                  pl.BlockSpec((1, 6, D), lambda i: (b0 + i // nper, 0, 0)),
                  pl.BlockSpec(w_hy.shape, full), pl.BlockSpec(w_at_x.shape, full), pl.BlockSpec(w_o.shape, full),
                  pl.BlockSpec((1, D), full), pl.BlockSpec((E, D), full), pl.BlockSpec((E, D), full)],
        out_specs=(row(D), row(D), pl.BlockSpec((E, tm), lambda i: (0, i))),
        compiler_params=_params("arbitrary"),
        name="merge_router",
    )(x, z, att, gs, mod_l, w_hy, w_at_x, w_o, norm2_g, rw_hi, rw_lo)


def _cumsum_rows(mb, u, ones, ls):
    rc = _dot(mb, u)
    totb = _dot(mb, ones)
    offs = _dot(ls, totb.astype(BF16))
    return rc, totb, offs


def _topk_kernel(aff_ref, u_ref, ut_ref, ones_ref, ls_ref, idx_ref, gate_ref, dst_ref, cum_ref, cnt_ref,
                 thr_ref, sel_ref, rank_ref, *, cap):
    E, R, _ = aff_ref.shape
    bits = pltpu.bitcast(aff_ref[...], I32)

    def search(i, cur):
        cand = cur | jnp.left_shift(jnp.int32(1), 30 - i)
        ge = jnp.where(bits >= cand, 1.0, 0.0)
        c = jnp.sum(jnp.sum(ge, axis=2, keepdims=True), axis=1, keepdims=True)
        return jnp.where(c >= cap, cand, cur)

    thr = lax.fori_loop(0, 31, search, jnp.zeros((E, 1, 1), I32))
    thr_ref[...] = jnp.broadcast_to(thr, thr_ref.shape)
    u = u_ref[...]
    ut = ut_ref[...]
    ones = ones_ref[...]
    ls = ls_ref[...]

    def select(e, carry):
        cnt_run, cum = carry
        b = pltpu.bitcast(aff_ref[e], I32)
        t = thr_ref[e][0:1, :]
        gt = b > t
        eq = b == t
        ngt = jnp.sum(jnp.sum(jnp.where(gt, 1.0, 0.0), axis=1, keepdims=True), axis=0, keepdims=True)
        need = cap - ngt
        erc, _, eoffs = _cumsum_rows(jnp.where(eq, 1.0, 0.0).astype(BF16), u, ones, ls)
        sel = jnp.where(gt | (eq & ((erc + eoffs) <= need)), 1.0, 0.0)
        rc, _, offs = _cumsum_rows(sel.astype(BF16), u, ones, ls)
        sel_ref[e] = sel
        rank_ref[e] = cnt_run
        return cnt_run + sel, cum + (rc + offs - sel)

    zero = jnp.zeros((R, LANES), F32)
    cnt, cum = lax.fori_loop(0, E, select, (zero, zero))
    cnt_ref[...] = cnt.astype(I32)
    cum_ref[...] = cum.astype(I32)
    s_row = lax.broadcasted_iota(I32, (1, cap), 1).astype(F32)
    r_col = lax.broadcasted_iota(I32, (R, 1), 0).astype(F32)
    c_col = lax.broadcasted_iota(I32, (LANES, 1), 0).astype(F32)

    def pick(vt_pieces, ohb, hit):
        acc = _dot(vt_pieces[0], ohb)
        for piece in vt_pieces[1:]:
            acc = acc + _dot(piece, ohb)
        return jnp.sum(jnp.where(hit, acc, 0.0), axis=0, keepdims=True)

    def compact(e, carry):
        sel = sel_ref[e]
        _, totb, offs = _cumsum_rows(sel.astype(BF16), u, ones, ls)
        offs_col = offs[:, 0:1]
        end_col = offs_col + totb[:, 0:1]
        oht = (offs_col <= s_row) & (s_row < end_col)
        row_s = jnp.sum(jnp.where(oht, r_col, 0.0), axis=0, keepdims=True)
        offs_s = jnp.sum(jnp.where(oht, offs_col, 0.0), axis=0, keepdims=True)
        ohb = jnp.where(oht, 1.0, 0.0).astype(BF16)
        rct = _dot(ut, sel.T.astype(BF16))
        rcs = _dot(rct.astype(BF16), ohb)
        col_s = jnp.sum(jnp.where(rcs <= (s_row - offs_s), 1.0, 0.0), axis=0, keepdims=True)
        hit = c_col == col_s
        idx_ref[pl.ds(e, 1), :] = (row_s * LANES + col_s).astype(I32)
        at = aff_ref[e].T
        p0 = at.astype(BF16)
        r1 = at - p0.astype(F32)
        p1 = r1.astype(BF16)
        p2 = (r1 - p1.astype(F32)).astype(BF16)
        gate_ref[pl.ds(e, 1), :] = pick((p0, p1, p2), ohb, hit)
        d = (cum + rank_ref[e]).T
        d_hi = (d * (1.0 / 256.0)).astype(I32).astype(F32)
        d_lo = d - 256.0 * d_hi
        dst = 256.0 * pick((d_hi.astype(BF16),), ohb, hit) + pick((d_lo.astype(BF16),), ohb, hit)
        dst_ref[pl.ds(e, 1), :] = dst.astype(I32)
        return carry

    lax.fori_loop(0, E, compact, 0)


def _topk(aff_t, cap):
    E, T = aff_t.shape
    R = T // LANES
    u = jnp.asarray(np.triu(np.ones((LANES, LANES))), BF16)
    ut = jnp.asarray(np.tril(np.ones((LANES, LANES))), BF16)
    ones = jnp.ones((LANES, LANES), BF16)
    ls = jnp.asarray(np.tril(np.ones((R, R)), -1), BF16)
    z2 = lambda: (0, 0)
    z3 = lambda: (0, 0, 0)
    slot = jax.ShapeDtypeStruct((E, cap), I32)
    tok = jax.ShapeDtypeStruct((R, LANES), I32)
    idx, gate, dst, cum, cnt = pl.pallas_call(
        functools.partial(_topk_kernel, cap=cap),
        out_shape=(slot, jax.ShapeDtypeStruct((E, cap), F32), slot, tok, tok),
        in_specs=[pl.BlockSpec((E, R, LANES), z3), pl.BlockSpec((LANES, LANES), z2), pl.BlockSpec((LANES, LANES), z2),
                  pl.BlockSpec((LANES, LANES), z2), pl.BlockSpec((R, R), z2)],
        out_specs=(pl.BlockSpec((E, cap), z2), pl.BlockSpec((E, cap), z2), pl.BlockSpec((E, cap), z2),
                   pl.BlockSpec((R, LANES), z2), pl.BlockSpec((R, LANES), z2)),
        scratch_shapes=[pltpu.VMEM((E, 8, LANES), I32), pltpu.VMEM((E, R, LANES), F32),
                        pltpu.VMEM((E, R, LANES), F32)],
        compiler_params=pltpu.CompilerParams(vmem_limit_bytes=VMEM_LIMIT),
        name="expert_topk",
    )(aff_t.reshape(E, R, LANES), u, ut, ones, ls)
    return idx, gate, dst, cum.reshape(T), cnt.reshape(T)


ROW_CHUNKS = 4
FF_TILE = 256


def _expert_kernel(src_ref, dst_ref, h_ref, wg_ref, wu_ref, wd_ref, gate_ref, out_ref,
                   xbuf, xb, acc, gsem, ssem, *, cap, per):
    e = pl.program_id(0)
    f = pl.program_id(1)
    ne = pl.num_programs(0)
    nf = pl.num_programs(1)
    slot = e % 2
    other = 1 - slot
    sched = nf * per

    def gather_row(entry, r):
        pltpu.make_async_copy(h_ref.at[pl.ds(src_ref[entry], 1), :], xbuf.at[pl.ds(r, 1), :], gsem).start()

    def scatter_row(src_slot, entry, r):
        pltpu.make_async_copy(acc.at[src_slot, pl.ds(r, 1), :], out_ref.at[pl.ds(dst_ref[entry], 1), :],
                              ssem.at[src_slot]).start()

    def schedule_done(sem):
        pltpu.make_async_copy(h_ref.at[pl.ds(0, sched), :], xbuf, sem).wait()

    @pl.when((e == 0) & (f == 0))
    def _():
        def body(s, c):
            gather_row(s, s)
            return c

        lax.fori_loop(0, sched, body, 0, unroll=8)
        acc[...] = jnp.zeros(acc.shape, F32)

    @pl.when((e > 0) & (f == 0))
    def _():
        schedule_done(ssem.at[slot])
        acc[slot, pl.ds(0, cap), :] = jnp.zeros((cap, acc.shape[2]), F32)

    @pl.when(f == 0)
    def _():
        schedule_done(gsem)
        xb[...] = xbuf[pl.ds(0, cap), :].astype(BF16)

    wg = wg_ref[0, 0].astype(BF16)
    wu = wu_ref[0, 0].astype(BF16)
    wd = wd_ref[0, 0].astype(BF16)
    gbase = (jnp.minimum(e + 1, ne - 1) * nf + f) * per
    sbase = (e * nf + f) * per
    row0 = f * per
    share = per // ROW_CHUNKS
    chunk = cap // ROW_CHUNKS
    for c in range(ROW_CHUNKS):
        for s in range(c * share, (c + 1) * share):
            gather_row(gbase + s, row0 + s)
            scatter_row(other, sbase + s, row0 + s)
        rows = pl.ds(c * chunk, chunk)
        x = xb[rows, :]
        g = _dot(x, wg)
        u = _dot(x, wu)
        hid = (g * jax.nn.sigmoid(g) * u).astype(BF16)
        acc[slot, rows, :] += _dot(hid, wd)

    @pl.when(f == nf - 1)
    def _():
        acc[slot, pl.ds(0, cap), :] = gate_ref[0] * acc[slot, pl.ds(0, cap), :]

    @pl.when((f == nf - 1) & (e == ne - 1))
    def _():
        def body(s, c):
            scatter_row(slot, ne * sched + s, s)
            return c

        lax.fori_loop(0, sched, body, 0, unroll=8)
        schedule_done(gsem)
        schedule_done(ssem.at[other])
        schedule_done(ssem.at[slot])


def _experts(h2, idx, dst, w_gate, w_up, w_down, layer, gate_col, cap):
    _, E, D, F = w_gate.shape
    ft = FF_TILE
    nf = F // ft
    per = -(-cap // (nf * 8 * ROW_CHUNKS)) * (8 * ROW_CHUNKS)
    sched = nf * per
    nr = E * cap
    assert E >= 2 and cap % ROW_CHUNKS == 0 and sched >= cap
    rows = jnp.arange(sched, dtype=I32)
    real = rows < cap
    rc = jnp.minimum(rows, cap - 1)
    src_tab = idx[:, rc]
    odd = (jnp.arange(E + 1, dtype=I32)[:, None] % 2) == 1
    spare = jnp.where(odd, nr + sched + (rows[None, :] - cap), nr + rows[None, :])
    dst_prev = jnp.concatenate([jnp.zeros((1, sched), I32), dst[:, rc]], axis=0)
    dst_tab = jnp.where(real[None, :] & (jnp.arange(E + 1)[:, None] > 0), dst_prev, spare)
    return pl.pallas_call(
        functools.partial(_expert_kernel, cap=cap, per=per),
        out_shape=jax.ShapeDtypeStruct((nr + 2 * sched - cap, D), F32),
        grid_spec=pltpu.PrefetchScalarGridSpec(
            num_scalar_prefetch=2,
            grid=(E, nf),
            in_specs=[pl.BlockSpec(memory_space=pl.ANY),
                      pl.BlockSpec((1, 1, D, ft), lambda e, f, i, d: (layer, e, 0, f)),
                      pl.BlockSpec((1, 1, D, ft), lambda e, f, i, d: (layer, e, 0, f)),
                      pl.BlockSpec((1, 1, ft, D), lambda e, f, i, d: (layer, e, f, 0)),
                      pl.BlockSpec((1, cap, 1), lambda e, f, i, d: (e, 0, 0))],
            out_specs=pl.BlockSpec(memory_space=pl.ANY),
            scratch_shapes=[pltpu.VMEM((sched, D), F32), pltpu.VMEM((cap, D), BF16), pltpu.VMEM((2, sched, D), F32),
                            pltpu.SemaphoreType.DMA, pltpu.SemaphoreType.DMA((2,))]),
        compiler_params=_params("arbitrary", "arbitrary"),
        name="expert_ffn",
    )(src_tab.reshape(-1), dst_tab.reshape(-1), h2, w_gate, w_up, w_down, gate_col)


COMBINE_ROWS = 512
COMBINE_TOKENS = 256


def _combine_kernel(tile_ref, blk_ref, flag_ref, c_ref, x_ref, mod_ref, tab_ref, eye_ref, o_ref):
    j = pl.program_id(0)
    flag = flag_ref[j]
    first = (flag & 1) != 0
    last = (flag & 2) != 0
    skip = (flag & 4) != 0
    rb = c_ref.shape[0]
    cols = _dot_nt(eye_ref[...], tab_ref[0].astype(BF16))
    start = cols[:, 0:1] * 256.0 + cols[:, 1:2]
    count = cols[:, 2:3]
    rabs = (blk_ref[j] * rb + lax.broadcasted_iota(I32, (1, rb), 1)).astype(F32)
    seg = jnp.where((start <= rabs) & (rabs < start + count), 1.0, 0.0).astype(BF16)
    hi, lo = _split(c_ref[...])
    part = _dot(seg, hi) + _dot(seg, lo)

    @pl.when(first)
    def _():
        o_ref[...] = part

    @pl.when(jnp.logical_not(first | skip))
    def _():
        o_ref[...] += part

    @pl.when(last & jnp.logical_not(skip))
    def _():
        o_ref[...] = x_ref[...] + mod_ref[0][5:6] * o_ref[...]


def _combine(x, mod_l, b0, L, contrib, cum, cnt):
    T, D = x.shape
    tt = COMBINE_TOKENS
    R = T // tt
    rb = COMBINE_ROWS
    nr = CAPACITY_FACTOR * T
    nb = nr // rb
    nper = L // tt
    nitem = R + nb
    lo = cum[::tt]
    hi = jnp.concatenate([lo[1:], jnp.full((1,), nr, I32)])
    b_lo = jnp.minimum(lo // rb, nb - 1)
    b_hi = jnp.minimum(jnp.maximum((hi - 1) // rb, b_lo), nb - 1)
    n = b_hi - b_lo + 1
    cn = jnp.cumsum(n)
    start = cn - n
    j = jnp.arange(nitem, dtype=I32)
    valid = j < cn[-1]
    ti = jnp.minimum(jnp.searchsorted(cn, j, side='right').astype(I32), R - 1)
    blk = jnp.where(valid, b_lo[ti] + (j - start[ti]), b_hi[R - 1])
    flag = jnp.where(valid, (j == start[ti]) * 1 + (j == cn[ti] - 1) * 2, 4).astype(I32)
    tab = jnp.stack([(cum >> 8).astype(F32), (cum & 255).astype(F32), cnt.astype(F32)], axis=0)
    tab = jnp.concatenate([tab, jnp.zeros((5, T), F32)], axis=0).reshape(8, R, tt).transpose(1, 0, 2)
    eye = jnp.eye(tt, dtype=BF16)
    return pl.pallas_call(
        _combine_kernel,
        out_shape=jax.ShapeDtypeStruct((T, D), F32),
        grid_spec=pltpu.PrefetchScalarGridSpec(
            num_scalar_prefetch=3,
            grid=(nitem,),
            in_specs=[pl.BlockSpec((rb, D), lambda j, t, b, f: (b[j], 0)),
                      pl.BlockSpec((tt, D), lambda j, t, b, f: (t[j], 0)),
                      pl.BlockSpec((1, 6, D), lambda j, t, b, f: (b0 + t[j] // nper, 0, 0)),
                      pl.BlockSpec((1, 8, tt), lambda j, t, b, f: (t[j], 0, 0)),
                      pl.BlockSpec((tt, tt), lambda j, t, b, f: (0, 0))],
            out_specs=pl.BlockSpec((tt, D), lambda j, t, b, f: (t[j], 0))),
        compiler_params=_params("arbitrary"),
        name="moe_combine",
    )(ti, blk, flag, contrib, x, mod_l, tab, eye)


def _layer(x, B, L, b0, mod_l, p, kf, fc, tabs):
    hy_w = p['hy_bias'].shape[1]
    att_w = N_HEADS * HEAD_DIM
    kv_w = N_KV_HEADS * HEAD_DIM
    planes = 2 if B % 2 == 0 else 1
    z, ga, gb, q, k, v, gs = _inproj(x, mod_l, b0, L, p['norm1_g'][None], p['w_in_bf'], p['hy_conv_w'], p['hy_conv_b'],
                                     tabs['cos'], tabs['sin'], tabs['qg'], tabs['kg'], tabs['blk'], hy_w, att_w, kv_w)
    shape5 = (B // planes, planes, L // LANES, hy_w, LANES)
    z = z.reshape(shape5)
    gates = (ga.reshape(shape5), gb.reshape(shape5))
    for o in range(HY_ORDER):
        z = _longconv(z, gates[o], kf, o, p['bias_t'], fc)
    z = z.reshape(B * L // LANES, hy_w, LANES)
    att = _attention(q, k, v, p['sink_logit'], B, L)
    x, h2, aff_t = _merge(x, z, att, gs, mod_l, b0, L, p['w_hy_bf'], p['w_at_x'], p['w_o_bf'], p['norm2_g'][None],
                          p['rw_hi'], p['rw_lo'])
    T = x.shape[0]
    cap = CAPACITY_FACTOR * T // N_EXPERTS
    idx, gate, dst, cum, cnt = _topk(aff_t, cap)
    contrib = _experts(h2, idx, dst, p['exp_w_gate'], p['exp_w_up'], p['exp_w_down'], p['layer'],
                       gate.reshape(N_EXPERTS, cap, 1), cap)
    return _combine(x, mod_l, b0, L, contrib, cum, cnt)


def _rope_tables(L):
    inv = 1.0 / (ROPE_THETA ** (jnp.arange(0, HEAD_DIM, 2, dtype=F32) / HEAD_DIM))
    ang = jnp.arange(L, dtype=F32)[:, None] * inv[None, :]
    cos, sin = jnp.cos(ang), jnp.sin(ang)
    cos_t = jnp.tile(cos, (1, LANES // (HEAD_DIM // 2)))
    sin_t = jnp.tile(jnp.concatenate([-sin, sin], axis=1), (1, LANES // HEAD_DIM))
    return cos_t, sin_t


def _filter_spectrum_for(L, p, fc):
    hy_w = p['hy_bias'].shape[1]
    k3, ssq = _filter_rows(L, p)
    return _filter_spectrum(k3, ssq, fc, hy_w)


def kernel(x_prompt, x_sample, c_prompt, c_sample, ada_w, ada_b, norm1_g, norm2_g, w_in, hy_conv_w, hy_conv_b,
           flt_w1, flt_b1, flt_w2, flt_b2, flt_w3, flt_b3, flt_freq, flt_w_out, hy_bias, q_norm_g, k_norm_g,
           sink_logit, w_hy_proj, w_at_proj, w_o, router_w, exp_w_gate, exp_w_up, exp_w_down):
    depth = ada_w.shape[0]
    Bp, Lp, D = x_prompt.shape
    Bs, Ls, _ = x_sample.shape
    nb = Bp + Bs
    c_all = jnp.concatenate([c_prompt, c_sample, jnp.zeros(((-nb) % 8, D), F32)], axis=0)
    mod = _adaln(c_all, ada_w, ada_b).reshape(depth, c_all.shape[0], 6, D)
    hy_w = hy_bias.shape[2]
    att_w = N_HEADS * HEAD_DIM
    groups = [dict(B=Bp, L=Lp, b0=0, x=x_prompt.reshape(Bp * Lp, D)),
              dict(B=Bs, L=Ls, b0=Bp, x=x_sample.reshape(Bs * Ls, D))]
    blk = jnp.asarray(np.kron(np.eye(LANES // HEAD_DIM), np.ones((HEAD_DIM, HEAD_DIM))), BF16)
    for g in groups:
        g['fc'] = _fft_consts(g['L'], 2 if g['B'] % 2 == 0 else 1)
        cos_t, sin_t = _rope_tables(g['L'])
        g['tabs'] = dict(cos=cos_t, sin=sin_t, blk=blk)
    heads_per_kv = N_HEADS // N_KV_HEADS
    for l in range(depth):
        w_at_x = jnp.zeros((N_HEADS, LANES, D), BF16)
        w_at_h = w_at_proj[l].astype(BF16).reshape(N_HEADS, HEAD_DIM, D)
        for h in range(N_HEADS):
            off = (h // heads_per_kv) * HEAD_DIM
            w_at_x = w_at_x.at[h, off:off + HEAD_DIM].set(w_at_h[h])
        rw_hi, rw_lo = _split(router_w[l].T)
        p = dict(norm1_g=norm1_g[l], norm2_g=norm2_g[l], w_in_bf=w_in[l].astype(BF16),
                 hy_conv_w=hy_conv_w[l], hy_conv_b=hy_conv_b[l],
                 flt_w1=flt_w1[l], flt_b1=flt_b1[l], flt_w2=flt_w2[l], flt_b2=flt_b2[l], flt_w3=flt_w3[l],
                 flt_b3=flt_b3[l], flt_freq=flt_freq[l], flt_w_out=flt_w_out[l], hy_bias=hy_bias[l],
                 bias_t=jnp.broadcast_to(hy_bias[l].reshape(-1, 1, CONV_CH), (HY_ORDER * hy_w // CONV_CH, 8, CONV_CH)),
                 sink_logit=sink_logit[l], w_hy_bf=w_hy_proj[l].astype(BF16),
                 w_at_x=w_at_x.reshape(N_HEADS * LANES, D), w_o_bf=w_o[l].astype(BF16),
                 rw_hi=rw_hi, rw_lo=rw_lo,
                 layer=l, exp_w_gate=exp_w_gate, exp_w_up=exp_w_up, exp_w_down=exp_w_down)
        qg = jnp.tile(q_norm_g[l], LANES // HEAD_DIM)[None]
        kg = jnp.tile(k_norm_g[l], LANES // HEAD_DIM)[None]
        for g in groups:
            kf = _filter_spectrum_for(g['L'], p, g['fc'])
            tabs = dict(g['tabs'], qg=qg, kg=kg)
            g['x'] = _layer(g['x'], g['B'], g['L'], g['b0'], mod[l], p, kf, g['fc'], tabs)
    return (groups[0]['x'].reshape(Bp, Lp, D), groups[1]['x'].reshape(Bs, Ls, D))
```

```python
import functools
import math

import numpy as np
import jax
import jax.numpy as jnp
from jax import lax
from jax.experimental import pallas as pl
from jax.experimental.pallas import tpu as pltpu

F32 = jnp.float32
BF16 = jnp.bfloat16
I32 = jnp.int32

EPS = 1e-6
HEAD_DIM = 64
N_HEADS = 8
N_KV_HEADS = 2
WINDOW = 128
ROPE_THETA = 10000.0
HY_ORDER = 2
FILTER_EMB = 33
DECAY_TARGET = 0.01
FAST_DECAY_PCT = 0.3
SLOW_DECAY_PCT = 1.5
N_EXPERTS = 16
CAPACITY_FACTOR = 2
LANES = 128
FFT_N2 = LANES
VMEM_LIMIT = 52 << 20


def _params(*sem):
    return pltpu.CompilerParams(dimension_semantics=sem, vmem_limit_bytes=VMEM_LIMIT)


def _dot(a, b):
    return jnp.dot(a, b, preferred_element_type=F32)


def _dot_nt(a, b):
    return lax.dot_general(a, b, (((1,), (1,)), ((), ())), preferred_element_type=F32)


def _split(x):
    hi = x.astype(BF16)
    lo = (x - hi.astype(F32)).astype(BF16)
    return hi, lo


def _dot3(a_hi, a_lo, b_hi, b_lo):
    return _dot(a_hi, b_hi) + _dot(a_lo, b_hi) + _dot(a_hi, b_lo)


def _ada_kernel(c_ref, w_ref, b_ref, o_ref):
    c = c_ref[...]
    s = c * jax.nn.sigmoid(c)
    o_ref[0] = _dot(s.astype(BF16), w_ref[0].astype(BF16)) + b_ref[0]


def _adaln(c_all, ada_w, ada_b):
    depth, d, n = ada_w.shape
    tn = n // 4
    return pl.pallas_call(
        _ada_kernel,
        out_shape=jax.ShapeDtypeStruct((depth, c_all.shape[0], n), F32),
        grid=(depth, n // tn),
        in_specs=[pl.BlockSpec(c_all.shape, lambda l, j: (0, 0)),
                  pl.BlockSpec((1, d, tn), lambda l, j: (l, 0, j)),
                  pl.BlockSpec((1, 1, tn), lambda l, j: (l, 0, j))],
        out_specs=pl.BlockSpec((1, c_all.shape[0], tn), lambda l, j: (l, 0, j)),
        compiler_params=_params("arbitrary", "arbitrary"),
        name="adaln",
    )(c_all, ada_w, ada_b.reshape(depth, 1, n))


def _head_norm_rope(x, gain, cos, sin, blk_hi):
    sq = x * x
    hi, lo = _split(sq)
    ssum = _dot(hi, blk_hi) + _dot(lo, blk_hi)
    xn = x * lax.rsqrt(ssum * (1.0 / HEAD_DIM) + EPS) * gain
    lane = lax.broadcasted_iota(I32, x.shape, 1)
    first_half = (lane % HEAD_DIM) < (HEAD_DIM // 2)
    rot = jnp.where(first_half, pltpu.roll(xn, LANES - HEAD_DIM // 2, 1), pltpu.roll(xn, HEAD_DIM // 2, 1))
    return xn * cos + rot * sin


def _inproj_kernel(x_ref, xp_ref, xn_ref, mod_ref, g_ref, w_ref, cw_ref, cb_ref, cos_ref, sin_ref, qg_ref, kg_ref, blk_ref,
                   z_ref, ga_ref, gb_ref, q_ref, k_ref, v_ref, gs_ref, *, hy_w, att_w, kv_w, nper):
    i = pl.program_id(0)
    tm = x_ref.shape[0]
    halo = xp_ref.shape[0]
    x = jnp.concatenate([xp_ref[...], x_ref[...], xn_ref[...]], axis=0)
    mod = mod_ref[0]
    ms = jnp.mean(x * x, axis=-1, keepdims=True)
    h = (x * lax.rsqrt(ms + EPS) * g_ref[...]) * (1.0 + mod[1:2]) + mod[0:1]
    hb_all = h.astype(BF16)
    hb = hb_all[halo:halo + tm]
    s0 = 3 * hy_w
    s1 = s0 + att_w
    s2 = s1 + kv_w
    s3 = s2 + kv_w
    hy = _dot(hb_all, w_ref[:, 0:s0])
    n_all = tm + 2 * halo
    row = lax.broadcasted_iota(I32, (tm, 1), 0)
    first = (i % nper) == 0
    last = (i % nper) == nper - 1
    up = jnp.where((row == 0) & first, 0.0, pltpu.roll(hy, 1, 0)[halo:halo + tm])
    dn = jnp.where((row == tm - 1) & last, 0.0, pltpu.roll(hy, n_all - 1, 0)[halo:halo + tm])
    y = up * cw_ref[0:1, :] + hy[halo:halo + tm] * cw_ref[1:2, :] + dn * cw_ref[2:3, :] + cb_ref[...]
    for r in range(tm // LANES):
        blk_y = y[r * LANES:(r + 1) * LANES, :]
        z_ref[r] = blk_y[:, 0:hy_w].T
        ga_ref[r] = blk_y[:, hy_w:2 * hy_w].T
        gb_ref[r] = blk_y[:, 2 * hy_w:3 * hy_w].T
    q = _dot(hb, w_ref[:, s0:s1])
    k = _dot(hb, w_ref[:, s1:s2])
    v_ref[...] = _dot(hb, w_ref[:, s2:s3]).astype(BF16)
    gs_ref[...] = jax.nn.sigmoid(_dot(hb, w_ref[:, s3:]))
    cos = cos_ref[...]
    sin = sin_ref[...]
    blk = blk_ref[...]
    k_ref[...] = _head_norm_rope(k, kg_ref[...], cos, sin, blk).astype(BF16)
    lane = lax.broadcasted_iota(I32, cos.shape, 1)
    low = lane < HEAD_DIM
    heads_per_kv = N_HEADS // N_KV_HEADS
    for ci in range(att_w // LANES):
        qr = _head_norm_rope(q[:, ci * LANES:(ci + 1) * LANES], qg_ref[...], cos, sin, blk) * (HEAD_DIM ** -0.5)
        qs = pltpu.roll(qr, HEAD_DIM, 1)
        kvh = (2 * ci) // heads_per_kv
        if kvh == 0:
            h0 = jnp.where(low, qr, 0.0)
            h1 = jnp.where(low, qs, 0.0)
        else:
            h0 = jnp.where(low, 0.0, qs)
            h1 = jnp.where(low, 0.0, qr)
        q_ref[:, (2 * ci) * LANES:(2 * ci + 1) * LANES] = h0.astype(BF16)
        q_ref[:, (2 * ci + 1) * LANES:(2 * ci + 2) * LANES] = h1.astype(BF16)


def _inproj(x, mod_l, b0, L, norm_g, w_in_bf, conv_w, conv_b, cos_t, sin_t, qg, kg, blk, hy_w, att_w, kv_w):
    T, D = x.shape
    tm = 512
    halo = 8
    nper = L // tm
    r8 = tm // halo
    nb8 = T // halo
    in_w = w_in_bf.shape[1]
    kern = functools.partial(_inproj_kernel, hy_w=hy_w, att_w=att_w, kv_w=kv_w, nper=nper)
    stream = jax.ShapeDtypeStruct((T // LANES, hy_w, LANES), F32)
    sspec = pl.BlockSpec((tm // LANES, hy_w, LANES), lambda i: (i, 0, 0))
    full = lambda i: (0, 0)
    return pl.pallas_call(
        kern,
        out_shape=(stream, stream, stream,
                   jax.ShapeDtypeStruct((T, N_HEADS * LANES), BF16),
                   jax.ShapeDtypeStruct((T, kv_w), BF16),
                   jax.ShapeDtypeStruct((T, kv_w), BF16),
                   jax.ShapeDtypeStruct((T, in_w - 3 * hy_w - att_w - 2 * kv_w), F32)),
        grid=(T // tm,),
        in_specs=[pl.BlockSpec((tm, D), lambda i: (i, 0)),
                  pl.BlockSpec((halo, D), lambda i: (jnp.maximum(i * r8 - 1, 0), 0)),
                  pl.BlockSpec((halo, D), lambda i: (jnp.minimum((i + 1) * r8, nb8 - 1), 0)),
                  pl.BlockSpec((1, 6, D), lambda i: (b0 + i // nper, 0, 0)),
                  pl.BlockSpec((1, D), full),
                  pl.BlockSpec((D, in_w), full),
                  pl.BlockSpec((3, 3 * hy_w), full),
                  pl.BlockSpec((1, 3 * hy_w), full),
                  pl.BlockSpec((tm, LANES), lambda i: (i % nper, 0)),
                  pl.BlockSpec((tm, LANES), lambda i: (i % nper, 0)),
                  pl.BlockSpec((1, LANES), full),
                  pl.BlockSpec((1, LANES), full),
                  pl.BlockSpec((LANES, LANES), full)],
        out_specs=(sspec, sspec, sspec,
                   pl.BlockSpec((tm, N_HEADS * LANES), lambda i: (i, 0)),
                   pl.BlockSpec((tm, kv_w), lambda i: (i, 0)),
                   pl.BlockSpec((tm, kv_w), lambda i: (i, 0)),
                   pl.BlockSpec((tm, in_w - 3 * hy_w - att_w - 2 * kv_w), lambda i: (i, 0))),
        compiler_params=_params("arbitrary"),
        name="inproj",
    )(x, x, x, mod_l, norm_g, w_in_bf, conv_w, conv_b.reshape(1, -1), cos_t, sin_t, qg, kg, blk)


def _filter_kernel(z_ref, tl_ref, w1_ref, b1_ref, w2_ref, b2_ref, w3_ref, b3_ref, fr_ref, woa_ref, wob_ref, dl_ref,
                   k_ref, ssq_ref, *, L):
    i = pl.program_id(0)
    z = z_ref[...]
    tr = z.shape[0]
    half = tr // 2
    fr = fr_ref[...]

    def lin(a, w_ref_, b_ref_):
        ah, al = _split(a)
        wh, wl = _split(w_ref_[...])
        return _dot3(ah, al, wh, wl) + b_ref_[...]

    zz = jnp.concatenate([z[:half], z[half:]], axis=1)
    h = jnp.sin(fr * lin(zz, w1_ref, b1_ref))
    h = jnp.sin(fr * lin(h, w2_ref, b2_ref))
    h = jnp.sin(fr * lin(h, w3_ref, b3_ref))
    hh, hl = _split(h)
    wts = (_split(woa_ref[0]), _split(wob_ref[0]))

    @pl.when(i == 0)
    def _():
        ssq_ref[...] = jnp.zeros_like(ssq_ref)

    lane = lax.broadcasted_iota(I32, (1, LANES), 1)
    per_half = half // LANES
    for r in range(tr // LANES):
        wh, wl = wts[r // per_half]
        rows = slice((r % per_half) * LANES, (r % per_half + 1) * LANES)
        kt = _dot_nt(wh, hh[rows]) + _dot_nt(wl, hh[rows]) + _dot_nt(wh, hl[rows])
        kt = kt * jnp.exp(-dl_ref[...] * tl_ref[r])
        kt = jnp.where((i * tr + r * LANES + lane) == L, 0.0, kt)
        k_ref[r] = kt
        ssq_ref[...] += kt * kt


def _filter_rows(L, p):
    hid = p['flt_w2'].shape[0]
    hy_w = p['hy_bias'].shape[1]
    cw = HY_ORDER * hy_w
    tr = 512
    N = 2 * L
    n1 = N // LANES
    t = jnp.linspace(0.0, 1.0, L, dtype=F32)[:, None]
    bands = (FILTER_EMB - 1) // 2
    f = jnp.linspace(1e-4, bands - 1, bands, dtype=F32)[None, :]
    w = (2.0 * math.pi / L) * jnp.arange(L, dtype=F32)[:, None]
    zf = jnp.concatenate([t, jnp.cos(f * w), -jnp.sin(f * w)], axis=-1)
    z = jnp.concatenate([zf, zf[L - 1:L], zf[:0:-1]], axis=0)
    tl = z[:, 0].reshape(n1, 1, LANES)
    z = jnp.pad(z, ((0, 0), (0, LANES - FILTER_EMB)))
    eye2 = jnp.eye(2, dtype=F32)
    w1 = jnp.concatenate([jnp.pad(p['flt_w1'], ((0, LANES - FILTER_EMB), (0, hid))),
                          jnp.pad(p['flt_w1'], ((0, LANES - FILTER_EMB), (hid, 0)))], axis=0)
    w2 = jnp.kron(eye2, p['flt_w2'])
    w3 = jnp.kron(eye2, p['flt_w3'])
    two = lambda v: jnp.tile(v, 2)[None]
    wo_t = p['flt_w_out'].reshape(hid, 2, cw).transpose(1, 2, 0)
    woa = jnp.pad(wo_t, ((0, 0), (0, 0), (0, hid)))
    wob = jnp.pad(wo_t, ((0, 0), (0, 0), (hid, 0)))
    max_decay = math.log(DECAY_TARGET) / FAST_DECAY_PCT
    min_decay = math.log(DECAY_TARGET) / SLOW_DECAY_PCT
    deltas = jnp.abs(jnp.linspace(min_decay, max_decay, hy_w, dtype=F32))
    dl = jnp.tile(deltas, HY_ORDER)[:, None]
    nblk = N // tr
    halfb = nblk // 2
    full = lambda i: (0, 0)
    h2 = 2 * hid
    return pl.pallas_call(
        functools.partial(_filter_kernel, L=L),
        out_shape=(jax.ShapeDtypeStruct((n1, cw, LANES), F32), jax.ShapeDtypeStruct((cw, LANES), F32)),
        grid=(nblk,),
        in_specs=[pl.BlockSpec((tr, LANES), lambda i: (i, 0)),
                  pl.BlockSpec((tr // LANES, 1, LANES), lambda i: (i, 0, 0)),
                  pl.BlockSpec((2 * LANES, h2), full), pl.BlockSpec((1, h2), full),
                  pl.BlockSpec((h2, h2), full), pl.BlockSpec((1, h2), full),
                  pl.BlockSpec((h2, h2), full), pl.BlockSpec((1, h2), full),
                  pl.BlockSpec((1, h2), full),
                  pl.BlockSpec((1, cw, h2), lambda i: (i // halfb, 0, 0)),
                  pl.BlockSpec((1, cw, h2), lambda i: (i // halfb, 0, 0)),
                  pl.BlockSpec((cw, 1), full)],
        out_specs=(pl.BlockSpec((tr // LANES, cw, LANES), lambda i: (i, 0, 0)), pl.BlockSpec((cw, LANES), full)),
        compiler_params=_params("arbitrary"),
        name="hyena_filter",
    )(z, tl, w1, two(p['flt_b1']), w2, two(p['flt_b2']), w3, two(p['flt_b3']), two(p['flt_freq']), woa, wob, dl)


def _fft_consts(L, planes):
    N = 2 * L
    n1 = N // FFT_N2
    n1h = n1 // 2
    k1 = np.arange(n1)[:, None]
    th = 2.0 * np.pi * k1 * np.arange(n1)[None, :] / n1
    c, s = np.cos(th), np.sin(th)
    ch, sh = c[:, :n1h], s[:, :n1h]
    if planes == 1:
        m1 = np.concatenate([ch, -sh], axis=0)
        m3 = np.concatenate([ch.T, -sh.T], axis=1) / N
    else:
        m1 = np.block([[ch, sh], [-sh, ch]])
        m3 = np.block([[ch.T, -sh.T], [sh.T, ch.T]]) / N
    m1_full = np.concatenate([c, -s], axis=0)
    ph = 2.0 * np.pi * k1 * np.arange(FFT_N2)[None, :] / N
    twr, twi = np.cos(ph), -np.sin(ph)
    a = 2.0 * np.pi * np.arange(FFT_N2)[:, None] * np.arange(FFT_N2)[None, :] / FFT_N2
    ca, sa = np.cos(a), np.sin(a)
    rt = np.block([[ca, -sa], [sa, ca]])
    rit = np.block([[ca, sa], [-sa, ca]])
    bf = lambda m: jnp.asarray(np.asarray(m, np.float32).astype(BF16))
    return dict(n1=n1, n1h=n1h, m1=bf(m1), m3=bf(m3), m1_full=bf(m1_full),
                twr=jnp.asarray(twr, F32), twi=jnp.asarray(twi, F32), rt=bf(rt), rit=bf(rit))


CONV_CH = 16


def _load_channels(ref, lead):
    return jnp.concatenate([ref[lead + (slice(None), c, slice(None))] for c in range(CONV_CH)], axis=1)


def _fwd_lane_dft(ar, ai, twr, twi, rt):
    pieces = []
    for c in range(CONV_CH):
        r = ar[:, c * LANES:(c + 1) * LANES]
        i = ai[:, c * LANES:(c + 1) * LANES]
        pieces.append(jnp.concatenate([r * twr - i * twi, r * twi + i * twr], axis=1))
    return _dot(jnp.concatenate(pieces, axis=0).astype(BF16), rt)


def _spectrum_kernel(k_ref, ssq_ref, m1_ref, twr_ref, twi_ref, rt_ref, o_ref):
    n1 = twr_ref.shape[0]
    a = _dot(m1_ref[...], _load_channels(k_ref, ()).astype(BF16))
    x = _fwd_lane_dft(a[:n1], a[n1:], twr_ref[...], twi_ref[...], rt_ref[...])
    scale = lax.rsqrt(jnp.sum(ssq_ref[...], axis=1, keepdims=True) + EPS)
    for c in range(CONV_CH):
        o_ref[0, :, c * 2 * LANES:(c + 1) * 2 * LANES] = x[c * n1:(c + 1) * n1] * scale[c:c + 1, :]


def _filter_spectrum(k3, ssq, fc, hy_w):
    n1 = fc['n1']
    ct = CONV_CH
    ntile = hy_w // ct
    full = lambda o, j: (0, 0)
    return pl.pallas_call(
        _spectrum_kernel,
        out_shape=jax.ShapeDtypeStruct((HY_ORDER, n1, hy_w * 2 * LANES), F32),
        grid=(HY_ORDER, ntile),
        in_specs=[pl.BlockSpec((n1, ct, LANES), lambda o, j: (0, o * ntile + j, 0)),
                  pl.BlockSpec((ct, LANES), lambda o, j: (o * ntile + j, 0)),
                  pl.BlockSpec((2 * n1, n1), full),
                  pl.BlockSpec((n1, LANES), full), pl.BlockSpec((n1, LANES), full),
                  pl.BlockSpec((2 * LANES, 2 * LANES), full)],
        out_specs=pl.BlockSpec((1, n1, ct * 2 * LANES), lambda o, j: (o, 0, j)),
        compiler_params=_params("arbitrary", "arbitrary"),
        name="filter_spectrum",
    )(k3, ssq, fc['m1_full'], fc['twr'], fc['twi'], fc['rt'])


def _longconv_kernel(z_ref, g_ref, kf_ref, b_ref, m1_ref, twr_ref, twi_ref, rt_ref, rit_ref, m3_ref, o_ref):
    ct = CONV_CH
    n1 = twr_ref.shape[0]
    planes, n1h = z_ref.shape[1], z_ref.shape[2]
    z = jnp.concatenate([_load_channels(z_ref, (0, p)) for p in range(planes)], axis=0)
    a = _dot(m1_ref[...], z.astype(BF16))
    twr = twr_ref[...]
    twi = twi_ref[...]
    x = _fwd_lane_dft(a[:n1], a[n1:], twr, twi, rt_ref[...])
    ys = []
    for c in range(ct):
        xc = x[c * n1:(c + 1) * n1]
        kc = kf_ref[0, :, c * 2 * LANES:(c + 1) * 2 * LANES]
        xr, xi = xc[:, :LANES], xc[:, LANES:]
        kr, ki = kc[:, :LANES], kc[:, LANES:]
        ys.append(jnp.concatenate([xr * kr - xi * ki, xr * ki + xi * kr], axis=1))
    bp = _dot(jnp.concatenate(ys, axis=0).astype(BF16), rit_ref[...])
    brs, bis = [], []
    for c in range(ct):
        bc = bp[c * n1:(c + 1) * n1]
        r, i = bc[:, :LANES], bc[:, LANES:]
        brs.append(r * twr + i * twi)
        bis.append(i * twr - r * twi)
    bst = jnp.concatenate([jnp.concatenate(brs, axis=1), jnp.concatenate(bis, axis=1)], axis=0)
    yt = _dot(m3_ref[...], bst.astype(BF16))
    for p in range(planes):
        rows = slice(p * n1h, (p + 1) * n1h)
        for c in range(ct):
            lanes = slice(c * LANES, (c + 1) * LANES)
            o_ref[0, p, :, c, :] = g_ref[0, p, :, c, :] * (yt[rows, lanes] + b_ref[0, 0:1, c:c + 1] * z[rows, lanes])


def _longconv(z5, g5, kf, order, bias_t, fc):
    npair, planes, n1h, ch, _ = z5.shape
    n1 = fc['n1']
    ct = CONV_CH
    ntile = ch // ct
    full = lambda p, j: (0, 0)
    m1r, m1c = 2 * n1, planes * n1h
    blk = pl.BlockSpec((1, planes, n1h, ct, LANES), lambda p, j: (p, 0, 0, j, 0))
    return pl.pallas_call(
        _longconv_kernel,
        out_shape=jax.ShapeDtypeStruct(z5.shape, F32),
        grid=(npair, ntile),
        in_specs=[blk, blk,
                  pl.BlockSpec((1, n1, ct * 2 * LANES), lambda p, j: (order, 0, j)),
                  pl.BlockSpec((1, 8, ct), lambda p, j: (order * ntile + j, 0, 0)),
                  pl.BlockSpec((m1r, m1c), full),
                  pl.BlockSpec((n1, LANES), full), pl.BlockSpec((n1, LANES), full),
                  pl.BlockSpec((2 * LANES, 2 * LANES), full), pl.BlockSpec((2 * LANES, 2 * LANES), full),
                  pl.BlockSpec((m1c, m1r), full)],
        out_specs=blk,
        compiler_params=_params("arbitrary", "arbitrary"),
        name="hyena_longconv",
    )(z5, g5, kf, bias_t, fc['m1'], fc['twr'], fc['twi'], fc['rt'], fc['rit'], fc['m3'])


def _attn_kernel(sink_ref, q_ref, kp_ref, km_ref, kn_ref, vp_ref, vm_ref, vn_ref, o_ref, *, L, tq):
    i = pl.program_id(1)
    t0 = i * tq
    kk = jnp.concatenate([kp_ref[...], km_ref[...], kn_ref[...]], axis=0)
    vv = jnp.concatenate([vp_ref[...], vm_ref[...], vn_ref[...]], axis=0)
    sub = WINDOW
    tk = 3 * WINDOW
    di = lax.broadcasted_iota(I32, (sub, tk), 0)
    dj = lax.broadcasted_iota(I32, (sub, tk), 1)
    band = jnp.abs(dj - WINDOW - di) <= WINDOW
    hid = lax.broadcasted_iota(I32, (N_HEADS, 1, 1), 0)
    sink = jnp.zeros((N_HEADS, 1, 1), F32)
    for h in range(N_HEADS):
        sink = jnp.where(hid == h, sink_ref[h], sink)
    for a in range(tq // sub):
        kpos = t0 + (a - 1) * sub + dj
        valid = band & (kpos >= 0) & (kpos < L)
        rows = slice(a * sub, (a + 1) * sub)
        qa = jnp.concatenate([q_ref[rows, h * LANES:(h + 1) * LANES] for h in range(N_HEADS)], axis=0)
        s = _dot_nt(qa, kk[a * sub:a * sub + tk]).reshape(N_HEADS, sub, tk)
        s = jnp.where(valid, s, -1e30)
        m = jnp.maximum(jnp.max(s, axis=-1, keepdims=True), sink)
        p = jnp.exp(s - m)
        denom = jnp.sum(p, axis=-1, keepdims=True) + jnp.exp(sink - m)
        o = _dot((p / denom).astype(BF16).reshape(N_HEADS * sub, tk), vv[a * sub:a * sub + tk]).astype(BF16)
        for h in range(N_HEADS):
            o_ref[rows, h * LANES:(h + 1) * LANES] = o[h * sub:(h + 1) * sub]


def _attention(q, k, v, sink, B, L):
    T = q.shape[0]
    tq = 512
    nq = L // tq
    r = tq // WINDOW
    nb = L // WINDOW
    kv_w = k.shape[1]
    kp = pl.BlockSpec((WINDOW, kv_w), lambda b, i: (b * nb + jnp.maximum(i * r - 1, 0), 0))
    km = pl.BlockSpec((tq, kv_w), lambda b, i: (b * nq + i, 0))
    kn = pl.BlockSpec((WINDOW, kv_w), lambda b, i: (b * nb + jnp.minimum((i + 1) * r, nb - 1), 0))
    qs = pl.BlockSpec((tq, N_HEADS * LANES), lambda b, i: (b * nq + i, 0))
    return pl.pallas_call(
        functools.partial(_attn_kernel, L=L, tq=tq),
        out_shape=jax.ShapeDtypeStruct((T, N_HEADS * LANES), BF16),
        grid=(B, nq),
        in_specs=[pl.BlockSpec(memory_space=pltpu.SMEM), qs, kp, km, kn, kp, km, kn],
        out_specs=qs,
        compiler_params=_params("arbitrary", "arbitrary"),
        name="band_attention",
    )(sink, q, k, k, k, v, v, v)


def _merge_kernel(x_ref, z_ref, a_ref, gs_ref, mod_ref, why_ref, wat_ref, wo_ref, g2_ref, rwh_ref, rwl_ref,
                  xo_ref, h2_ref, aff_ref):
    d = x_ref.shape[1]
    mod = mod_ref[0]
    z = jnp.concatenate([z_ref[r].T for r in range(z_ref.shape[0])], axis=0)
    y_hy = _dot(z.astype(BF16), why_ref[...])
    y_at = _dot(a_ref[...], wat_ref[...])
    gs = gs_ref[...]
    merged = gs[:, :d] * y_hy + gs[:, d:] * y_at
    x = x_ref[...] + mod[2:3] * _dot(merged.astype(BF16), wo_ref[...])
    xo_ref[...] = x
    ms = jnp.mean(x * x, axis=-1, keepdims=True)
    h = (x * lax.rsqrt(ms + EPS) * g2_ref[...]) * (1.0 + mod[4:5]) + mod[3:4]
    half = d // 2
    lo = pltpu.bitcast(h[:, :half].astype(BF16).astype(F32), jnp.uint32)
    hi = pltpu.bitcast(h[:, half:].astype(BF16).astype(F32), jnp.uint32)
    h2_ref[...] = hi | (lo >> 16)
    hh, hl = _split(h)
    rh, rl = rwh_ref[...], rwl_ref[...]
    logits = _dot_nt(rh, hh) + _dot_nt(rl, hh) + _dot_nt(rh, hl)
    m = jnp.max(logits, axis=0, keepdims=True)
    e = jnp.exp(logits - m)
    aff_ref[...] = e / jnp.sum(e, axis=0, keepdims=True)


def _merge(x, z, att, gs, mod_l, b0, L, w_hy, w_at_x, w_o, norm2_g, rw_hi, rw_lo):
    T, D = x.shape
    tm = 512
    nper = L // tm
    E = rw_hi.shape[0]
    full = lambda i: (0, 0)
    row = lambda w: pl.BlockSpec((tm, w), lambda i: (i, 0))
    return pl.pallas_call(
        _merge_kernel,
        out_shape=(jax.ShapeDtypeStruct((T, D), F32), jax.ShapeDtypeStruct((T, D // 2), jnp.uint32),
                   jax.ShapeDtypeStruct((E, T), F32)),
        grid=(T // tm,),
        in_specs=[row(D), pl.BlockSpec((tm // LANES,) + z.shape[1:], lambda i: (i, 0, 0)), row(att.shape[1]), row(gs.shape[1]),
                  pl.BlockSpec((1, 6, D), lambda i: (b0 + i // nper, 0, 0)),
                  pl.BlockSpec(w_hy.shape, full), pl.BlockSpec(w_at_x.shape, full), pl.BlockSpec(w_o.shape, full),
                  pl.BlockSpec((1, D), full), pl.BlockSpec((E, D), full), pl.BlockSpec((E, D), full)],
        out_specs=(row(D), row(D // 2), pl.BlockSpec((E, tm), lambda i: (0, i))),
        compiler_params=_params("arbitrary"),
        name="merge_router",
    )(x, z, att, gs, mod_l, w_hy, w_at_x, w_o, norm2_g, rw_hi, rw_lo)


def _cumsum_rows(mb, u, ones, ls):
    rc = _dot(mb, u)
    totb = _dot(mb, ones)
    offs = _dot(ls, totb.astype(BF16))
    return rc, totb, offs


def _topk_kernel(aff_ref, u_ref, ut_ref, ones_ref, ls_ref, idx_ref, gate_ref, dst_ref, cum_ref, cnt_ref,
                 thr_ref, sel_ref, rank_ref, *, cap):
    E, R, _ = aff_ref.shape
    bits = pltpu.bitcast(aff_ref[...], I32)

    def search(i, cur):
        cand = cur | jnp.left_shift(jnp.int32(1), 30 - i)
        ge = jnp.where(bits >= cand, 1.0, 0.0)
        c = jnp.sum(jnp.sum(ge, axis=2, keepdims=True), axis=1, keepdims=True)
        return jnp.where(c >= cap, cand, cur)

    thr = lax.fori_loop(0, 31, search, jnp.zeros((E, 1, 1), I32))
    thr_ref[...] = jnp.broadcast_to(thr, thr_ref.shape)
    u = u_ref[...]
    ut = ut_ref[...]
    ones = ones_ref[...]
    ls = ls_ref[...]

    def select(e, carry):
        cnt_run, cum = carry
        b = pltpu.bitcast(aff_ref[e], I32)
        t = thr_ref[e][0:1, :]
        gt = b > t
        eq = b == t
        ngt = jnp.sum(jnp.sum(jnp.where(gt, 1.0, 0.0), axis=1, keepdims=True), axis=0, keepdims=True)
        need = cap - ngt
        erc, _, eoffs = _cumsum_rows(jnp.where(eq, 1.0, 0.0).astype(BF16), u, ones, ls)
        sel = jnp.where(gt | (eq & ((erc + eoffs) <= need)), 1.0, 0.0)
        rc, _, offs = _cumsum_rows(sel.astype(BF16), u, ones, ls)
        sel_ref[e] = sel
        rank_ref[e] = cnt_run
        return cnt_run + sel, cum + (rc + offs - sel)

    zero = jnp.zeros((R, LANES), F32)
    cnt, cum = lax.fori_loop(0, E, select, (zero, zero))
    cnt_ref[...] = cnt.astype(I32)
    cum_ref[...] = cum.astype(I32)
    s_row = lax.broadcasted_iota(I32, (1, cap), 1).astype(F32)
    r_col = lax.broadcasted_iota(I32, (R, 1), 0).astype(F32)
    c_col = lax.broadcasted_iota(I32, (LANES, 1), 0).astype(F32)

    def pick(vt_pieces, ohb, hit):
        acc = _dot(vt_pieces[0], ohb)
        for piece in vt_pieces[1:]:
            acc = acc + _dot(piece, ohb)
        return jnp.sum(jnp.where(hit, acc, 0.0), axis=0, keepdims=True)

    def compact(e, carry):
        sel = sel_ref[e]
        _, totb, offs = _cumsum_rows(sel.astype(BF16), u, ones, ls)
        offs_col = offs[:, 0:1]
        end_col = offs_col + totb[:, 0:1]
        oht = (offs_col <= s_row) & (s_row < end_col)
        row_s = jnp.sum(jnp.where(oht, r_col, 0.0), axis=0, keepdims=True)
        offs_s = jnp.sum(jnp.where(oht, offs_col, 0.0), axis=0, keepdims=True)
        ohb = jnp.where(oht, 1.0, 0.0).astype(BF16)
        rct = _dot(ut, sel.T.astype(BF16))
        rcs = _dot(rct.astype(BF16), ohb)
        col_s = jnp.sum(jnp.where(rcs <= (s_row - offs_s), 1.0, 0.0), axis=0, keepdims=True)
        hit = c_col == col_s
        idx_ref[pl.ds(e, 1), :] = (row_s * LANES + col_s).astype(I32)
        at = aff_ref[e].T
        p0 = at.astype(BF16)
        r1 = at - p0.astype(F32)
        p1 = r1.astype(BF16)
        p2 = (r1 - p1.astype(F32)).astype(BF16)
        gate_ref[pl.ds(e, 1), :] = pick((p0, p1, p2), ohb, hit)
        d = (cum + rank_ref[e]).T
        d_hi = (d * (1.0 / 256.0)).astype(I32).astype(F32)
        d_lo = d - 256.0 * d_hi
        dst = 256.0 * pick((d_hi.astype(BF16),), ohb, hit) + pick((d_lo.astype(BF16),), ohb, hit)
        dst_ref[pl.ds(e, 1), :] = dst.astype(I32)
        return carry

    lax.fori_loop(0, E, compact, 0)


def _topk(aff_t, cap):
    E, T = aff_t.shape
    R = T // LANES
    u = jnp.asarray(np.triu(np.ones((LANES, LANES))), BF16)
    ut = jnp.asarray(np.tril(np.ones((LANES, LANES))), BF16)
    ones = jnp.ones((LANES, LANES), BF16)
    ls = jnp.asarray(np.tril(np.ones((R, R)), -1), BF16)
    z2 = lambda: (0, 0)
    z3 = lambda: (0, 0, 0)
    slot = jax.ShapeDtypeStruct((E, cap), I32)
    tok = jax.ShapeDtypeStruct((R, LANES), I32)
    idx, gate, dst, cum, cnt = pl.pallas_call(
        functools.partial(_topk_kernel, cap=cap),
        out_shape=(slot, jax.ShapeDtypeStruct((E, cap), F32), slot, tok, tok),
        in_specs=[pl.BlockSpec((E, R, LANES), z3), pl.BlockSpec((LANES, LANES), z2), pl.BlockSpec((LANES, LANES), z2),
                  pl.BlockSpec((LANES, LANES), z2), pl.BlockSpec((R, R), z2)],
        out_specs=(pl.BlockSpec((E, cap), z2), pl.BlockSpec((E, cap), z2), pl.BlockSpec((E, cap), z2),
                   pl.BlockSpec((R, LANES), z2), pl.BlockSpec((R, LANES), z2)),
        scratch_shapes=[pltpu.VMEM((E, 8, LANES), I32), pltpu.VMEM((E, R, LANES), F32),
                        pltpu.VMEM((E, R, LANES), F32)],
        compiler_params=pltpu.CompilerParams(vmem_limit_bytes=VMEM_LIMIT),
        name="expert_topk",
    )(aff_t.reshape(E, R, LANES), u, ut, ones, ls)
    return idx, gate, dst, cum.reshape(T), cnt.reshape(T)


ROW_CHUNKS = 4
FF_TILE = 256


def _expert_kernel(src_ref, dst_ref, h_ref, wg_ref, wu_ref, wd_ref, gate_ref, out_ref,
                   xbuf, xb, acc, gsem, ssem, *, cap, per):
    e = pl.program_id(0)
    f = pl.program_id(1)
    ne = pl.num_programs(0)
    nf = pl.num_programs(1)
    slot = e % 2
    other = 1 - slot
    sched = nf * per

    def gather_row(entry, r):
        pltpu.make_async_copy(h_ref.at[pl.ds(src_ref[entry], 1), :], xbuf.at[pl.ds(r, 1), :], gsem).start()

    def scatter_row(src_slot, entry, r):
        pltpu.make_async_copy(acc.at[src_slot, pl.ds(r, 1), :], out_ref.at[pl.ds(dst_ref[entry], 1), :],
                              ssem.at[src_slot]).start()

    def gather_done():
        pltpu.make_async_copy(h_ref.at[pl.ds(0, sched), :], xbuf, gsem).wait()

    def scatter_done(s):
        pltpu.make_async_copy(acc.at[s], out_ref.at[pl.ds(0, sched), :], ssem.at[s]).wait()

    @pl.when((e == 0) & (f == 0))
    def _():
        def body(s, c):
            gather_row(s, s)
            return c

        lax.fori_loop(0, sched, body, 0, unroll=8)
        acc[...] = jnp.zeros(acc.shape, F32)

    @pl.when((e > 0) & (f == 0))
    def _():
        scatter_done(slot)
        acc[slot, pl.ds(0, cap), :] = jnp.zeros((cap, acc.shape[2]), F32)

    @pl.when(f == 0)
    def _():
        gather_done()
        w = xbuf[pl.ds(0, cap), :]
        half = w.shape[1]
        xb[:, 0:half] = pltpu.bitcast(w << 16, F32).astype(BF16)
        xb[:, half:2 * half] = pltpu.bitcast(w & jnp.uint32(0xFFFF0000), F32).astype(BF16)

    wg = wg_ref[0, 0].astype(BF16)
    wu = wu_ref[0, 0].astype(BF16)
    wd = wd_ref[0, 0].astype(BF16)
    gbase = (jnp.minimum(e + 1, ne - 1) * nf + f) * per
    sbase = (e * nf + f) * per
    row0 = f * per
    share = per // ROW_CHUNKS
    chunk = cap // ROW_CHUNKS
    for c in range(ROW_CHUNKS):
        for s in range(c * share, (c + 1) * share):
            gather_row(gbase + s, row0 + s)
            scatter_row(other, sbase + s, row0 + s)
        rows = pl.ds(c * chunk, chunk)
        x = xb[rows, :]
        g = _dot(x, wg)
        u = _dot(x, wu)
        hid = (g * jax.nn.sigmoid(g) * u).astype(BF16)
        acc[slot, rows, :] += _dot(hid, wd)

    @pl.when(f == nf - 1)
    def _():
        acc[slot, pl.ds(0, cap), :] = gate_ref[0] * acc[slot, pl.ds(0, cap), :]

    @pl.when((f == nf - 1) & (e == ne - 1))
    def _():
        def body(s, c):
            scatter_row(slot, ne * sched + s, s)
            return c

        lax.fori_loop(0, sched, body, 0, unroll=8)
        gather_done()
        scatter_done(other)
        scatter_done(slot)


def _experts(h2, idx, dst, w_gate, w_up, w_down, layer, gate_col, cap):
    _, E, D, F = w_gate.shape
    ft = FF_TILE
    nf = F // ft
    per = -(-cap // (nf * 8 * ROW_CHUNKS)) * (8 * ROW_CHUNKS)
    sched = nf * per
    nr = E * cap
    assert E >= 2 and cap % ROW_CHUNKS == 0 and sched >= cap
    rows = jnp.arange(sched, dtype=I32)
    real = rows < cap
    rc = jnp.minimum(rows, cap - 1)
    src_tab = idx[:, rc]
    odd = (jnp.arange(E + 1, dtype=I32)[:, None] % 2) == 1
    spare = jnp.where(odd, nr + sched + (rows[None, :] - cap), nr + rows[None, :])
    dst_prev = jnp.concatenate([jnp.zeros((1, sched), I32), dst[:, rc]], axis=0)
    dst_tab = jnp.where(real[None, :] & (jnp.arange(E + 1)[:, None] > 0), dst_prev, spare)
    return pl.pallas_call(
        functools.partial(_expert_kernel, cap=cap, per=per),
        out_shape=jax.ShapeDtypeStruct((nr + 2 * sched - cap, D), F32),
        grid_spec=pltpu.PrefetchScalarGridSpec(
            num_scalar_prefetch=2,
            grid=(E, nf),
            in_specs=[pl.BlockSpec(memory_space=pl.ANY),
                      pl.BlockSpec((1, 1, D, ft), lambda e, f, i, d: (layer, e, 0, f)),
                      pl.BlockSpec((1, 1, D, ft), lambda e, f, i, d: (layer, e, 0, f)),
                      pl.BlockSpec((1, 1, ft, D), lambda e, f, i, d: (layer, e, f, 0)),
                      pl.BlockSpec((1, cap, 1), lambda e, f, i, d: (e, 0, 0))],
            out_specs=pl.BlockSpec(memory_space=pl.ANY),
            scratch_shapes=[pltpu.VMEM((sched, D // 2), jnp.uint32), pltpu.VMEM((cap, D), BF16),
                            pltpu.VMEM((2, sched, D), F32),
                            pltpu.SemaphoreType.DMA, pltpu.SemaphoreType.DMA((2,))]),
        compiler_params=_params("arbitrary", "arbitrary"),
        name="expert_ffn",
    )(src_tab.reshape(-1), dst_tab.reshape(-1), h2, w_gate, w_up, w_down, gate_col)


COMBINE_ROWS = 512
COMBINE_TOKENS = 256


def _combine_kernel(tile_ref, blk_ref, flag_ref, c_ref, x_ref, mod_ref, tab_ref, eye_ref, o_ref):
    j = pl.program_id(0)
    flag = flag_ref[j]
    first = (flag & 1) != 0
    last = (flag & 2) != 0
    skip = (flag & 4) != 0
    rb = c_ref.shape[0]
    cols = _dot_nt(eye_ref[...], tab_ref[0].astype(BF16))
    start = cols[:, 0:1] * 256.0 + cols[:, 1:2]
    count = cols[:, 2:3]
    rabs = (blk_ref[j] * rb + lax.broadcasted_iota(I32, (1, rb), 1)).astype(F32)
    seg = jnp.where((start <= rabs) & (rabs < start + count), 1.0, 0.0).astype(BF16)
    hi, lo = _split(c_ref[...])
    part = _dot(seg, hi) + _dot(seg, lo)

    @pl.when(first)
    def _():
        o_ref[...] = part

    @pl.when(jnp.logical_not(first | skip))
    def _():
        o_ref[...] += part

    @pl.when(last & jnp.logical_not(skip))
    def _():
        o_ref[...] = x_ref[...] + mod_ref[0][5:6] * o_ref[...]


def _combine(x, mod_l, b0, L, contrib, cum, cnt):
    T, D = x.shape
    tt = COMBINE_TOKENS
    R = T // tt
    rb = COMBINE_ROWS
    nr = CAPACITY_FACTOR * T
    nb = nr // rb
    nper = L // tt
    nitem = R + nb
    lo = cum[::tt]
    hi = jnp.concatenate([lo[1:], jnp.full((1,), nr, I32)])
    b_lo = jnp.minimum(lo // rb, nb - 1)
    b_hi = jnp.minimum(jnp.maximum((hi - 1) // rb, b_lo), nb - 1)
    n = b_hi - b_lo + 1
    cn = jnp.cumsum(n)
    start = cn - n
    j = jnp.arange(nitem, dtype=I32)
    valid = j < cn[-1]
    ti = jnp.minimum(jnp.searchsorted(cn, j, side='right').astype(I32), R - 1)
    blk = jnp.where(valid, b_lo[ti] + (j - start[ti]), b_hi[R - 1])
    flag = jnp.where(valid, (j == start[ti]) * 1 + (j == cn[ti] - 1) * 2, 4).astype(I32)
    tab = jnp.stack([(cum >> 8).astype(F32), (cum & 255).astype(F32), cnt.astype(F32)], axis=0)
    tab = jnp.concatenate([tab, jnp.zeros((5, T), F32)], axis=0).reshape(8, R, tt).transpose(1, 0, 2)
    eye = jnp.eye(tt, dtype=BF16)
    return pl.pallas_call(
        _combine_kernel,
        out_shape=jax.ShapeDtypeStruct((T, D), F32),
        grid_spec=pltpu.PrefetchScalarGridSpec(
            num_scalar_prefetch=3,
            grid=(nitem,),
            in_specs=[pl.BlockSpec((rb, D), lambda j, t, b, f: (b[j], 0)),
                      pl.BlockSpec((tt, D), lambda j, t, b, f: (t[j], 0)),
                      pl.BlockSpec((1, 6, D), lambda j, t, b, f: (b0 + t[j] // nper, 0, 0)),
                      pl.BlockSpec((1, 8, tt), lambda j, t, b, f: (t[j], 0, 0)),
                      pl.BlockSpec((tt, tt), lambda j, t, b, f: (0, 0))],
            out_specs=pl.BlockSpec((tt, D), lambda j, t, b, f: (t[j], 0))),
        compiler_params=_params("arbitrary"),
        name="moe_combine",
    )(ti, blk, flag, contrib, x, mod_l, tab, eye)


def _layer(x, B, L, b0, mod_l, p, kf, fc, tabs):
    hy_w = p['hy_bias'].shape[1]
    att_w = N_HEADS * HEAD_DIM
    kv_w = N_KV_HEADS * HEAD_DIM
    planes = 2 if B % 2 == 0 else 1
    z, ga, gb, q, k, v, gs = _inproj(x, mod_l, b0, L, p['norm1_g'][None], p['w_in_bf'], p['hy_conv_w'], p['hy_conv_b'],
                                     tabs['cos'], tabs['sin'], tabs['qg'], tabs['kg'], tabs['blk'], hy_w, att_w, kv_w)
    shape5 = (B // planes, planes, L // LANES, hy_w, LANES)
    z = z.reshape(shape5)
    gates = (ga.reshape(shape5), gb.reshape(shape5))
    for o in range(HY_ORDER):
        z = _longconv(z, gates[o], kf, o, p['bias_t'], fc)
    z = z.reshape(B * L // LANES, hy_w, LANES)
    att = _attention(q, k, v, p['sink_logit'], B, L)
    x, h2, aff_t = _merge(x, z, att, gs, mod_l, b0, L, p['w_hy_bf'], p['w_at_x'], p['w_o_bf'], p['norm2_g'][None],
                          p['rw_hi'], p['rw_lo'])
    T = x.shape[0]
    cap = CAPACITY_FACTOR * T // N_EXPERTS
    idx, gate, dst, cum, cnt = _topk(aff_t, cap)
    contrib = _experts(h2, idx, dst, p['exp_w_gate'], p['exp_w_up'], p['exp_w_down'], p['layer'],
                       gate.reshape(N_EXPERTS, cap, 1), cap)
    return _combine(x, mod_l, b0, L, contrib, cum, cnt)


def _rope_tables(L):
    inv = 1.0 / (ROPE_THETA ** (jnp.arange(0, HEAD_DIM, 2, dtype=F32) / HEAD_DIM))
    ang = jnp.arange(L, dtype=F32)[:, None] * inv[None, :]
    cos, sin = jnp.cos(ang), jnp.sin(ang)
    cos_t = jnp.tile(cos, (1, LANES // (HEAD_DIM // 2)))
    sin_t = jnp.tile(jnp.concatenate([-sin, sin], axis=1), (1, LANES // HEAD_DIM))
    return cos_t, sin_t


def _filter_spectrum_for(L, p, fc):
    hy_w = p['hy_bias'].shape[1]
    k3, ssq = _filter_rows(L, p)
    return _filter_spectrum(k3, ssq, fc, hy_w)


def kernel(x_prompt, x_sample, c_prompt, c_sample, ada_w, ada_b, norm1_g, norm2_g, w_in, hy_conv_w, hy_conv_b,
           flt_w1, flt_b1, flt_w2, flt_b2, flt_w3, flt_b3, flt_freq, flt_w_out, hy_bias, q_norm_g, k_norm_g,
           sink_logit, w_hy_proj, w_at_proj, w_o, router_w, exp_w_gate, exp_w_up, exp_w_down):
    depth = ada_w.shape[0]
    Bp, Lp, D = x_prompt.shape
    Bs, Ls, _ = x_sample.shape
    nb = Bp + Bs
    c_all = jnp.concatenate([c_prompt, c_sample, jnp.zeros(((-nb) % 8, D), F32)], axis=0)
    mod = _adaln(c_all, ada_w, ada_b).reshape(depth, c_all.shape[0], 6, D)
    hy_w = hy_bias.shape[2]
    att_w = N_HEADS * HEAD_DIM
    groups = [dict(B=Bp, L=Lp, b0=0, x=x_prompt.reshape(Bp * Lp, D)),
              dict(B=Bs, L=Ls, b0=Bp, x=x_sample.reshape(Bs * Ls, D))]
    blk = jnp.asarray(np.kron(np.eye(LANES // HEAD_DIM), np.ones((HEAD_DIM, HEAD_DIM))), BF16)
    for g in groups:
        g['fc'] = _fft_consts(g['L'], 2 if g['B'] % 2 == 0 else 1)
        cos_t, sin_t = _rope_tables(g['L'])
        g['tabs'] = dict(cos=cos_t, sin=sin_t, blk=blk)
    heads_per_kv = N_HEADS // N_KV_HEADS
    for l in range(depth):
        w_at_x = jnp.zeros((N_HEADS, LANES, D), BF16)
        w_at_h = w_at_proj[l].astype(BF16).reshape(N_HEADS, HEAD_DIM, D)
        for h in range(N_HEADS):
            off = (h // heads_per_kv) * HEAD_DIM
            w_at_x = w_at_x.at[h, off:off + HEAD_DIM].set(w_at_h[h])
        rw_hi, rw_lo = _split(router_w[l].T)
        p = dict(norm1_g=norm1_g[l], norm2_g=norm2_g[l], w_in_bf=w_in[l].astype(BF16),
                 hy_conv_w=hy_conv_w[l], hy_conv_b=hy_conv_b[l],
                 flt_w1=flt_w1[l], flt_b1=flt_b1[l], flt_w2=flt_w2[l], flt_b2=flt_b2[l], flt_w3=flt_w3[l],
                 flt_b3=flt_b3[l], flt_freq=flt_freq[l], flt_w_out=flt_w_out[l], hy_bias=hy_bias[l],
                 bias_t=jnp.broadcast_to(hy_bias[l].reshape(-1, 1, CONV_CH), (HY_ORDER * hy_w // CONV_CH, 8, CONV_CH)),
                 sink_logit=sink_logit[l], w_hy_bf=w_hy_proj[l].astype(BF16),
                 w_at_x=w_at_x.reshape(N_HEADS * LANES, D), w_o_bf=w_o[l].astype(BF16),
                 rw_hi=rw_hi, rw_lo=rw_lo,
                 layer=l, exp_w_gate=exp_w_gate, exp_w_up=exp_w_up, exp_w_down=exp_w_down)
        qg = jnp.tile(q_norm_g[l], LANES // HEAD_DIM)[None]
        kg = jnp.tile(k_norm_g[l], LANES // HEAD_DIM)[None]
        for g in groups:
            kf = _filter_spectrum_for(g['L'], p, g['fc'])
            tabs = dict(g['tabs'], qg=qg, kg=kg)
            g['x'] = _layer(g['x'], g['B'], g['L'], g['b0'], mod[l], p, kf, g['fc'], tabs)
    return (groups[0]['x'].reshape(Bp, Lp, D), groups[1]['x'].reshape(Bs, Ls, D))
```

```python
import functools
import math

import numpy as np
import jax
import jax.numpy as jnp
from jax import lax
from jax.experimental import pallas as pl
from jax.experimental.pallas import tpu as pltpu

F32 = jnp.float32
BF16 = jnp.bfloat16
I32 = jnp.int32

EPS = 1e-6
HEAD_DIM = 64
N_HEADS = 8
N_KV_HEADS = 2
WINDOW = 128
ROPE_THETA = 10000.0
HY_ORDER = 2
FILTER_EMB = 33
DECAY_TARGET = 0.01
FAST_DECAY_PCT = 0.3
SLOW_DECAY_PCT = 1.5
N_EXPERTS = 16
CAPACITY_FACTOR = 2
LANES = 128
FFT_N2 = LANES
VMEM_LIMIT = 52 << 20


def _params(*sem):
    return pltpu.CompilerParams(dimension_semantics=sem, vmem_limit_bytes=VMEM_LIMIT)


def _dot(a, b):
    return jnp.dot(a, b, preferred_element_type=F32)


def _dot_nt(a, b):
    return lax.dot_general(a, b, (((1,), (1,)), ((), ())), preferred_element_type=F32)


def _split(x):
    hi = x.astype(BF16)
    lo = (x - hi.astype(F32)).astype(BF16)
    return hi, lo


def _dot3(a_hi, a_lo, b_hi, b_lo):
    return _dot(a_hi, b_hi) + _dot(a_lo, b_hi) + _dot(a_hi, b_lo)


def _ada_kernel(c_ref, w_ref, b_ref, o_ref):
    c = c_ref[...]
    s = c * jax.nn.sigmoid(c)
    o_ref[0] = _dot(s.astype(BF16), w_ref[0].astype(BF16)) + b_ref[0]


def _adaln(c_all, ada_w, ada_b):
    depth, d, n = ada_w.shape
    tn = n // 4
    return pl.pallas_call(
        _ada_kernel,
        out_shape=jax.ShapeDtypeStruct((depth, c_all.shape[0], n), F32),
        grid=(depth, n // tn),
        in_specs=[pl.BlockSpec(c_all.shape, lambda l, j: (0, 0)),
                  pl.BlockSpec((1, d, tn), lambda l, j: (l, 0, j)),
                  pl.BlockSpec((1, 1, tn), lambda l, j: (l, 0, j))],
        out_specs=pl.BlockSpec((1, c_all.shape[0], tn), lambda l, j: (l, 0, j)),
        compiler_params=_params("arbitrary", "arbitrary"),
        name="adaln",
    )(c_all, ada_w, ada_b.reshape(depth, 1, n))


def _head_norm_rope(x, gain, cos, sin, blk_hi):
    sq = x * x
    hi, lo = _split(sq)
    ssum = _dot(hi, blk_hi) + _dot(lo, blk_hi)
    xn = x * lax.rsqrt(ssum * (1.0 / HEAD_DIM) + EPS) * gain
    lane = lax.broadcasted_iota(I32, x.shape, 1)
    first_half = (lane % HEAD_DIM) < (HEAD_DIM // 2)
    rot = jnp.where(first_half, pltpu.roll(xn, LANES - HEAD_DIM // 2, 1), pltpu.roll(xn, HEAD_DIM // 2, 1))
    return xn * cos + rot * sin


def _inproj_kernel(x_ref, xp_ref, xn_ref, mod_ref, g_ref, w_ref, cw_ref, cb_ref, cos_ref, sin_ref, qg_ref, kg_ref, blk_ref,
                   z_ref, ga_ref, gb_ref, q_ref, k_ref, v_ref, gs_ref, *, hy_w, att_w, kv_w, nper):
    i = pl.program_id(0)
    tm = x_ref.shape[0]
    halo = xp_ref.shape[0]
    x = jnp.concatenate([xp_ref[...], x_ref[...], xn_ref[...]], axis=0)
    mod = mod_ref[0]
    ms = jnp.mean(x * x, axis=-1, keepdims=True)
    h = (x * lax.rsqrt(ms + EPS) * g_ref[...]) * (1.0 + mod[1:2]) + mod[0:1]
    hb_all = h.astype(BF16)
    hb = hb_all[halo:halo + tm]
    s0 = 3 * hy_w
    s1 = s0 + att_w
    s2 = s1 + kv_w
    s3 = s2 + kv_w
    hy = _dot(hb_all, w_ref[:, 0:s0])
    n_all = tm + 2 * halo
    row = lax.broadcasted_iota(I32, (tm, 1), 0)
    first = (i % nper) == 0
    last = (i % nper) == nper - 1
    up = jnp.where((row == 0) & first, 0.0, pltpu.roll(hy, 1, 0)[halo:halo + tm])
    dn = jnp.where((row == tm - 1) & last, 0.0, pltpu.roll(hy, n_all - 1, 0)[halo:halo + tm])
    y = up * cw_ref[0:1, :] + hy[halo:halo + tm] * cw_ref[1:2, :] + dn * cw_ref[2:3, :] + cb_ref[...]
    for r in range(tm // LANES):
        blk_y = y[r * LANES:(r + 1) * LANES, :]
        z_ref[r] = blk_y[:, 0:hy_w].T
        ga_ref[r] = blk_y[:, hy_w:2 * hy_w].T
        gb_ref[r] = blk_y[:, 2 * hy_w:3 * hy_w].T
    q = _dot(hb, w_ref[:, s0:s1])
    k = _dot(hb, w_ref[:, s1:s2])
    v_ref[...] = _dot(hb, w_ref[:, s2:s3]).astype(BF16)
    gs_ref[...] = jax.nn.sigmoid(_dot(hb, w_ref[:, s3:]))
    cos = cos_ref[...]
    sin = sin_ref[...]
    blk = blk_ref[...]
    k_ref[...] = _head_norm_rope(k, kg_ref[...], cos, sin, blk).astype(BF16)
    lane = lax.broadcasted_iota(I32, cos.shape, 1)
    low = lane < HEAD_DIM
    heads_per_kv = N_HEADS // N_KV_HEADS
    for ci in range(att_w // LANES):
        qr = _head_norm_rope(q[:, ci * LANES:(ci + 1) * LANES], qg_ref[...], cos, sin, blk) * (HEAD_DIM ** -0.5)
        qs = pltpu.roll(qr, HEAD_DIM, 1)
        kvh = (2 * ci) // heads_per_kv
        if kvh == 0:
            h0 = jnp.where(low, qr, 0.0)
            h1 = jnp.where(low, qs, 0.0)
        else:
            h0 = jnp.where(low, 0.0, qs)
            h1 = jnp.where(low, 0.0, qr)
        q_ref[:, (2 * ci) * LANES:(2 * ci + 1) * LANES] = h0.astype(BF16)
        q_ref[:, (2 * ci + 1) * LANES:(2 * ci + 2) * LANES] = h1.astype(BF16)


def _inproj(x, mod_l, b0, L, norm_g, w_in_bf, conv_w, conv_b, cos_t, sin_t, qg, kg, blk, hy_w, att_w, kv_w):
    T, D = x.shape
    tm = 512
    halo = 8
    nper = L // tm
    r8 = tm // halo
    nb8 = T // halo
    in_w = w_in_bf.shape[1]
    kern = functools.partial(_inproj_kernel, hy_w=hy_w, att_w=att_w, kv_w=kv_w, nper=nper)
    stream = jax.ShapeDtypeStruct((T // LANES, hy_w, LANES), F32)
    sspec = pl.BlockSpec((tm // LANES, hy_w, LANES), lambda i: (i, 0, 0))
    full = lambda i: (0, 0)
    return pl.pallas_call(
        kern,
        out_shape=(stream, stream, stream,
                   jax.ShapeDtypeStruct((T, N_HEADS * LANES), BF16),
                   jax.ShapeDtypeStruct((T, kv_w), BF16),
                   jax.ShapeDtypeStruct((T, kv_w), BF16),
                   jax.ShapeDtypeStruct((T, in_w - 3 * hy_w - att_w - 2 * kv_w), F32)),
        grid=(T // tm,),
        in_specs=[pl.BlockSpec((tm, D), lambda i: (i, 0)),
                  pl.BlockSpec((halo, D), lambda i: (jnp.maximum(i * r8 - 1, 0), 0)),
                  pl.BlockSpec((halo, D), lambda i: (jnp.minimum((i + 1) * r8, nb8 - 1), 0)),
                  pl.BlockSpec((1, 6, D), lambda i: (b0 + i // nper, 0, 0)),
                  pl.BlockSpec((1, D), full),
                  pl.BlockSpec((D, in_w), full),
                  pl.BlockSpec((3, 3 * hy_w), full),
                  pl.BlockSpec((1, 3 * hy_w), full),
                  pl.BlockSpec((tm, LANES), lambda i: (i % nper, 0)),
                  pl.BlockSpec((tm, LANES), lambda i: (i % nper, 0)),
                  pl.BlockSpec((1, LANES), full),
                  pl.BlockSpec((1, LANES), full),
                  pl.BlockSpec((LANES, LANES), full)],
        out_specs=(sspec, sspec, sspec,
                   pl.BlockSpec((tm, N_HEADS * LANES), lambda i: (i, 0)),
                   pl.BlockSpec((tm, kv_w), lambda i: (i, 0)),
                   pl.BlockSpec((tm, kv_w), lambda i: (i, 0)),
                   pl.BlockSpec((tm, in_w - 3 * hy_w - att_w - 2 * kv_w), lambda i: (i, 0))),
        compiler_params=_params("arbitrary"),
        name="inproj",
    )(x, x, x, mod_l, norm_g, w_in_bf, conv_w, conv_b.reshape(1, -1), cos_t, sin_t, qg, kg, blk)


def _filter_kernel(z_ref, tl_ref, w1_ref, b1_ref, w2_ref, b2_ref, w3_ref, b3_ref, fr_ref, woa_ref, wob_ref, dl_ref,
                   k_ref, ssq_ref, *, L):
    i = pl.program_id(0)
    z = z_ref[...]
    tr = z.shape[0]
    half = tr // 2
    fr = fr_ref[...]

    def lin(a, w_ref_, b_ref_):
        ah, al = _split(a)
        wh, wl = _split(w_ref_[...])
        return _dot3(ah, al, wh, wl) + b_ref_[...]

    zz = jnp.concatenate([z[:half], z[half:]], axis=1)
    h = jnp.sin(fr * lin(zz, w1_ref, b1_ref))
    h = jnp.sin(fr * lin(h, w2_ref, b2_ref))
    h = jnp.sin(fr * lin(h, w3_ref, b3_ref))
    hh, hl = _split(h)
    wts = (_split(woa_ref[0]), _split(wob_ref[0]))

    @pl.when(i == 0)
    def _():
        ssq_ref[...] = jnp.zeros_like(ssq_ref)

    lane = lax.broadcasted_iota(I32, (1, LANES), 1)
    per_half = half // LANES
    for r in range(tr // LANES):
        wh, wl = wts[r // per_half]
        rows = slice((r % per_half) * LANES, (r % per_half + 1) * LANES)
        kt = _dot_nt(wh, hh[rows]) + _dot_nt(wl, hh[rows]) + _dot_nt(wh, hl[rows])
        kt = kt * jnp.exp(-dl_ref[...] * tl_ref[r])
        kt = jnp.where((i * tr + r * LANES + lane) == L, 0.0, kt)
        k_ref[r] = kt
        ssq_ref[...] += kt * kt


def _filter_rows(L, p):
    hid = p['flt_w2'].shape[0]
    hy_w = p['hy_bias'].shape[1]
    cw = HY_ORDER * hy_w
    tr = 512
    N = 2 * L
    n1 = N // LANES
    t = jnp.linspace(0.0, 1.0, L, dtype=F32)[:, None]
    bands = (FILTER_EMB - 1) // 2
    f = jnp.linspace(1e-4, bands - 1, bands, dtype=F32)[None, :]
    w = (2.0 * math.pi / L) * jnp.arange(L, dtype=F32)[:, None]
    zf = jnp.concatenate([t, jnp.cos(f * w), -jnp.sin(f * w)], axis=-1)
    z = jnp.concatenate([zf, zf[L - 1:L], zf[:0:-1]], axis=0)
    tl = z[:, 0].reshape(n1, 1, LANES)
    z = jnp.pad(z, ((0, 0), (0, LANES - FILTER_EMB)))
    eye2 = jnp.eye(2, dtype=F32)
    w1 = jnp.concatenate([jnp.pad(p['flt_w1'], ((0, LANES - FILTER_EMB), (0, hid))),
                          jnp.pad(p['flt_w1'], ((0, LANES - FILTER_EMB), (hid, 0)))], axis=0)
    w2 = jnp.kron(eye2, p['flt_w2'])
    w3 = jnp.kron(eye2, p['flt_w3'])
    two = lambda v: jnp.tile(v, 2)[None]
    wo_t = p['flt_w_out'].reshape(hid, 2, cw).transpose(1, 2, 0)
    woa = jnp.pad(wo_t, ((0, 0), (0, 0), (0, hid)))
    wob = jnp.pad(wo_t, ((0, 0), (0, 0), (hid, 0)))
    max_decay = math.log(DECAY_TARGET) / FAST_DECAY_PCT
    min_decay = math.log(DECAY_TARGET) / SLOW_DECAY_PCT
    deltas = jnp.abs(jnp.linspace(min_decay, max_decay, hy_w, dtype=F32))
    dl = jnp.tile(deltas, HY_ORDER)[:, None]
    nblk = N // tr
    halfb = nblk // 2
    full = lambda i: (0, 0)
    h2 = 2 * hid
    return pl.pallas_call(
        functools.partial(_filter_kernel, L=L),
        out_shape=(jax.ShapeDtypeStruct((n1, cw, LANES), F32), jax.ShapeDtypeStruct((cw, LANES), F32)),
        grid=(nblk,),
        in_specs=[pl.BlockSpec((tr, LANES), lambda i: (i, 0)),
                  pl.BlockSpec((tr // LANES, 1, LANES), lambda i: (i, 0, 0)),
                  pl.BlockSpec((2 * LANES, h2), full), pl.BlockSpec((1, h2), full),
                  pl.BlockSpec((h2, h2), full), pl.BlockSpec((1, h2), full),
                  pl.BlockSpec((h2, h2), full), pl.BlockSpec((1, h2), full),
                  pl.BlockSpec((1, h2), full),
                  pl.BlockSpec((1, cw, h2), lambda i: (i // halfb, 0, 0)),
                  pl.BlockSpec((1, cw, h2), lambda i: (i // halfb, 0, 0)),
                  pl.BlockSpec((cw, 1), full)],
        out_specs=(pl.BlockSpec((tr // LANES, cw, LANES), lambda i: (i, 0, 0)), pl.BlockSpec((cw, LANES), full)),
        compiler_params=_params("arbitrary"),
        name="hyena_filter",
    )(z, tl, w1, two(p['flt_b1']), w2, two(p['flt_b2']), w3, two(p['flt_b3']), two(p['flt_freq']), woa, wob, dl)


def _fft_consts(L, planes):
    N = 2 * L
    n1 = N // FFT_N2
    n1h = n1 // 2
    k1 = np.arange(n1)[:, None]
    th = 2.0 * np.pi * k1 * np.arange(n1)[None, :] / n1
    c, s = np.cos(th), np.sin(th)
    ch, sh = c[:, :n1h], s[:, :n1h]
    if planes == 1:
        m1 = np.concatenate([ch, -sh], axis=0)
        m3 = np.concatenate([ch.T, -sh.T], axis=1) / N
    else:
        m1 = np.block([[ch, sh], [-sh, ch]])
        m3 = np.block([[ch.T, -sh.T], [sh.T, ch.T]]) / N
    m1_full = np.concatenate([c, -s], axis=0)
    ph = 2.0 * np.pi * k1 * np.arange(FFT_N2)[None, :] / N
    twr, twi = np.cos(ph), -np.sin(ph)
    a = 2.0 * np.pi * np.arange(FFT_N2)[:, None] * np.arange(FFT_N2)[None, :] / FFT_N2
    ca, sa = np.cos(a), np.sin(a)
    rt = np.block([[ca, -sa], [sa, ca]])
    rit = np.block([[ca, sa], [-sa, ca]])
    bf = lambda m: jnp.asarray(np.asarray(m, np.float32).astype(BF16))
    return dict(n1=n1, n1h=n1h, m1=bf(m1), m3=bf(m3), m1_full=bf(m1_full),
                twr=jnp.asarray(twr, F32), twi=jnp.asarray(twi, F32), rt=bf(rt), rit=bf(rit))


CONV_CH = 16


def _load_channels(ref, lead):
    return jnp.concatenate([ref[lead + (slice(None), c, slice(None))] for c in range(CONV_CH)], axis=1)


def _fwd_lane_dft(ar, ai, twr, twi, rt):
    pieces = []
    for c in range(CONV_CH):
        r = ar[:, c * LANES:(c + 1) * LANES]
        i = ai[:, c * LANES:(c + 1) * LANES]
        pieces.append(jnp.concatenate([r * twr - i * twi, r * twi + i * twr], axis=1))
    return _dot(jnp.concatenate(pieces, axis=0).astype(BF16), rt)


def _spectrum_kernel(k_ref, ssq_ref, m1_ref, twr_ref, twi_ref, rt_ref, o_ref):
    n1 = twr_ref.shape[0]
    a = _dot(m1_ref[...], _load_channels(k_ref, ()).astype(BF16))
    x = _fwd_lane_dft(a[:n1], a[n1:], twr_ref[...], twi_ref[...], rt_ref[...])
    scale = lax.rsqrt(jnp.sum(ssq_ref[...], axis=1, keepdims=True) + EPS)
    for c in range(CONV_CH):
        o_ref[0, :, c * 2 * LANES:(c + 1) * 2 * LANES] = x[c * n1:(c + 1) * n1] * scale[c:c + 1, :]


def _filter_spectrum(k3, ssq, fc, hy_w):
    n1 = fc['n1']
    ct = CONV_CH
    ntile = hy_w // ct
    full = lambda o, j: (0, 0)
    return pl.pallas_call(
        _spectrum_kernel,
        out_shape=jax.ShapeDtypeStruct((HY_ORDER, n1, hy_w * 2 * LANES), F32),
        grid=(HY_ORDER, ntile),
        in_specs=[pl.BlockSpec((n1, ct, LANES), lambda o, j: (0, o * ntile + j, 0)),
                  pl.BlockSpec((ct, LANES), lambda o, j: (o * ntile + j, 0)),
                  pl.BlockSpec((2 * n1, n1), full),
                  pl.BlockSpec((n1, LANES), full), pl.BlockSpec((n1, LANES), full),
                  pl.BlockSpec((2 * LANES, 2 * LANES), full)],
        out_specs=pl.BlockSpec((1, n1, ct * 2 * LANES), lambda o, j: (o, 0, j)),
        compiler_params=_params("arbitrary", "arbitrary"),
        name="filter_spectrum",
    )(k3, ssq, fc['m1_full'], fc['twr'], fc['twi'], fc['rt'])


def _longconv_kernel(z_ref, g_ref, kf_ref, b_ref, m1_ref, twr_ref, twi_ref, rt_ref, rit_ref, m3_ref, o_ref):
    ct = CONV_CH
    n1 = twr_ref.shape[0]
    planes, n1h = z_ref.shape[1], z_ref.shape[2]
    z = jnp.concatenate([_load_channels(z_ref, (0, p)) for p in range(planes)], axis=0)
    a = _dot(m1_ref[...], z.astype(BF16))
    twr = twr_ref[...]
    twi = twi_ref[...]
    x = _fwd_lane_dft(a[:n1], a[n1:], twr, twi, rt_ref[...])
    ys = []
    for c in range(ct):
        xc = x[c * n1:(c + 1) * n1]
        kc = kf_ref[0, :, c * 2 * LANES:(c + 1) * 2 * LANES]
        xr, xi = xc[:, :LANES], xc[:, LANES:]
        kr, ki = kc[:, :LANES], kc[:, LANES:]
        ys.append(jnp.concatenate([xr * kr - xi * ki, xr * ki + xi * kr], axis=1))
    bp = _dot(jnp.concatenate(ys, axis=0).astype(BF16), rit_ref[...])
    brs, bis = [], []
    for c in range(ct):
        bc = bp[c * n1:(c + 1) * n1]
        r, i = bc[:, :LANES], bc[:, LANES:]
        brs.append(r * twr + i * twi)
        bis.append(i * twr - r * twi)
    bst = jnp.concatenate([jnp.concatenate(brs, axis=1), jnp.concatenate(bis, axis=1)], axis=0)
    yt = _dot(m3_ref[...], bst.astype(BF16))
    for p in range(planes):
        rows = slice(p * n1h, (p + 1) * n1h)
        for c in range(ct):
            lanes = slice(c * LANES, (c + 1) * LANES)
            o_ref[0, p, :, c, :] = g_ref[0, p, :, c, :] * (yt[rows, lanes] + b_ref[0, 0:1, c:c + 1] * z[rows, lanes])


def _longconv(z5, g5, kf, order, bias_t, fc):
    npair, planes, n1h, ch, _ = z5.shape
    n1 = fc['n1']
    ct = CONV_CH
    ntile = ch // ct
    full = lambda p, j: (0, 0)
    m1r, m1c = 2 * n1, planes * n1h
    blk = pl.BlockSpec((1, planes, n1h, ct, LANES), lambda p, j: (p, 0, 0, j, 0))
    return pl.pallas_call(
        _longconv_kernel,
        out_shape=jax.ShapeDtypeStruct(z5.shape, F32),
        grid=(npair, ntile),
        in_specs=[blk, blk,
                  pl.BlockSpec((1, n1, ct * 2 * LANES), lambda p, j: (order, 0, j)),
                  pl.BlockSpec((1, 8, ct), lambda p, j: (order * ntile + j, 0, 0)),
                  pl.BlockSpec((m1r, m1c), full),
                  pl.BlockSpec((n1, LANES), full), pl.BlockSpec((n1, LANES), full),
                  pl.BlockSpec((2 * LANES, 2 * LANES), full), pl.BlockSpec((2 * LANES, 2 * LANES), full),
                  pl.BlockSpec((m1c, m1r), full)],
        out_specs=blk,
        compiler_params=_params("arbitrary", "arbitrary"),
        name="hyena_longconv",
    )(z5, g5, kf, bias_t, fc['m1'], fc['twr'], fc['twi'], fc['rt'], fc['rit'], fc['m3'])


def _attn_kernel(sink_ref, q_ref, kp_ref, km_ref, kn_ref, vp_ref, vm_ref, vn_ref, o_ref, *, L, tq):
    i = pl.program_id(1)
    t0 = i * tq
    kk = jnp.concatenate([kp_ref[...], km_ref[...], kn_ref[...]], axis=0)
    vv = jnp.concatenate([vp_ref[...], vm_ref[...], vn_ref[...]], axis=0)
    sub = WINDOW
    tk = 3 * WINDOW
    di = lax.broadcasted_iota(I32, (sub, tk), 0)
    dj = lax.broadcasted_iota(I32, (sub, tk), 1)
    band = jnp.abs(dj - WINDOW - di) <= WINDOW
    hid = lax.broadcasted_iota(I32, (N_HEADS, 1, 1), 0)
    sink = jnp.zeros((N_HEADS, 1, 1), F32)
    for h in range(N_HEADS):
        sink = jnp.where(hid == h, sink_ref[h], sink)
    for a in range(tq // sub):
        kpos = t0 + (a - 1) * sub + dj
        valid = band & (kpos >= 0) & (kpos < L)
        rows = slice(a * sub, (a + 1) * sub)
        qa = jnp.concatenate([q_ref[rows, h * LANES:(h + 1) * LANES] for h in range(N_HEADS)], axis=0)
        s = _dot_nt(qa, kk[a * sub:a * sub + tk]).reshape(N_HEADS, sub, tk)
        s = jnp.where(valid, s, -1e30)
        m = jnp.maximum(jnp.max(s, axis=-1, keepdims=True), sink)
        p = jnp.exp(s - m)
        denom = jnp.sum(p, axis=-1, keepdims=True) + jnp.exp(sink - m)
        o = _dot((p / denom).astype(BF16).reshape(N_HEADS * sub, tk), vv[a * sub:a * sub + tk]).astype(BF16)
        for h in range(N_HEADS):
            o_ref[rows, h * LANES:(h + 1) * LANES] = o[h * sub:(h + 1) * sub]


def _attention(q, k, v, sink, B, L):
    T = q.shape[0]
    tq = 512
    nq = L // tq
    r = tq // WINDOW
    nb = L // WINDOW
    kv_w = k.shape[1]
    kp = pl.BlockSpec((WINDOW, kv_w), lambda b, i: (b * nb + jnp.maximum(i * r - 1, 0), 0))
    km = pl.BlockSpec((tq, kv_w), lambda b, i: (b * nq + i, 0))
    kn = pl.BlockSpec((WINDOW, kv_w), lambda b, i: (b * nb + jnp.minimum((i + 1) * r, nb - 1), 0))
    qs = pl.BlockSpec((tq, N_HEADS * LANES), lambda b, i: (b * nq + i, 0))
    return pl.pallas_call(
        functools.partial(_attn_kernel, L=L, tq=tq),
        out_shape=jax.ShapeDtypeStruct((T, N_HEADS * LANES), BF16),
        grid=(B, nq),
        in_specs=[pl.BlockSpec(memory_space=pltpu.SMEM), qs, kp, km, kn, kp, km, kn],
        out_specs=qs,
        compiler_params=_params("arbitrary", "arbitrary"),
        name="band_attention",
    )(sink, q, k, k, k, v, v, v)


def _merge_kernel(x_ref, z_ref, a_ref, gs_ref, mod_ref, why_ref, wat_ref, wo_ref, g2_ref, rwh_ref, rwl_ref,
                  xo_ref, h2_ref, aff_ref):
    d = x_ref.shape[1]
    mod = mod_ref[0]
    z = jnp.concatenate([z_ref[r].T for r in range(z_ref.shape[0])], axis=0)
    y_hy = _dot(z.astype(BF16), why_ref[...])
    y_at = _dot(a_ref[...], wat_ref[...])
    gs = gs_ref[...]
    merged = gs[:, :d] * y_hy + gs[:, d:] * y_at
    x = x_ref[...] + mod[2:3] * _dot(merged.astype(BF16), wo_ref[...])
    xo_ref[...] = x
    ms = jnp.mean(x * x, axis=-1, keepdims=True)
    h = (x * lax.rsqrt(ms + EPS) * g2_ref[...]) * (1.0 + mod[4:5]) + mod[3:4]
    h2_ref[...] = h
    hh, hl = _split(h)
    rh, rl = rwh_ref[...], rwl_ref[...]
    logits = _dot_nt(rh, hh) + _dot_nt(rl, hh) + _dot_nt(rh, hl)
    m = jnp.max(logits, axis=0, keepdims=True)
    e = jnp.exp(logits - m)
    aff_ref[...] = e / jnp.sum(e, axis=0, keepdims=True)


def _merge(x, z, att, gs, mod_l, b0, L, w_hy, w_at_x, w_o, norm2_g, rw_hi, rw_lo):
    T, D = x.shape
    tm = 512
    nper = L // tm
    E = rw_hi.shape[0]
    full = lambda i: (0, 0)
    row = lambda w: pl.BlockSpec((tm, w), lambda i: (i, 0))
    return pl.pallas_call(
        _merge_kernel,
        out_shape=(jax.ShapeDtypeStruct((T, D), F32), jax.ShapeDtypeStruct((T, D), F32),
                   jax.ShapeDtypeStruct((E, T), F32)),
        grid=(T // tm,),
        in_specs=[row(D), pl.BlockSpec((tm // LANES,) + z.shape[1:], lambda i: (i, 0, 0)), row(att.shape[1]), row(gs.shape[1]),
                  pl.BlockSpec((1, 6, D), lambda i: (b0 + i // nper, 0, 0)),
                  pl.BlockSpec(w_hy.shape, full), pl.BlockSpec(w_at_x.shape, full), pl.BlockSpec(w_o.shape, full),
                  pl.BlockSpec((1, D), full), pl.BlockSpec((E, D), full), pl.BlockSpec((E, D), full)],
        out_specs=(row(D), row(D), pl.BlockSpec((E, tm), lambda i: (0, i))),
        compiler_params=_params("arbitrary"),
        name="merge_router",
    )(x, z, att, gs, mod_l, w_hy, w_at_x, w_o, norm2_g, rw_hi, rw_lo)


def _cumsum_rows(mb, u, ones, ls):
    rc = _dot(mb, u)
    totb = _dot(mb, ones)
    offs = _dot(ls, totb.astype(BF16))
    return rc, totb, offs


def _topk_kernel(aff_ref, u_ref, ut_ref, ones_ref, ls_ref, idx_ref, gate_ref, dst_ref, cum_ref, cnt_ref,
                 thr_ref, sel_ref, rank_ref, *, cap):
    E, R, _ = aff_ref.shape
    bits = pltpu.bitcast(aff_ref[...], I32)

    def search(i, cur):
        cand = cur | jnp.left_shift(jnp.int32(1), 30 - i)
        ge = jnp.where(bits >= cand, 1.0, 0.0)
        c = jnp.sum(jnp.sum(ge, axis=2, keepdims=True), axis=1, keepdims=True)
        return jnp.where(c >= cap, cand, cur)

    thr = lax.fori_loop(0, 31, search, jnp.zeros((E, 1, 1), I32))
    thr_ref[...] = jnp.broadcast_to(thr, thr_ref.shape)
    u = u_ref[...]
    ut = ut_ref[...]
    ones = ones_ref[...]
    ls = ls_ref[...]

    def select(e, carry):
        cnt_run, cum = carry
        b = pltpu.bitcast(aff_ref[e], I32)
        t = thr_ref[e][0:1, :]
        gt = b > t
        eq = b == t
        ngt = jnp.sum(jnp.sum(jnp.where(gt, 1.0, 0.0), axis=1, keepdims=True), axis=0, keepdims=True)
        need = cap - ngt
        erc, _, eoffs = _cumsum_rows(jnp.where(eq, 1.0, 0.0).astype(BF16), u, ones, ls)
        sel = jnp.where(gt | (eq & ((erc + eoffs) <= need)), 1.0, 0.0)
        rc, _, offs = _cumsum_rows(sel.astype(BF16), u, ones, ls)
        sel_ref[e] = sel
        rank_ref[e] = cnt_run
        return cnt_run + sel, cum + (rc + offs - sel)

    zero = jnp.zeros((R, LANES), F32)
    cnt, cum = lax.fori_loop(0, E, select, (zero, zero))
    cnt_ref[...] = cnt.astype(I32)
    cum_ref[...] = cum.astype(I32)
    s_row = lax.broadcasted_iota(I32, (1, cap), 1).astype(F32)
    r_col = lax.broadcasted_iota(I32, (R, 1), 0).astype(F32)
    c_col = lax.broadcasted_iota(I32, (LANES, 1), 0).astype(F32)

    def pick(vt_pieces, ohb, hit):
        acc = _dot(vt_pieces[0], ohb)
        for piece in vt_pieces[1:]:
            acc = acc + _dot(piece, ohb)
        return jnp.sum(jnp.where(hit, acc, 0.0), axis=0, keepdims=True)

    def compact(e, carry):
        sel = sel_ref[e]
        _, totb, offs = _cumsum_rows(sel.astype(BF16), u, ones, ls)
        offs_col = offs[:, 0:1]
        end_col = offs_col + totb[:, 0:1]
        oht = (offs_col <= s_row) & (s_row < end_col)
        row_s = jnp.sum(jnp.where(oht, r_col, 0.0), axis=0, keepdims=True)
        offs_s = jnp.sum(jnp.where(oht, offs_col, 0.0), axis=0, keepdims=True)
        ohb = jnp.where(oht, 1.0, 0.0).astype(BF16)
        rct = _dot(ut, sel.T.astype(BF16))
        rcs = _dot(rct.astype(BF16), ohb)
        col_s = jnp.sum(jnp.where(rcs <= (s_row - offs_s), 1.0, 0.0), axis=0, keepdims=True)
        hit = c_col == col_s
        idx_ref[pl.ds(e, 1), :] = (row_s * LANES + col_s).astype(I32)
        at = aff_ref[e].T
        p0 = at.astype(BF16)
        r1 = at - p0.astype(F32)
        p1 = r1.astype(BF16)
        p2 = (r1 - p1.astype(F32)).astype(BF16)
        gate_ref[pl.ds(e, 1), :] = pick((p0, p1, p2), ohb, hit)
        d = (cum + rank_ref[e]).T
        d_hi = (d * (1.0 / 256.0)).astype(I32).astype(F32)
        d_lo = d - 256.0 * d_hi
        dst = 256.0 * pick((d_hi.astype(BF16),), ohb, hit) + pick((d_lo.astype(BF16),), ohb, hit)
        dst_ref[pl.ds(e, 1), :] = dst.astype(I32)
        return carry

    lax.fori_loop(0, E, compact, 0)


def _topk(aff_t, cap):
    E, T = aff_t.shape
    R = T // LANES
    u = jnp.asarray(np.triu(np.ones((LANES, LANES))), BF16)
    ut = jnp.asarray(np.tril(np.ones((LANES, LANES))), BF16)
    ones = jnp.ones((LANES, LANES), BF16)
    ls = jnp.asarray(np.tril(np.ones((R, R)), -1), BF16)
    z2 = lambda: (0, 0)
    z3 = lambda: (0, 0, 0)
    slot = jax.ShapeDtypeStruct((E, cap), I32)
    tok = jax.ShapeDtypeStruct((R, LANES), I32)
    idx, gate, dst, cum, cnt = pl.pallas_call(
        functools.partial(_topk_kernel, cap=cap),
        out_shape=(slot, jax.ShapeDtypeStruct((E, cap), F32), slot, tok, tok),
        in_specs=[pl.BlockSpec((E, R, LANES), z3), pl.BlockSpec((LANES, LANES), z2), pl.BlockSpec((LANES, LANES), z2),
                  pl.BlockSpec((LANES, LANES), z2), pl.BlockSpec((R, R), z2)],
        out_specs=(pl.BlockSpec((E, cap), z2), pl.BlockSpec((E, cap), z2), pl.BlockSpec((E, cap), z2),
                   pl.BlockSpec((R, LANES), z2), pl.BlockSpec((R, LANES), z2)),
        scratch_shapes=[pltpu.VMEM((E, 8, LANES), I32), pltpu.VMEM((E, R, LANES), F32),
                        pltpu.VMEM((E, R, LANES), F32)],
        compiler_params=pltpu.CompilerParams(vmem_limit_bytes=VMEM_LIMIT),
        name="expert_topk",
    )(aff_t.reshape(E, R, LANES), u, ut, ones, ls)
    return idx, gate, dst, cum.reshape(T), cnt.reshape(T)


ROW_CHUNKS = 2
FF_TILE = 256


def _expert_kernel(src_ref, dst_ref, h_ref, wg_ref, wu_ref, wd_ref, gate_ref, out_ref,
                   xbuf, xb, acc, gsem, ssem, *, cap, per):
    e = pl.program_id(0)
    f = pl.program_id(1)
    ne = pl.num_programs(0)
    nf = pl.num_programs(1)
    slot = e % 2
    other = 1 - slot
    sched = nf * per

    def gather_row(entry, r):
        pltpu.make_async_copy(h_ref.at[pl.ds(src_ref[entry], 1), :], xbuf.at[pl.ds(r, 1), :], gsem).start()

    def scatter_row(src_slot, entry, r):
        pltpu.make_async_copy(acc.at[src_slot, pl.ds(r, 1), :], out_ref.at[pl.ds(dst_ref[entry], 1), :],
                              ssem.at[src_slot]).start()

    def schedule_done(sem):
        pltpu.make_async_copy(h_ref.at[pl.ds(0, sched), :], xbuf, sem).wait()

    @pl.when((e == 0) & (f == 0))
    def _():
        def body(s, c):
            gather_row(s, s)
            return c

        lax.fori_loop(0, sched, body, 0, unroll=8)
        acc[...] = jnp.zeros(acc.shape, F32)

    @pl.when((e > 0) & (f == 0))
    def _():
        schedule_done(ssem.at[slot])
        acc[slot, pl.ds(0, cap), :] = jnp.zeros((cap, acc.shape[2]), F32)

    @pl.when(f == 0)
    def _():
        schedule_done(gsem)
        xb[...] = xbuf[pl.ds(0, cap), :].astype(BF16)

    wg = wg_ref[0, 0].astype(BF16)
    wu = wu_ref[0, 0].astype(BF16)
    wd = wd_ref[0, 0].astype(BF16)
    gbase = (jnp.minimum(e + 1, ne - 1) * nf + f) * per
    sbase = (e * nf + f) * per
    row0 = f * per
    share = per // ROW_CHUNKS
    chunk = cap // ROW_CHUNKS
    for c in range(ROW_CHUNKS):
        for s in range(c * share, (c + 1) * share):
            gather_row(gbase + s, row0 + s)
            scatter_row(other, sbase + s, row0 + s)
        rows = pl.ds(c * chunk, chunk)
        x = xb[rows, :]
        g = _dot(x, wg)
        u = _dot(x, wu)
        hid = (g * jax.nn.sigmoid(g) * u).astype(BF16)
        acc[slot, rows, :] += _dot(hid, wd)

    @pl.when(f == nf - 1)
    def _():
        acc[slot, pl.ds(0, cap), :] = gate_ref[0] * acc[slot, pl.ds(0, cap), :]

    @pl.when((f == nf - 1) & (e == ne - 1))
    def _():
        def body(s, c):
            scatter_row(slot, ne * sched + s, s)
            return c

        lax.fori_loop(0, sched, body, 0, unroll=8)
        schedule_done(gsem)
        schedule_done(ssem.at[other])
        schedule_done(ssem.at[slot])


def _experts(h2, idx, dst, w_gate, w_up, w_down, layer, gate_col, cap):
    _, E, D, F = w_gate.shape
    ft = FF_TILE
    nf = F // ft
    per = -(-cap // (nf * 8 * ROW_CHUNKS)) * (8 * ROW_CHUNKS)
    sched = nf * per
    nr = E * cap
    assert E >= 2 and cap % ROW_CHUNKS == 0 and sched >= cap
    rows = jnp.arange(sched, dtype=I32)
    real = rows < cap
    rc = jnp.minimum(rows, cap - 1)
    src_tab = idx[:, rc]
    odd = (jnp.arange(E + 1, dtype=I32)[:, None] % 2) == 1
    spare = jnp.where(odd, nr + sched + (rows[None, :] - cap), nr + rows[None, :])
    dst_prev = jnp.concatenate([jnp.zeros((1, sched), I32), dst[:, rc]], axis=0)
    dst_tab = jnp.where(real[None, :] & (jnp.arange(E + 1)[:, None] > 0), dst_prev, spare)
    return pl.pallas_call(
        functools.partial(_expert_kernel, cap=cap, per=per),
        out_shape=jax.ShapeDtypeStruct((nr + 2 * sched - cap, D), F32),
        grid_spec=pltpu.PrefetchScalarGridSpec(
            num_scalar_prefetch=2,
            grid=(E, nf),
            in_specs=[pl.BlockSpec(memory_space=pl.ANY),
                      pl.BlockSpec((1, 1, D, ft), lambda e, f, i, d: (layer, e, 0, f)),
                      pl.BlockSpec((1, 1, D, ft), lambda e, f, i, d: (layer, e, 0, f)),
                      pl.BlockSpec((1, 1, ft, D), lambda e, f, i, d: (layer, e, f, 0)),
                      pl.BlockSpec((1, cap, 1), lambda e, f, i, d: (e, 0, 0))],
            out_specs=pl.BlockSpec(memory_space=pl.ANY),
            scratch_shapes=[pltpu.VMEM((sched, D), F32), pltpu.VMEM((cap, D), BF16), pltpu.VMEM((2, sched, D), F32),
                            pltpu.SemaphoreType.DMA, pltpu.SemaphoreType.DMA((2,))]),
        compiler_params=_params("arbitrary", "arbitrary"),
        name="expert_ffn",
    )(src_tab.reshape(-1), dst_tab.reshape(-1), h2, w_gate, w_up, w_down, gate_col)


COMBINE_ROWS = 512
COMBINE_TOKENS = 256


def _combine_kernel(tile_ref, blk_ref, flag_ref, c_ref, x_ref, mod_ref, tab_ref, eye_ref, o_ref):
    j = pl.program_id(0)
    flag = flag_ref[j]
    first = (flag & 1) != 0
    last = (flag & 2) != 0
    skip = (flag & 4) != 0
    rb = c_ref.shape[0]
    cols = _dot_nt(eye_ref[...], tab_ref[0].astype(BF16))
    start = cols[:, 0:1] * 256.0 + cols[:, 1:2]
    count = cols[:, 2:3]
    rabs = (blk_ref[j] * rb + lax.broadcasted_iota(I32, (1, rb), 1)).astype(F32)
    seg = jnp.where((start <= rabs) & (rabs < start + count), 1.0, 0.0).astype(BF16)
    hi, lo = _split(c_ref[...])
    part = _dot(seg, hi) + _dot(seg, lo)

    @pl.when(first)
    def _():
        o_ref[...] = part

    @pl.when(jnp.logical_not(first | skip))
    def _():
        o_ref[...] += part

    @pl.when(last & jnp.logical_not(skip))
    def _():
        o_ref[...] = x_ref[...] + mod_ref[0][5:6] * o_ref[...]


def _combine(x, mod_l, b0, L, contrib, cum, cnt):
    T, D = x.shape
    tt = COMBINE_TOKENS
    R = T // tt
    rb = COMBINE_ROWS
    nr = CAPACITY_FACTOR * T
    nb = nr // rb
    nper = L // tt
    nitem = R + nb
    lo = cum[::tt]
    hi = jnp.concatenate([lo[1:], jnp.full((1,), nr, I32)])
    b_lo = jnp.minimum(lo // rb, nb - 1)
    b_hi = jnp.minimum(jnp.maximum((hi - 1) // rb, b_lo), nb - 1)
    n = b_hi - b_lo + 1
    cn = jnp.cumsum(n)
    start = cn - n
    j = jnp.arange(nitem, dtype=I32)
    valid = j < cn[-1]
    ti = jnp.minimum(jnp.searchsorted(cn, j, side='right').astype(I32), R - 1)
    blk = jnp.where(valid, b_lo[ti] + (j - start[ti]), b_hi[R - 1])
    flag = jnp.where(valid, (j == start[ti]) * 1 + (j == cn[ti] - 1) * 2, 4).astype(I32)
    tab = jnp.stack([(cum >> 8).astype(F32), (cum & 255).astype(F32), cnt.astype(F32)], axis=0)
    tab = jnp.concatenate([tab, jnp.zeros((5, T), F32)], axis=0).reshape(8, R, tt).transpose(1, 0, 2)
    eye = jnp.eye(tt, dtype=BF16)
    return pl.pallas_call(
        _combine_kernel,
        out_shape=jax.ShapeDtypeStruct((T, D), F32),
        grid_spec=pltpu.PrefetchScalarGridSpec(
            num_scalar_prefetch=3,
            grid=(nitem,),
            in_specs=[pl.BlockSpec((rb, D), lambda j, t, b, f: (b[j], 0)),
                      pl.BlockSpec((tt, D), lambda j, t, b, f: (t[j], 0)),
                      pl.BlockSpec((1, 6, D), lambda j, t, b, f: (b0 + t[j] // nper, 0, 0)),
                      pl.BlockSpec((1, 8, tt), lambda j, t, b, f: (t[j], 0, 0)),
                      pl.BlockSpec((tt, tt), lambda j, t, b, f: (0, 0))],
            out_specs=pl.BlockSpec((tt, D), lambda j, t, b, f: (t[j], 0))),
        compiler_params=_params("arbitrary"),
        name="moe_combine",
    )(ti, blk, flag, contrib, x, mod_l, tab, eye)


def _layer(x, B, L, b0, mod_l, p, kf, fc, tabs):
    hy_w = p['hy_bias'].shape[1]
    att_w = N_HEADS * HEAD_DIM
    kv_w = N_KV_HEADS * HEAD_DIM
    planes = 2 if B % 2 == 0 else 1
    z, ga, gb, q, k, v, gs = _inproj(x, mod_l, b0, L, p['norm1_g'][None], p['w_in_bf'], p['hy_conv_w'], p['hy_conv_b'],
                                     tabs['cos'], tabs['sin'], tabs['qg'], tabs['kg'], tabs['blk'], hy_w, att_w, kv_w)
    shape5 = (B // planes, planes, L // LANES, hy_w, LANES)
    z = z.reshape(shape5)
    gates = (ga.reshape(shape5), gb.reshape(shape5))
    for o in range(HY_ORDER):
        z = _longconv(z, gates[o], kf, o, p['bias_t'], fc)
    z = z.reshape(B * L // LANES, hy_w, LANES)
    att = _attention(q, k, v, p['sink_logit'], B, L)
    x, h2, aff_t = _merge(x, z, att, gs, mod_l, b0, L, p['w_hy_bf'], p['w_at_x'], p['w_o_bf'], p['norm2_g'][None],
                          p['rw_hi'], p['rw_lo'])
    T = x.shape[0]
    cap = CAPACITY_FACTOR * T // N_EXPERTS
    idx, gate, dst, cum, cnt = _topk(aff_t, cap)
    contrib = _experts(h2, idx, dst, p['exp_w_gate'], p['exp_w_up'], p['exp_w_down'], p['layer'],
                       gate.reshape(N_EXPERTS, cap, 1), cap)
    return _combine(x, mod_l, b0, L, contrib, cum, cnt)


def _rope_tables(L):
    inv = 1.0 / (ROPE_THETA ** (jnp.arange(0, HEAD_DIM, 2, dtype=F32) / HEAD_DIM))
    ang = jnp.arange(L, dtype=F32)[:, None] * inv[None, :]
    cos, sin = jnp.cos(ang), jnp.sin(ang)
    cos_t = jnp.tile(cos, (1, LANES // (HEAD_DIM // 2)))
    sin_t = jnp.tile(jnp.concatenate([-sin, sin], axis=1), (1, LANES // HEAD_DIM))
    return cos_t, sin_t


def _filter_spectrum_for(L, p, fc):
    hy_w = p['hy_bias'].shape[1]
    k3, ssq = _filter_rows(L, p)
    return _filter_spectrum(k3, ssq, fc, hy_w)


def kernel(x_prompt, x_sample, c_prompt, c_sample, ada_w, ada_b, norm1_g, norm2_g, w_in, hy_conv_w, hy_conv_b,
           flt_w1, flt_b1, flt_w2, flt_b2, flt_w3, flt_b3, flt_freq, flt_w_out, hy_bias, q_norm_g, k_norm_g,
           sink_logit, w_hy_proj, w_at_proj, w_o, router_w, exp_w_gate, exp_w_up, exp_w_down):
    depth = ada_w.shape[0]
    Bp, Lp, D = x_prompt.shape
    Bs, Ls, _ = x_sample.shape
    nb = Bp + Bs
    c_all = jnp.concatenate([c_prompt, c_sample, jnp.zeros(((-nb) % 8, D), F32)], axis=0)
    mod = _adaln(c_all, ada_w, ada_b).reshape(depth, c_all.shape[0], 6, D)
    hy_w = hy_bias.shape[2]
    att_w = N_HEADS * HEAD_DIM
    groups = [dict(B=Bp, L=Lp, b0=0, x=x_prompt.reshape(Bp * Lp, D)),
              dict(B=Bs, L=Ls, b0=Bp, x=x_sample.reshape(Bs * Ls, D))]
    blk = jnp.asarray(np.kron(np.eye(LANES // HEAD_DIM), np.ones((HEAD_DIM, HEAD_DIM))), BF16)
    for g in groups:
        g['fc'] = _fft_consts(g['L'], 2 if g['B'] % 2 == 0 else 1)
        cos_t, sin_t = _rope_tables(g['L'])
        g['tabs'] = dict(cos=cos_t, sin=sin_t, blk=blk)
    heads_per_kv = N_HEADS // N_KV_HEADS
    for l in range(depth):
        w_at_x = jnp.zeros((N_HEADS, LANES, D), BF16)
        w_at_h = w_at_proj[l].astype(BF16).reshape(N_HEADS, HEAD_DIM, D)
        for h in range(N_HEADS):
            off = (h // heads_per_kv) * HEAD_DIM
            w_at_x = w_at_x.at[h, off:off + HEAD_DIM].set(w_at_h[h])
        rw_hi, rw_lo = _split(router_w[l].T)
        p = dict(norm1_g=norm1_g[l], norm2_g=norm2_g[l], w_in_bf=w_in[l].astype(BF16),
                 hy_conv_w=hy_conv_w[l], hy_conv_b=hy_conv_b[l],
                 flt_w1=flt_w1[l], flt_b1=flt_b1[l], flt_w2=flt_w2[l], flt_b2=flt_b2[l], flt_w3=flt_w3[l],
                 flt_b3=flt_b3[l], flt_freq=flt_freq[l], flt_w_out=flt_w_out[l], hy_bias=hy_bias[l],
                 bias_t=jnp.broadcast_to(hy_bias[l].reshape(-1, 1, CONV_CH), (HY_ORDER * hy_w // CONV_CH, 8, CONV_CH)),
                 sink_logit=sink_logit[l], w_hy_bf=w_hy_proj[l].astype(BF16),
                 w_at_x=w_at_x.reshape(N_HEADS * LANES, D), w_o_bf=w_o[l].astype(BF16),
                 rw_hi=rw_hi, rw_lo=rw_lo,
                 layer=l, exp_w_gate=exp_w_gate, exp_w_up=exp_w_up, exp_w_down=exp_w_down)
        qg = jnp.tile(q_norm_g[l], LANES // HEAD_DIM)[None]
        kg = jnp.tile(k_norm_g[l], LANES // HEAD_DIM)[None]
        for g in groups:
            kf = _filter_spectrum_for(g['L'], p, g['fc'])
            tabs = dict(g['tabs'], qg=qg, kg=kg)
            g['x'] = _layer(g['x'], g['B'], g['L'], g['b0'], mod[l], p, kf, g['fc'], tabs)
    return (groups[0]['x'].reshape(Bp, Lp, D), groups[1]['x'].reshape(Bs, Ls, D))
```

```python
import functools
import math

import numpy as np
import jax
import jax.numpy as jnp
from jax import lax
from jax.experimental import pallas as pl
from jax.experimental.pallas import tpu as pltpu

F32 = jnp.float32
BF16 = jnp.bfloat16
I32 = jnp.int32

EPS = 1e-6
HEAD_DIM = 64
N_HEADS = 8
N_KV_HEADS = 2
WINDOW = 128
ROPE_THETA = 10000.0
HY_ORDER = 2
FILTER_EMB = 33
DECAY_TARGET = 0.01
FAST_DECAY_PCT = 0.3
SLOW_DECAY_PCT = 1.5
N_EXPERTS = 16
CAPACITY_FACTOR = 2
LANES = 128
FFT_N2 = LANES
VMEM_LIMIT = 52 << 20


def _params(*sem):
    return pltpu.CompilerParams(dimension_semantics=sem, vmem_limit_bytes=VMEM_LIMIT)


def _dot(a, b):
    return jnp.dot(a, b, preferred_element_type=F32)


def _dot_nt(a, b):
    return lax.dot_general(a, b, (((1,), (1,)), ((), ())), preferred_element_type=F32)


def _split(x):
    hi = x.astype(BF16)
    lo = (x - hi.astype(F32)).astype(BF16)
    return hi, lo


def _dot3(a_hi, a_lo, b_hi, b_lo):
    return _dot(a_hi, b_hi) + _dot(a_lo, b_hi) + _dot(a_hi, b_lo)


def _ada_kernel(c_ref, w_ref, b_ref, o_ref):
    c = c_ref[...]
    s = c * jax.nn.sigmoid(c)
    o_ref[0] = _dot(s.astype(BF16), w_ref[0].astype(BF16)) + b_ref[0]


def _adaln(c_all, ada_w, ada_b):
    depth, d, n = ada_w.shape
    tn = n // 4
    return pl.pallas_call(
        _ada_kernel,
        out_shape=jax.ShapeDtypeStruct((depth, c_all.shape[0], n), F32),
        grid=(depth, n // tn),
        in_specs=[pl.BlockSpec(c_all.shape, lambda l, j: (0, 0)),
                  pl.BlockSpec((1, d, tn), lambda l, j: (l, 0, j)),
                  pl.BlockSpec((1, 1, tn), lambda l, j: (l, 0, j))],
        out_specs=pl.BlockSpec((1, c_all.shape[0], tn), lambda l, j: (l, 0, j)),
        compiler_params=_params("arbitrary", "arbitrary"),
        name="adaln",
    )(c_all, ada_w, ada_b.reshape(depth, 1, n))


def _head_norm_rope(x, gain, cos, sin, blk_hi):
    sq = x * x
    hi, lo = _split(sq)
    ssum = _dot(hi, blk_hi) + _dot(lo, blk_hi)
    xn = x * lax.rsqrt(ssum * (1.0 / HEAD_DIM) + EPS) * gain
    lane = lax.broadcasted_iota(I32, x.shape, 1)
    first_half = (lane % HEAD_DIM) < (HEAD_DIM // 2)
    rot = jnp.where(first_half, pltpu.roll(xn, LANES - HEAD_DIM // 2, 1), pltpu.roll(xn, HEAD_DIM // 2, 1))
    return xn * cos + rot * sin


def _inproj_kernel(x_ref, xp_ref, xn_ref, mod_ref, g_ref, w_ref, cw_ref, cb_ref, cos_ref, sin_ref, qg_ref, kg_ref, blk_ref,
                   z_ref, ga_ref, gb_ref, q_ref, k_ref, v_ref, gs_ref, *, hy_w, att_w, kv_w, nper):
    i = pl.program_id(0)
    tm = x_ref.shape[0]
    halo = xp_ref.shape[0]
    x = jnp.concatenate([xp_ref[...], x_ref[...], xn_ref[...]], axis=0)
    mod = mod_ref[0]
    ms = jnp.mean(x * x, axis=-1, keepdims=True)
    h = (x * lax.rsqrt(ms + EPS) * g_ref[...]) * (1.0 + mod[1:2]) + mod[0:1]
    hb_all = h.astype(BF16)
    hb = hb_all[halo:halo + tm]
    s0 = 3 * hy_w
    s1 = s0 + att_w
    s2 = s1 + kv_w
    s3 = s2 + kv_w
    hy = _dot(hb_all, w_ref[:, 0:s0])
    n_all = tm + 2 * halo
    row = lax.broadcasted_iota(I32, (tm, 1), 0)
    first = (i % nper) == 0
    last = (i % nper) == nper - 1
    up = jnp.where((row == 0) & first, 0.0, pltpu.roll(hy, 1, 0)[halo:halo + tm])
    dn = jnp.where((row == tm - 1) & last, 0.0, pltpu.roll(hy, n_all - 1, 0)[halo:halo + tm])
    y = up * cw_ref[0:1, :] + hy[halo:halo + tm] * cw_ref[1:2, :] + dn * cw_ref[2:3, :] + cb_ref[...]
    for r in range(tm // LANES):
        blk_y = y[r * LANES:(r + 1) * LANES, :]
        z_ref[r] = blk_y[:, 0:hy_w].T
        ga_ref[r] = blk_y[:, hy_w:2 * hy_w].T
        gb_ref[r] = blk_y[:, 2 * hy_w:3 * hy_w].T
    q = _dot(hb, w_ref[:, s0:s1])
    k = _dot(hb, w_ref[:, s1:s2])
    v_ref[...] = _dot(hb, w_ref[:, s2:s3]).astype(BF16)
    gs_ref[...] = jax.nn.sigmoid(_dot(hb, w_ref[:, s3:]))
    cos = cos_ref[...]
    sin = sin_ref[...]
    blk = blk_ref[...]
    k_ref[...] = _head_norm_rope(k, kg_ref[...], cos, sin, blk).astype(BF16)
    lane = lax.broadcasted_iota(I32, cos.shape, 1)
    low = lane < HEAD_DIM
    heads_per_kv = N_HEADS // N_KV_HEADS
    for ci in range(att_w // LANES):
        qr = _head_norm_rope(q[:, ci * LANES:(ci + 1) * LANES], qg_ref[...], cos, sin, blk) * (HEAD_DIM ** -0.5)
        qs = pltpu.roll(qr, HEAD_DIM, 1)
        kvh = (2 * ci) // heads_per_kv
        if kvh == 0:
            h0 = jnp.where(low, qr, 0.0)
            h1 = jnp.where(low, qs, 0.0)
        else:
            h0 = jnp.where(low, 0.0, qs)
            h1 = jnp.where(low, 0.0, qr)
        q_ref[:, (2 * ci) * LANES:(2 * ci + 1) * LANES] = h0.astype(BF16)
        q_ref[:, (2 * ci + 1) * LANES:(2 * ci + 2) * LANES] = h1.astype(BF16)


def _inproj(x, mod_l, b0, L, norm_g, w_in_bf, conv_w, conv_b, cos_t, sin_t, qg, kg, blk, hy_w, att_w, kv_w):
    T, D = x.shape
    tm = 512
    halo = 8
    nper = L // tm
    r8 = tm // halo
    nb8 = T // halo
    in_w = w_in_bf.shape[1]
    kern = functools.partial(_inproj_kernel, hy_w=hy_w, att_w=att_w, kv_w=kv_w, nper=nper)
    stream = jax.ShapeDtypeStruct((T // LANES, hy_w, LANES), F32)
    sspec = pl.BlockSpec((tm // LANES, hy_w, LANES), lambda i: (i, 0, 0))
    full = lambda i: (0, 0)
    return pl.pallas_call(
        kern,
        out_shape=(stream, stream, stream,
                   jax.ShapeDtypeStruct((T, N_HEADS * LANES), BF16),
                   jax.ShapeDtypeStruct((T, kv_w), BF16),
                   jax.ShapeDtypeStruct((T, kv_w), BF16),
                   jax.ShapeDtypeStruct((T, in_w - 3 * hy_w - att_w - 2 * kv_w), F32)),
        grid=(T // tm,),
        in_specs=[pl.BlockSpec((tm, D), lambda i: (i, 0)),
                  pl.BlockSpec((halo, D), lambda i: (jnp.maximum(i * r8 - 1, 0), 0)),
                  pl.BlockSpec((halo, D), lambda i: (jnp.minimum((i + 1) * r8, nb8 - 1), 0)),
                  pl.BlockSpec((1, 6, D), lambda i: (b0 + i // nper, 0, 0)),
                  pl.BlockSpec((1, D), full),
                  pl.BlockSpec((D, in_w), full),
                  pl.BlockSpec((3, 3 * hy_w), full),
                  pl.BlockSpec((1, 3 * hy_w), full),
                  pl.BlockSpec((tm, LANES), lambda i: (i % nper, 0)),
                  pl.BlockSpec((tm, LANES), lambda i: (i % nper, 0)),
                  pl.BlockSpec((1, LANES), full),
                  pl.BlockSpec((1, LANES), full),
                  pl.BlockSpec((LANES, LANES), full)],
        out_specs=(sspec, sspec, sspec,
                   pl.BlockSpec((tm, N_HEADS * LANES), lambda i: (i, 0)),
                   pl.BlockSpec((tm, kv_w), lambda i: (i, 0)),
                   pl.BlockSpec((tm, kv_w), lambda i: (i, 0)),
                   pl.BlockSpec((tm, in_w - 3 * hy_w - att_w - 2 * kv_w), lambda i: (i, 0))),
        compiler_params=_params("arbitrary"),
        name="inproj",
    )(x, x, x, mod_l, norm_g, w_in_bf, conv_w, conv_b.reshape(1, -1), cos_t, sin_t, qg, kg, blk)


def _filter_kernel(z_ref, tl_ref, w1_ref, b1_ref, w2_ref, b2_ref, w3_ref, b3_ref, fr_ref, woa_ref, wob_ref, dl_ref,
                   k_ref, ssq_ref, *, L):
    i = pl.program_id(0)
    z = z_ref[...]
    tr = z.shape[0]
    half = tr // 2
    fr = fr_ref[...]

    def lin(a, w_ref_, b_ref_):
        ah, al = _split(a)
        wh, wl = _split(w_ref_[...])
        return _dot3(ah, al, wh, wl) + b_ref_[...]

    zz = jnp.concatenate([z[:half], z[half:]], axis=1)
    h = jnp.sin(fr * lin(zz, w1_ref, b1_ref))
    h = jnp.sin(fr * lin(h, w2_ref, b2_ref))
    h = jnp.sin(fr * lin(h, w3_ref, b3_ref))
    hh, hl = _split(h)
    wts = (_split(woa_ref[0]), _split(wob_ref[0]))

    @pl.when(i == 0)
    def _():
        ssq_ref[...] = jnp.zeros_like(ssq_ref)

    lane = lax.broadcasted_iota(I32, (1, LANES), 1)
    per_half = half // LANES
    for r in range(tr // LANES):
        wh, wl = wts[r // per_half]
        rows = slice((r % per_half) * LANES, (r % per_half + 1) * LANES)
        kt = _dot_nt(wh, hh[rows]) + _dot_nt(wl, hh[rows]) + _dot_nt(wh, hl[rows])
        kt = kt * jnp.exp(-dl_ref[...] * tl_ref[r])
        kt = jnp.where((i * tr + r * LANES + lane) == L, 0.0, kt)
        k_ref[r] = kt
        ssq_ref[...] += kt * kt


def _filter_rows(L, p):
    hid = p['flt_w2'].shape[0]
    hy_w = p['hy_bias'].shape[1]
    cw = HY_ORDER * hy_w
    tr = 512
    N = 2 * L
    n1 = N // LANES
    t = jnp.linspace(0.0, 1.0, L, dtype=F32)[:, None]
    bands = (FILTER_EMB - 1) // 2
    f = jnp.linspace(1e-4, bands - 1, bands, dtype=F32)[None, :]
    w = (2.0 * math.pi / L) * jnp.arange(L, dtype=F32)[:, None]
    zf = jnp.concatenate([t, jnp.cos(f * w), -jnp.sin(f * w)], axis=-1)
    z = jnp.concatenate([zf, zf[L - 1:L], zf[:0:-1]], axis=0)
    tl = z[:, 0].reshape(n1, 1, LANES)
    z = jnp.pad(z, ((0, 0), (0, LANES - FILTER_EMB)))
    eye2 = jnp.eye(2, dtype=F32)
    w1 = jnp.concatenate([jnp.pad(p['flt_w1'], ((0, LANES - FILTER_EMB), (0, hid))),
                          jnp.pad(p['flt_w1'], ((0, LANES - FILTER_EMB), (hid, 0)))], axis=0)
    w2 = jnp.kron(eye2, p['flt_w2'])
    w3 = jnp.kron(eye2, p['flt_w3'])
    two = lambda v: jnp.tile(v, 2)[None]
    wo_t = p['flt_w_out'].reshape(hid, 2, cw).transpose(1, 2, 0)
    woa = jnp.pad(wo_t, ((0, 0), (0, 0), (0, hid)))
    wob = jnp.pad(wo_t, ((0, 0), (0, 0), (hid, 0)))
    max_decay = math.log(DECAY_TARGET) / FAST_DECAY_PCT
    min_decay = math.log(DECAY_TARGET) / SLOW_DECAY_PCT
    deltas = jnp.abs(jnp.linspace(min_decay, max_decay, hy_w, dtype=F32))
    dl = jnp.tile(deltas, HY_ORDER)[:, None]
    nblk = N // tr
    halfb = nblk // 2
    full = lambda i: (0, 0)
    h2 = 2 * hid
    return pl.pallas_call(
        functools.partial(_filter_kernel, L=L),
        out_shape=(jax.ShapeDtypeStruct((n1, cw, LANES), F32), jax.ShapeDtypeStruct((cw, LANES), F32)),
        grid=(nblk,),
        in_specs=[pl.BlockSpec((tr, LANES), lambda i: (i, 0)),
                  pl.BlockSpec((tr // LANES, 1, LANES), lambda i: (i, 0, 0)),
                  pl.BlockSpec((2 * LANES, h2), full), pl.BlockSpec((1, h2), full),
                  pl.BlockSpec((h2, h2), full), pl.BlockSpec((1, h2), full),
                  pl.BlockSpec((h2, h2), full), pl.BlockSpec((1, h2), full),
                  pl.BlockSpec((1, h2), full),
                  pl.BlockSpec((1, cw, h2), lambda i: (i // halfb, 0, 0)),
                  pl.BlockSpec((1, cw, h2), lambda i: (i // halfb, 0, 0)),
                  pl.BlockSpec((cw, 1), full)],
        out_specs=(pl.BlockSpec((tr // LANES, cw, LANES), lambda i: (i, 0, 0)), pl.BlockSpec((cw, LANES), full)),
        compiler_params=_params("arbitrary"),
        name="hyena_filter",
    )(z, tl, w1, two(p['flt_b1']), w2, two(p['flt_b2']), w3, two(p['flt_b3']), two(p['flt_freq']), woa, wob, dl)


def _fft_consts(L, planes):
    N = 2 * L
    n1 = N // FFT_N2
    n1h = n1 // 2
    k1 = np.arange(n1)[:, None]
    th = 2.0 * np.pi * k1 * np.arange(n1)[None, :] / n1
    c, s = np.cos(th), np.sin(th)
    ch, sh = c[:, :n1h], s[:, :n1h]
    if planes == 1:
        m1 = np.concatenate([ch, -sh], axis=0)
        m3 = np.concatenate([ch.T, -sh.T], axis=1) / N
    else:
        m1 = np.block([[ch, sh], [-sh, ch]])
        m3 = np.block([[ch.T, -sh.T], [sh.T, ch.T]]) / N
    m1_full = np.concatenate([c, -s], axis=0)
    ph = 2.0 * np.pi * k1 * np.arange(FFT_N2)[None, :] / N
    twr, twi = np.cos(ph), -np.sin(ph)
    a = 2.0 * np.pi * np.arange(FFT_N2)[:, None] * np.arange(FFT_N2)[None, :] / FFT_N2
    ca, sa = np.cos(a), np.sin(a)
    rt = np.block([[ca, -sa], [sa, ca]])
    rit = np.block([[ca, sa], [-sa, ca]])
    bf = lambda m: jnp.asarray(np.asarray(m, np.float32).astype(BF16))
    return dict(n1=n1, n1h=n1h, m1=bf(m1), m3=bf(m3), m1_full=bf(m1_full),
                twr=jnp.asarray(twr, F32), twi=jnp.asarray(twi, F32), rt=bf(rt), rit=bf(rit))


CONV_CH = 16


def _load_channels(ref, lead):
    return jnp.concatenate([ref[lead + (slice(None), c, slice(None))] for c in range(CONV_CH)], axis=1)


def _fwd_lane_dft(ar, ai, twr, twi, rt):
    pieces = []
    for c in range(CONV_CH):
        r = ar[:, c * LANES:(c + 1) * LANES]
        i = ai[:, c * LANES:(c + 1) * LANES]
        pieces.append(jnp.concatenate([r * twr - i * twi, r * twi + i * twr], axis=1))
    return _dot(jnp.concatenate(pieces, axis=0).astype(BF16), rt)


def _spectrum_kernel(k_ref, ssq_ref, m1_ref, twr_ref, twi_ref, rt_ref, o_ref):
    n1 = twr_ref.shape[0]
    a = _dot(m1_ref[...], _load_channels(k_ref, ()).astype(BF16))
    x = _fwd_lane_dft(a[:n1], a[n1:], twr_ref[...], twi_ref[...], rt_ref[...])
    scale = lax.rsqrt(jnp.sum(ssq_ref[...], axis=1, keepdims=True) + EPS)
    for c in range(CONV_CH):
        o_ref[0, :, c * 2 * LANES:(c + 1) * 2 * LANES] = x[c * n1:(c + 1) * n1] * scale[c:c + 1, :]


def _filter_spectrum(k3, ssq, fc, hy_w):
    n1 = fc['n1']
    ct = CONV_CH
    ntile = hy_w // ct
    full = lambda o, j: (0, 0)
    return pl.pallas_call(
        _spectrum_kernel,
        out_shape=jax.ShapeDtypeStruct((HY_ORDER, n1, hy_w * 2 * LANES), F32),
        grid=(HY_ORDER, ntile),
        in_specs=[pl.BlockSpec((n1, ct, LANES), lambda o, j: (0, o * ntile + j, 0)),
                  pl.BlockSpec((ct, LANES), lambda o, j: (o * ntile + j, 0)),
                  pl.BlockSpec((2 * n1, n1), full),
                  pl.BlockSpec((n1, LANES), full), pl.BlockSpec((n1, LANES), full),
                  pl.BlockSpec((2 * LANES, 2 * LANES), full)],
        out_specs=pl.BlockSpec((1, n1, ct * 2 * LANES), lambda o, j: (o, 0, j)),
        compiler_params=_params("arbitrary", "arbitrary"),
        name="filter_spectrum",
    )(k3, ssq, fc['m1_full'], fc['twr'], fc['twi'], fc['rt'])


def _longconv_kernel(z_ref, g_ref, kf_ref, b_ref, m1_ref, twr_ref, twi_ref, rt_ref, rit_ref, m3_ref, o_ref):
    ct = CONV_CH
    n1 = twr_ref.shape[0]
    planes, n1h = z_ref.shape[1], z_ref.shape[2]
    z = jnp.concatenate([_load_channels(z_ref, (0, p)) for p in range(planes)], axis=0)
    a = _dot(m1_ref[...], z.astype(BF16))
    twr = twr_ref[...]
    twi = twi_ref[...]
    x = _fwd_lane_dft(a[:n1], a[n1:], twr, twi, rt_ref[...])
    ys = []
    for c in range(ct):
        xc = x[c * n1:(c + 1) * n1]
        kc = kf_ref[0, :, c * 2 * LANES:(c + 1) * 2 * LANES]
        xr, xi = xc[:, :LANES], xc[:, LANES:]
        kr, ki = kc[:, :LANES], kc[:, LANES:]
        ys.append(jnp.concatenate([xr * kr - xi * ki, xr * ki + xi * kr], axis=1))
    bp = _dot(jnp.concatenate(ys, axis=0).astype(BF16), rit_ref[...])
    brs, bis = [], []
    for c in range(ct):
        bc = bp[c * n1:(c + 1) * n1]
        r, i = bc[:, :LANES], bc[:, LANES:]
        brs.append(r * twr + i * twi)
        bis.append(i * twr - r * twi)
    bst = jnp.concatenate([jnp.concatenate(brs, axis=1), jnp.concatenate(bis, axis=1)], axis=0)
    yt = _dot(m3_ref[...], bst.astype(BF16))
    for p in range(planes):
        rows = slice(p * n1h, (p + 1) * n1h)
        for c in range(ct):
            lanes = slice(c * LANES, (c + 1) * LANES)
            o_ref[0, p, :, c, :] = g_ref[0, p, :, c, :] * (yt[rows, lanes] + b_ref[0, 0:1, c:c + 1] * z[rows, lanes])


def _longconv(z5, g5, kf, order, bias_t, fc):
    npair, planes, n1h, ch, _ = z5.shape
    n1 = fc['n1']
    ct = CONV_CH
    ntile = ch // ct
    full = lambda p, j: (0, 0)
    m1r, m1c = 2 * n1, planes * n1h
    blk = pl.BlockSpec((1, planes, n1h, ct, LANES), lambda p, j: (p, 0, 0, j, 0))
    return pl.pallas_call(
        _longconv_kernel,
        out_shape=jax.ShapeDtypeStruct(z5.shape, F32),
        grid=(npair, ntile),
        in_specs=[blk, blk,
                  pl.BlockSpec((1, n1, ct * 2 * LANES), lambda p, j: (order, 0, j)),
                  pl.BlockSpec((1, 8, ct), lambda p, j: (order * ntile + j, 0, 0)),
                  pl.BlockSpec((m1r, m1c), full),
                  pl.BlockSpec((n1, LANES), full), pl.BlockSpec((n1, LANES), full),
                  pl.BlockSpec((2 * LANES, 2 * LANES), full), pl.BlockSpec((2 * LANES, 2 * LANES), full),
                  pl.BlockSpec((m1c, m1r), full)],
        out_specs=blk,
        compiler_params=_params("arbitrary", "arbitrary"),
        name="hyena_longconv",
    )(z5, g5, kf, bias_t, fc['m1'], fc['twr'], fc['twi'], fc['rt'], fc['rit'], fc['m3'])


def _attn_kernel(sink_ref, q_ref, kp_ref, km_ref, kn_ref, vp_ref, vm_ref, vn_ref, o_ref, *, L, tq):
    i = pl.program_id(1)
    t0 = i * tq
    kk = jnp.concatenate([kp_ref[...], km_ref[...], kn_ref[...]], axis=0)
    vv = jnp.concatenate([vp_ref[...], vm_ref[...], vn_ref[...]], axis=0)
    sub = WINDOW
    tk = 3 * WINDOW
    di = lax.broadcasted_iota(I32, (sub, tk), 0)
    dj = lax.broadcasted_iota(I32, (sub, tk), 1)
    band = jnp.abs(dj - WINDOW - di) <= WINDOW
    hid = lax.broadcasted_iota(I32, (N_HEADS, 1, 1), 0)
    sink = jnp.zeros((N_HEADS, 1, 1), F32)
    for h in range(N_HEADS):
        sink = jnp.where(hid == h, sink_ref[h], sink)
    for a in range(tq // sub):
        kpos = t0 + (a - 1) * sub + dj
        valid = band & (kpos >= 0) & (kpos < L)
        rows = slice(a * sub, (a + 1) * sub)
        qa = jnp.concatenate([q_ref[rows, h * LANES:(h + 1) * LANES] for h in range(N_HEADS)], axis=0)
        s = _dot_nt(qa, kk[a * sub:a * sub + tk]).reshape(N_HEADS, sub, tk)
        s = jnp.where(valid, s, -1e30)
        m = jnp.maximum(jnp.max(s, axis=-1, keepdims=True), sink)
        p = jnp.exp(s - m)
        denom = jnp.sum(p, axis=-1, keepdims=True) + jnp.exp(sink - m)
        o = _dot((p / denom).astype(BF16).reshape(N_HEADS * sub, tk), vv[a * sub:a * sub + tk]).astype(BF16)
        for h in range(N_HEADS):
            o_ref[rows, h * LANES:(h + 1) * LANES] = o[h * sub:(h + 1) * sub]


def _attention(q, k, v, sink, B, L):
    T = q.shape[0]
    tq = 512
    nq = L // tq
    r = tq // WINDOW
    nb = L // WINDOW
    kv_w = k.shape[1]
    kp = pl.BlockSpec((WINDOW, kv_w), lambda b, i: (b * nb + jnp.maximum(i * r - 1, 0), 0))
    km = pl.BlockSpec((tq, kv_w), lambda b, i: (b * nq + i, 0))
    kn = pl.BlockSpec((WINDOW, kv_w), lambda b, i: (b * nb + jnp.minimum((i + 1) * r, nb - 1), 0))
    qs = pl.BlockSpec((tq, N_HEADS * LANES), lambda b, i: (b * nq + i, 0))
    return pl.pallas_call(
        functools.partial(_attn_kernel, L=L, tq=tq),
        out_shape=jax.ShapeDtypeStruct((T, N_HEADS * LANES), BF16),
        grid=(B, nq),
        in_specs=[pl.BlockSpec(memory_space=pltpu.SMEM), qs, kp, km, kn, kp, km, kn],
        out_specs=qs,
        compiler_params=_params("arbitrary", "arbitrary"),
        name="band_attention",
    )(sink, q, k, k, k, v, v, v)


def _merge_kernel(x_ref, z_ref, a_ref, gs_ref, mod_ref, why_ref, wat_ref, wo_ref, g2_ref, rwh_ref, rwl_ref,
                  xo_ref, h2_ref, aff_ref):
    d = x_ref.shape[1]
    mod = mod_ref[0]
    z = jnp.concatenate([z_ref[r].T for r in range(z_ref.shape[0])], axis=0)
    y_hy = _dot(z.astype(BF16), why_ref[...])
    y_at = _dot(a_ref[...], wat_ref[...])
    gs = gs_ref[...]
    merged = gs[:, :d] * y_hy + gs[:, d:] * y_at
    x = x_ref[...] + mod[2:3] * _dot(merged.astype(BF16), wo_ref[...])
    xo_ref[...] = x
    ms = jnp.mean(x * x, axis=-1, keepdims=True)
    h = (x * lax.rsqrt(ms + EPS) * g2_ref[...]) * (1.0 + mod[4:5]) + mod[3:4]
    h2_ref[...] = h
    hh, hl = _split(h)
    rh, rl = rwh_ref[...], rwl_ref[...]
    logits = _dot_nt(rh, hh) + _dot_nt(rl, hh) + _dot_nt(rh, hl)
    m = jnp.max(logits, axis=0, keepdims=True)
    e = jnp.exp(logits - m)
    aff_ref[...] = e / jnp.sum(e, axis=0, keepdims=True)


def _merge(x, z, att, gs, mod_l, b0, L, w_hy, w_at_x, w_o, norm2_g, rw_hi, rw_lo):
    T, D = x.shape
    tm = 512
    nper = L // tm
    E = rw_hi.shape[0]
    full = lambda i: (0, 0)
    row = lambda w: pl.BlockSpec((tm, w), lambda i: (i, 0))
    return pl.pallas_call(
        _merge_kernel,
        out_shape=(jax.ShapeDtypeStruct((T, D), F32), jax.ShapeDtypeStruct((T, D), F32),
                   jax.ShapeDtypeStruct((E, T), F32)),
        grid=(T // tm,),
        in_specs=[row(D), pl.BlockSpec((tm // LANES,) + z.shape[1:], lambda i: (i, 0, 0)), row(att.shape[1]), row(gs.shape[1]),
                  pl.BlockSpec((1, 6, D), lambda i: (b0 + i // nper, 0, 0)),
                  pl.BlockSpec(w_hy.shape, full), pl.BlockSpec(w_at_x.shape, full), pl.BlockSpec(w_o.shape, full),
                  pl.BlockSpec((1, D), full), pl.BlockSpec((E, D), full), pl.BlockSpec((E, D), full)],
        out_specs=(row(D), row(D), pl.BlockSpec((E, tm), lambda i: (0, i))),
        compiler_params=_params("arbitrary"),
        name="merge_router",
    )(x, z, att, gs, mod_l, w_hy, w_at_x, w_o, norm2_g, rw_hi, rw_lo)


def _cumsum_rows(mb, u, ones, ls):
    rc = _dot(mb, u)
    totb = _dot(mb, ones)
    offs = _dot(ls, totb.astype(BF16))
    return rc, totb, offs


def _topk_kernel(aff_ref, u_ref, ut_ref, ones_ref, ls_ref, idx_ref, gate_ref, dst_ref, cum_ref, cnt_ref,
                 thr_ref, sel_ref, rank_ref, *, cap):
    E, R, _ = aff_ref.shape
    bits = pltpu.bitcast(aff_ref[...], I32)

    def search(i, cur):
        cand = cur | jnp.left_shift(jnp.int32(1), 30 - i)
        ge = jnp.where(bits >= cand, 1.0, 0.0)
        c = jnp.sum(jnp.sum(ge, axis=2, keepdims=True), axis=1, keepdims=True)
        return jnp.where(c >= cap, cand, cur)

    thr = lax.fori_loop(0, 31, search, jnp.zeros((E, 1, 1), I32))
    thr_ref[...] = jnp.broadcast_to(thr, thr_ref.shape)
    u = u_ref[...]
    ut = ut_ref[...]
    ones = ones_ref[...]
    ls = ls_ref[...]

    def select(e, carry):
        cnt_run, cum = carry
        b = pltpu.bitcast(aff_ref[e], I32)
        t = thr_ref[e][0:1, :]
        gt = b > t
        eq = b == t
        ngt = jnp.sum(jnp.sum(jnp.where(gt, 1.0, 0.0), axis=1, keepdims=True), axis=0, keepdims=True)
        need = cap - ngt
        erc, _, eoffs = _cumsum_rows(jnp.where(eq, 1.0, 0.0).astype(BF16), u, ones, ls)
        sel = jnp.where(gt | (eq & ((erc + eoffs) <= need)), 1.0, 0.0)
        rc, _, offs = _cumsum_rows(sel.astype(BF16), u, ones, ls)
        sel_ref[e] = sel
        rank_ref[e] = cnt_run
        return cnt_run + sel, cum + (rc + offs - sel)

    zero = jnp.zeros((R, LANES), F32)
    cnt, cum = lax.fori_loop(0, E, select, (zero, zero))
    cnt_ref[...] = cnt.astype(I32)
    cum_ref[...] = cum.astype(I32)
    s_row = lax.broadcasted_iota(I32, (1, cap), 1).astype(F32)
    r_col = lax.broadcasted_iota(I32, (R, 1), 0).astype(F32)
    c_col = lax.broadcasted_iota(I32, (LANES, 1), 0).astype(F32)

    def pick(vt_pieces, ohb, hit):
        acc = _dot(vt_pieces[0], ohb)
        for piece in vt_pieces[1:]:
            acc = acc + _dot(piece, ohb)
        return jnp.sum(jnp.where(hit, acc, 0.0), axis=0, keepdims=True)

    def compact(e, carry):
        sel = sel_ref[e]
        _, totb, offs = _cumsum_rows(sel.astype(BF16), u, ones, ls)
        offs_col = offs[:, 0:1]
        end_col = offs_col + totb[:, 0:1]
        oht = (offs_col <= s_row) & (s_row < end_col)
        row_s = jnp.sum(jnp.where(oht, r_col, 0.0), axis=0, keepdims=True)
        offs_s = jnp.sum(jnp.where(oht, offs_col, 0.0), axis=0, keepdims=True)
        ohb = jnp.where(oht, 1.0, 0.0).astype(BF16)
        rct = _dot(ut, sel.T.astype(BF16))
        rcs = _dot(rct.astype(BF16), ohb)
        col_s = jnp.sum(jnp.where(rcs <= (s_row - offs_s), 1.0, 0.0), axis=0, keepdims=True)
        hit = c_col == col_s
        idx_ref[pl.ds(e, 1), :] = (row_s * LANES + col_s).astype(I32)
        at = aff_ref[e].T
        p0 = at.astype(BF16)
        r1 = at - p0.astype(F32)
        p1 = r1.astype(BF16)
        p2 = (r1 - p1.astype(F32)).astype(BF16)
        gate_ref[pl.ds(e, 1), :] = pick((p0, p1, p2), ohb, hit)
        d = (cum + rank_ref[e]).T
        d_hi = (d * (1.0 / 256.0)).astype(I32).astype(F32)
        d_lo = d - 256.0 * d_hi
        dst = 256.0 * pick((d_hi.astype(BF16),), ohb, hit) + pick((d_lo.astype(BF16),), ohb, hit)
        dst_ref[pl.ds(e, 1), :] = dst.astype(I32)
        return carry

    lax.fori_loop(0, E, compact, 0)


def _topk(aff_t, cap):
    E, T = aff_t.shape
    R = T // LANES
    u = jnp.asarray(np.triu(np.ones((LANES, LANES))), BF16)
    ut = jnp.asarray(np.tril(np.ones((LANES, LANES))), BF16)
    ones = jnp.ones((LANES, LANES), BF16)
    ls = jnp.asarray(np.tril(np.ones((R, R)), -1), BF16)
    z2 = lambda: (0, 0)
    z3 = lambda: (0, 0, 0)
    slot = jax.ShapeDtypeStruct((E, cap), I32)
    tok = jax.ShapeDtypeStruct((R, LANES), I32)
    idx, gate, dst, cum, cnt = pl.pallas_call(
        functools.partial(_topk_kernel, cap=cap),
        out_shape=(slot, jax.ShapeDtypeStruct((E, cap), F32), slot, tok, tok),
        in_specs=[pl.BlockSpec((E, R, LANES), z3), pl.BlockSpec((LANES, LANES), z2), pl.BlockSpec((LANES, LANES), z2),
                  pl.BlockSpec((LANES, LANES), z2), pl.BlockSpec((R, R), z2)],
        out_specs=(pl.BlockSpec((E, cap), z2), pl.BlockSpec((E, cap), z2), pl.BlockSpec((E, cap), z2),
                   pl.BlockSpec((R, LANES), z2), pl.BlockSpec((R, LANES), z2)),
        scratch_shapes=[pltpu.VMEM((E, 8, LANES), I32), pltpu.VMEM((E, R, LANES), F32),
                        pltpu.VMEM((E, R, LANES), F32)],
        compiler_params=pltpu.CompilerParams(vmem_limit_bytes=VMEM_LIMIT),
        name="expert_topk",
    )(aff_t.reshape(E, R, LANES), u, ut, ones, ls)
    return idx, gate, dst, cum.reshape(T), cnt.reshape(T)


ROW_CHUNKS = 2
FF_TILE = 256


def _expert_kernel(src_ref, dst_ref, h_ref, wg_ref, wu_ref, wd_ref, gate_ref, out_ref,
                   xbuf, xb, acc, gsem, ssem, *, cap, per):
    e = pl.program_id(0)
    f = pl.program_id(1)
    ne = pl.num_programs(0)
    nf = pl.num_programs(1)
    slot = e % 2
    other = 1 - slot
    sched = nf * per

    def gather_row(entry, r):
        pltpu.make_async_copy(h_ref.at[pl.ds(src_ref[entry], 1), :], xbuf.at[pl.ds(r, 1), :], gsem).start()

    def scatter_row(src_slot, entry, r, priority=0):
        pltpu.make_async_copy(acc.at[src_slot, pl.ds(r, 1), :], out_ref.at[pl.ds(dst_ref[entry], 1), :],
                              ssem.at[src_slot]).start(priority)

    def schedule_done(sem):
        pltpu.make_async_copy(h_ref.at[pl.ds(0, sched), :], xbuf, sem).wait()

    @pl.when((e == 0) & (f == 0))
    def _():
        def body(s, c):
            gather_row(s, s)
            return c

        lax.fori_loop(0, sched, body, 0, unroll=8)
        acc[...] = jnp.zeros(acc.shape, F32)

    @pl.when((e > 0) & (f == 0))
    def _():
        schedule_done(ssem.at[slot])
        acc[slot, pl.ds(0, cap), :] = jnp.zeros((cap, acc.shape[2]), F32)

    @pl.when(f == 0)
    def _():
        schedule_done(gsem)
        xb[...] = xbuf[pl.ds(0, cap), :].astype(BF16)

    wg = wg_ref[0, 0].astype(BF16)
    wu = wu_ref[0, 0].astype(BF16)
    wd = wd_ref[0, 0].astype(BF16)
    gbase = (jnp.minimum(e + 1, ne - 1) * nf + f) * per
    sbase = (e * nf + f) * per
    row0 = f * per
    share = per // ROW_CHUNKS
    chunk = cap // ROW_CHUNKS
    for c in range(ROW_CHUNKS):
        for s in range(c * share, (c + 1) * share):
            gather_row(gbase + s, row0 + s)
            scatter_row(other, sbase + s, row0 + s, s % 2)
        rows = pl.ds(c * chunk, chunk)
        x = xb[rows, :]
        g = _dot(x, wg)
        u = _dot(x, wu)
        hid = (g * jax.nn.sigmoid(g) * u).astype(BF16)
        acc[slot, rows, :] += _dot(hid, wd)

    @pl.when(f == nf - 1)
    def _():
        acc[slot, pl.ds(0, cap), :] = gate_ref[0] * acc[slot, pl.ds(0, cap), :]

    @pl.when((f == nf - 1) & (e == ne - 1))
    def _():
        def body(s, c):
            scatter_row(slot, ne * sched + s, s)
            return c

        lax.fori_loop(0, sched, body, 0, unroll=8)
        schedule_done(gsem)
        schedule_done(ssem.at[other])
        schedule_done(ssem.at[slot])


def _experts(h2, idx, dst, w_gate, w_up, w_down, layer, gate_col, cap):
    _, E, D, F = w_gate.shape
    ft = FF_TILE
    nf = F // ft
    per = -(-cap // (nf * 8 * ROW_CHUNKS)) * (8 * ROW_CHUNKS)
    sched = nf * per
    nr = E * cap
    assert E >= 2 and cap % ROW_CHUNKS == 0 and sched >= cap
    rows = jnp.arange(sched, dtype=I32)
    real = rows < cap
    rc = jnp.minimum(rows, cap - 1)
    src_tab = idx[:, rc]
    odd = (jnp.arange(E + 1, dtype=I32)[:, None] % 2) == 1
    spare = jnp.where(odd, nr + sched + (rows[None, :] - cap), nr + rows[None, :])
    dst_prev = jnp.concatenate([jnp.zeros((1, sched), I32), dst[:, rc]], axis=0)
    dst_tab = jnp.where(real[None, :] & (jnp.arange(E + 1)[:, None] > 0), dst_prev, spare)
    return pl.pallas_call(
        functools.partial(_expert_kernel, cap=cap, per=per),
        out_shape=jax.ShapeDtypeStruct((nr + 2 * sched - cap, D), F32),
        grid_spec=pltpu.PrefetchScalarGridSpec(
            num_scalar_prefetch=2,
            grid=(E, nf),
            in_specs=[pl.BlockSpec(memory_space=pl.ANY),
                      pl.BlockSpec((1, 1, D, ft), lambda e, f, i, d: (layer, e, 0, f)),
                      pl.BlockSpec((1, 1, D, ft), lambda e, f, i, d: (layer, e, 0, f)),
                      pl.BlockSpec((1, 1, ft, D), lambda e, f, i, d: (layer, e, f, 0)),
                      pl.BlockSpec((1, cap, 1), lambda e, f, i, d: (e, 0, 0))],
            out_specs=pl.BlockSpec(memory_space=pl.ANY),
            scratch_shapes=[pltpu.VMEM((sched, D), F32), pltpu.VMEM((cap, D), BF16), pltpu.VMEM((2, sched, D), F32),
                            pltpu.SemaphoreType.DMA, pltpu.SemaphoreType.DMA((2,))]),
        compiler_params=_params("arbitrary", "arbitrary"),
        name="expert_ffn",
    )(src_tab.reshape(-1), dst_tab.reshape(-1), h2, w_gate, w_up, w_down, gate_col)


COMBINE_ROWS = 512
COMBINE_TOKENS = 256


def _combine_kernel(tile_ref, blk_ref, flag_ref, c_ref, x_ref, mod_ref, tab_ref, eye_ref, o_ref):
    j = pl.program_id(0)
    flag = flag_ref[j]
    first = (flag & 1) != 0
    last = (flag & 2) != 0
    skip = (flag & 4) != 0
    rb = c_ref.shape[0]
    cols = _dot_nt(eye_ref[...], tab_ref[0].astype(BF16))
    start = cols[:, 0:1] * 256.0 + cols[:, 1:2]
    count = cols[:, 2:3]
    rabs = (blk_ref[j] * rb + lax.broadcasted_iota(I32, (1, rb), 1)).astype(F32)
    seg = jnp.where((start <= rabs) & (rabs < start + count), 1.0, 0.0).astype(BF16)
    hi, lo = _split(c_ref[...])
    part = _dot(seg, hi) + _dot(seg, lo)

    @pl.when(first)
    def _():
        o_ref[...] = part

    @pl.when(jnp.logical_not(first | skip))
    def _():
        o_ref[...] += part

    @pl.when(last & jnp.logical_not(skip))
    def _():
        o_ref[...] = x_ref[...] + mod_ref[0][5:6] * o_ref[...]


def _combine(x, mod_l, b0, L, contrib, cum, cnt):
    T, D = x.shape
    tt = COMBINE_TOKENS
    R = T // tt
    rb = COMBINE_ROWS
    nr = CAPACITY_FACTOR * T
    nb = nr // rb
    nper = L // tt
    nitem = R + nb
    lo = cum[::tt]
    hi = jnp.concatenate([lo[1:], jnp.full((1,), nr, I32)])
    b_lo = jnp.minimum(lo // rb, nb - 1)
    b_hi = jnp.minimum(jnp.maximum((hi - 1) // rb, b_lo), nb - 1)
    n = b_hi - b_lo + 1
    cn = jnp.cumsum(n)
    start = cn - n
    j = jnp.arange(nitem, dtype=I32)
    valid = j < cn[-1]
    ti = jnp.minimum(jnp.searchsorted(cn, j, side='right').astype(I32), R - 1)
    blk = jnp.where(valid, b_lo[ti] + (j - start[ti]), b_hi[R - 1])
    flag = jnp.where(valid, (j == start[ti]) * 1 + (j == cn[ti] - 1) * 2, 4).astype(I32)
    tab = jnp.stack([(cum >> 8).astype(F32), (cum & 255).astype(F32), cnt.astype(F32)], axis=0)
    tab = jnp.concatenate([tab, jnp.zeros((5, T), F32)], axis=0).reshape(8, R, tt).transpose(1, 0, 2)
    eye = jnp.eye(tt, dtype=BF16)
    return pl.pallas_call(
        _combine_kernel,
        out_shape=jax.ShapeDtypeStruct((T, D), F32),
        grid_spec=pltpu.PrefetchScalarGridSpec(
            num_scalar_prefetch=3,
            grid=(nitem,),
            in_specs=[pl.BlockSpec((rb, D), lambda j, t, b, f: (b[j], 0)),
                      pl.BlockSpec((tt, D), lambda j, t, b, f: (t[j], 0)),
                      pl.BlockSpec((1, 6, D), lambda j, t, b, f: (b0 + t[j] // nper, 0, 0)),
                      pl.BlockSpec((1, 8, tt), lambda j, t, b, f: (t[j], 0, 0)),
                      pl.BlockSpec((tt, tt), lambda j, t, b, f: (0, 0))],
            out_specs=pl.BlockSpec((tt, D), lambda j, t, b, f: (t[j], 0))),
        compiler_params=_params("arbitrary"),
        name="moe_combine",
    )(ti, blk, flag, contrib, x, mod_l, tab, eye)


def _layer(x, B, L, b0, mod_l, p, kf, fc, tabs):
    hy_w = p['hy_bias'].shape[1]
    att_w = N_HEADS * HEAD_DIM
    kv_w = N_KV_HEADS * HEAD_DIM
    planes = 2 if B % 2 == 0 else 1
    z, ga, gb, q, k, v, gs = _inproj(x, mod_l, b0, L, p['norm1_g'][None], p['w_in_bf'], p['hy_conv_w'], p['hy_conv_b'],
                                     tabs['cos'], tabs['sin'], tabs['qg'], tabs['kg'], tabs['blk'], hy_w, att_w, kv_w)
    shape5 = (B // planes, planes, L // LANES, hy_w, LANES)
    z = z.reshape(shape5)
    gates = (ga.reshape(shape5), gb.reshape(shape5))
    for o in range(HY_ORDER):
        z = _longconv(z, gates[o], kf, o, p['bias_t'], fc)
    z = z.reshape(B * L // LANES, hy_w, LANES)
    att = _attention(q, k, v, p['sink_logit'], B, L)
    x, h2, aff_t = _merge(x, z, att, gs, mod_l, b0, L, p['w_hy_bf'], p['w_at_x'], p['w_o_bf'], p['norm2_g'][None],
                          p['rw_hi'], p['rw_lo'])
    T = x.shape[0]
    cap = CAPACITY_FACTOR * T // N_EXPERTS
    idx, gate, dst, cum, cnt = _topk(aff_t, cap)
    contrib = _experts(h2, idx, dst, p['exp_w_gate'], p['exp_w_up'], p['exp_w_down'], p['layer'],
                       gate.reshape(N_EXPERTS, cap, 1), cap)
    return _combine(x, mod_l, b0, L, contrib, cum, cnt)


def _rope_tables(L):
    inv = 1.0 / (ROPE_THETA ** (jnp.arange(0, HEAD_DIM, 2, dtype=F32) / HEAD_DIM))
    ang = jnp.arange(L, dtype=F32)[:, None] * inv[None, :]
    cos, sin = jnp.cos(ang), jnp.sin(ang)
    cos_t = jnp.tile(cos, (1, LANES // (HEAD_DIM // 2)))
    sin_t = jnp.tile(jnp.concatenate([-sin, sin], axis=1), (1, LANES // HEAD_DIM))
    return cos_t, sin_t


def _filter_spectrum_for(L, p, fc):
    hy_w = p['hy_bias'].shape[1]
    k3, ssq = _filter_rows(L, p)
    return _filter_spectrum(k3, ssq, fc, hy_w)


def kernel(x_prompt, x_sample, c_prompt, c_sample, ada_w, ada_b, norm1_g, norm2_g, w_in, hy_conv_w, hy_conv_b,
           flt_w1, flt_b1, flt_w2, flt_b2, flt_w3, flt_b3, flt_freq, flt_w_out, hy_bias, q_norm_g, k_norm_g,
           sink_logit, w_hy_proj, w_at_proj, w_o, router_w, exp_w_gate, exp_w_up, exp_w_down):
    depth = ada_w.shape[0]
    Bp, Lp, D = x_prompt.shape
    Bs, Ls, _ = x_sample.shape
    nb = Bp + Bs
    c_all = jnp.concatenate([c_prompt, c_sample, jnp.zeros(((-nb) % 8, D), F32)], axis=0)
    mod = _adaln(c_all, ada_w, ada_b).reshape(depth, c_all.shape[0], 6, D)
    hy_w = hy_bias.shape[2]
    att_w = N_HEADS * HEAD_DIM
    groups = [dict(B=Bp, L=Lp, b0=0, x=x_prompt.reshape(Bp * Lp, D)),
              dict(B=Bs, L=Ls, b0=Bp, x=x_sample.reshape(Bs * Ls, D))]
    blk = jnp.asarray(np.kron(np.eye(LANES // HEAD_DIM), np.ones((HEAD_DIM, HEAD_DIM))), BF16)
    for g in groups:
        g['fc'] = _fft_consts(g['L'], 2 if g['B'] % 2 == 0 else 1)
        cos_t, sin_t = _rope_tables(g['L'])
        g['tabs'] = dict(cos=cos_t, sin=sin_t, blk=blk)
    heads_per_kv = N_HEADS // N_KV_HEADS
    for l in range(depth):
        w_at_x = jnp.zeros((N_HEADS, LANES, D), BF16)
        w_at_h = w_at_proj[l].astype(BF16).reshape(N_HEADS, HEAD_DIM, D)
        for h in range(N_HEADS):
            off = (h // heads_per_kv) * HEAD_DIM
            w_at_x = w_at_x.at[h, off:off + HEAD_DIM].set(w_at_h[h])
        rw_hi, rw_lo = _split(router_w[l].T)
        p = dict(norm1_g=norm1_g[l], norm2_g=norm2_g[l], w_in_bf=w_in[l].astype(BF16),
                 hy_conv_w=hy_conv_w[l], hy_conv_b=hy_conv_b[l],
                 flt_w1=flt_w1[l], flt_b1=flt_b1[l], flt_w2=flt_w2[l], flt_b2=flt_b2[l], flt_w3=flt_w3[l],
                 flt_b3=flt_b3[l], flt_freq=flt_freq[l], flt_w_out=flt_w_out[l], hy_bias=hy_bias[l],
                 bias_t=jnp.broadcast_to(hy_bias[l].reshape(-1, 1, CONV_CH), (HY_ORDER * hy_w // CONV_CH, 8, CONV_CH)),
                 sink_logit=sink_logit[l], w_hy_bf=w_hy_proj[l].astype(BF16),
                 w_at_x=w_at_x.reshape(N_HEADS * LANES, D), w_o_bf=w_o[l].astype(BF16),
                 rw_hi=rw_hi, rw_lo=rw_lo,
                 layer=l, exp_w_gate=exp_w_gate, exp_w_up=exp_w_up, exp_w_down=exp_w_down)
        qg = jnp.tile(q_norm_g[l], LANES // HEAD_DIM)[None]
        kg = jnp.tile(k_norm_g[l], LANES // HEAD_DIM)[None]
        for g in groups:
            kf = _filter_spectrum_for(g['L'], p, g['fc'])
            tabs = dict(g['tabs'], qg=qg, kg=kg)
            g['x'] = _layer(g['x'], g['B'], g['L'], g['b0'], mod[l], p, kf, g['fc'], tabs)
    return (groups[0]['x'].reshape(Bp, Lp, D), groups[1]['x'].reshape(Bs, Ls, D))
```
